```python
import jax, jax.numpy as jnp
from jax import lax
import numpy as np

D_MODEL = 2048
BATCH = 2
SEQ = 16384
DEPTH = 2
DEC_BATCH = 16
DEC_SEQ = 64
PAST_LEN = 2048

CHUNK = 64
N_BRANCH = 4
BRANCH_WIDTH = D_MODEL // N_BRANCH
A_HEAD = 64
A_HEADS = BRANCH_WIDTH // A_HEAD
A_DECAY_LORA = 32
A_ICL_LORA = 32
A_GATE_LORA = 96
A_IN = 3 * BRANCH_WIDTH + A_DECAY_LORA + A_ICL_LORA + A_GATE_LORA
A_GN_EPS = 64e-5
B_WINDOWS = (2, 4, 8, 16)
B_GROUPS = len(B_WINDOWS)
B_GROUP_WIDTH = BRANCH_WIDTH // B_GROUPS
B_HIST = max(B_WINDOWS) - 1
POOL_BLOCK = 128
C_HEAD = 64
C_HEADS = BRANCH_WIDTH // C_HEAD
C_IN = 3 * BRANCH_WIDTH
SB_BLOCK = 128
D_HEADS = 4
D_HEAD = BRANCH_WIDTH // D_HEADS
D_CONV = 4
D_IN = 4 * BRANCH_WIDTH + 2 * D_HEADS
D_NORM_EPS = 1e-6
GATE_RANK = 256
OFF_B = A_IN
OFF_C = OFF_B + BRANCH_WIDTH
OFF_D = OFF_C + C_IN
OFF_G = OFF_D + D_IN
IN_WIDTH = OFF_G + GATE_RANK
D_FF = 256 * ((8 * D_MODEL // 3 + 255) // 256)
N_EXPERTS = 8
TOP_K = 2
D_FF_EXPERT = D_FF // N_EXPERTS
N_DENSE = (DEPTH + 1) // 2
N_MOE = DEPTH // 2
NORM_EPS = 1e-6
F32 = jnp.float32

kernel_name = 'hybrid_rwkv7_pool_stickbreak_mlstm_step'


def rmsnorm(x, g):
    xf = x.astype(F32)
    y = xf * lax.rsqrt(jnp.mean(xf * xf, axis=-1, keepdims=True) + NORM_EPS)
    return (y * g.astype(F32)).astype(x.dtype)


def head_standardize(x, eps):
    mu = jnp.mean(x, axis=-1, keepdims=True)
    var = jnp.mean(jnp.square(x - mu), axis=-1, keepdims=True)
    return (x - mu) * lax.rsqrt(var + eps)


def token_shift(u, prev_row, mu):
    prev = jnp.concatenate([prev_row[:, None, :], u[:, :-1]], axis=1)
    return u + (prev - u) * mu


def rwkv7_recurrence(r, decay, k, v, kk, a, s0):
    def step(s, inp):
        r_t, w_t, k_t, v_t, kk_t, a_t = inp
        sa = jnp.einsum('bhvk,bhk->bhv', s, -kk_t)
        left = jnp.stack([sa, v_t], axis=-1)
        right = jnp.stack([kk_t * a_t, k_t], axis=-1)
        s = s * w_t[:, :, None, :] + jnp.einsum('bhvr,bhkr->bhvk', left, right)
        return s, jnp.einsum('bhvk,bhk->bhv', s, r_t)
    xs = tuple(jnp.moveaxis(t, 1, 0) for t in (r, decay, k, v, kk, a))
    s, ys = lax.scan(step, s0, xs)
    return jnp.moveaxis(ys, 0, 1), s


def rwkv7_mixer(u, shift_prev, wkv0, mu, w0, w2, a0, a2, g2, k_k, k_a, r_k, ln_w, ln_b):
    Bn, T, _ = u.shape
    W = BRANCH_WIDTH
    xm = token_shift(u, shift_prev, mu).astype(F32)
    r, k, v = xm[..., :W], xm[..., W:2 * W], xm[..., 2 * W:3 * W]
    o1 = 3 * W
    o2 = o1 + A_DECAY_LORA
    o3 = o2 + A_ICL_LORA
    wl, al, gl = xm[..., o1:o2], xm[..., o2:o3], xm[..., o3:]
    w_log = -jax.nn.softplus(-(w0 + jnp.tanh(wl) @ w2)) - 0.5
    decay = jnp.exp(-jnp.exp(w_log))
    a = jax.nn.sigmoid(a0 + al @ a2)
    g = jax.nn.sigmoid(gl) @ g2
    heads = lambda t: t.reshape(Bn, T, A_HEADS, A_HEAD)
    kk = heads(k * k_k)
    kk = kk / jnp.maximum(jnp.sqrt(jnp.sum(kk * kk, axis=-1, keepdims=True)), 1e-12)
    k = k * (1.0 + (a - 1.0) * k_a)
    r_h, k_h, v_h, a_h = heads(r), heads(k), heads(v), heads(a)
    y, wkv = rwkv7_recurrence(r_h, heads(decay), k_h, v_h, kk, a_h, wkv0.astype(F32))
    y = head_standardize(y, A_GN_EPS).reshape(Bn, T, W) * ln_w + ln_b
    y = y + (jnp.sum(r_h * k_h * r_k, axis=-1, keepdims=True) * v_h).reshape(Bn, T, W)
    return y * g, u[:, -1], wkv


def pool_mixer(u, hist, start_pos, pool_w, pool_scale):
    Bn, T, W = u.shape
    ext = jnp.concatenate([hist, u], axis=1)
    L = POOL_BLOCK if T % POOL_BLOCK == 0 else T
    nb = T // L
    seg_len = L + B_HIST
    idx = (jnp.arange(nb) * L)[:, None] + jnp.arange(seg_len)[None, :]
    seg = ext.astype(F32)[:, idx]
    tri = jnp.tril(jnp.ones((seg_len, seg_len), F32))
    cs = jnp.einsum('ij,bnjc->bnic', tri, seg)
    cs = jnp.concatenate([jnp.zeros_like(cs[:, :, :1]), cs], axis=2)
    end = cs[:, :, B_HIST + 1:]
    pos = start_pos + jnp.arange(T)
    means = []
    for gi, w in enumerate(B_WINDOWS):
        sl = slice(gi * B_GROUP_WIDTH, (gi + 1) * B_GROUP_WIDTH)
        win_sum = (end[..., sl] - cs[:, :, B_HIST + 1 - w:B_HIST + 1 - w + L, sl]).reshape(Bn, T, B_GROUP_WIDTH)
        cnt = jnp.minimum(w, pos + 1).astype(F32)[None, :, None]
        means.append(win_sum / cnt)
    d = (jnp.concatenate(means, axis=-1) - u.astype(F32)).reshape(Bn, T, B_GROUPS, B_GROUP_WIDTH)
    y = jnp.einsum('btgc,gcd->btgd', d, pool_w).reshape(Bn, T, W) * pool_scale
    return y, ext[:, -B_HIST:]


def sb_prompt(q, k, v):
    Bn, S, H, d = q.shape
    nb = S // SB_BLOCK
    if nb % 2:
        pad = ((0, 0), (0, SB_BLOCK), (0, 0), (0, 0))
        q, k, v = jnp.pad(q, pad), jnp.pad(k, pad), jnp.pad(v, pad)
        nb += 1
    P = nb // 2
    blocks = lambda t: t.reshape(Bn, nb, SB_BLOCK, H, d)
    qb, kb, vb = blocks(q), blocks(k), blocks(v)
    p = jnp.arange(P)
    ar = jnp.arange(SB_BLOCK)
    later_mat = (ar[:, None] > ar[None, :]).astype(F32)
    scale = C_HEAD ** -0.5

    def step(carry, s):
        acc, o, o_lo = carry
        use_lo = s <= p
        q_idx = jnp.where(use_lo, p, nb - 1 - p)
        k_idx = jnp.where(use_lo, p - s, nb - 1 - p - (s - p - 1))
        reset = s == p + 1
        o_lo = jnp.where(reset[None, :, None, None, None], o, o_lo)
        o = jnp.where(reset[None, :, None, None, None], 0.0, o)
        acc = jnp.where(reset[None, None, :, None], 0.0, acc)
        qs, ks, vs = qb[:, q_idx], kb[:, k_idx], vb[:, k_idx]
        q_pos = q_idx[:, None] * SB_BLOCK + ar
        k_pos = k_idx[:, None] * SB_BLOCK + ar
        mask = k_pos[:, None, :] < q_pos[:, :, None]
        z = jnp.einsum('bpthd,bpshd->bhpts', qs, ks) * scale
        lf = jnp.where(mask, jax.nn.log_sigmoid(-z), 0.0)
        later = jnp.einsum('bhptj,js->bhpts', lf, later_mat)
        att = jnp.where(mask, jnp.exp(lf + z + later + acc[..., None]), 0.0)
        o = o + jnp.einsum('bhpts,bpshd->bpthd', att, vs)
        acc = acc + jnp.sum(lf, axis=-1)
        return (acc, o, o_lo), None

    acc0 = jnp.zeros((Bn, H, P, SB_BLOCK), F32)
    o0 = jnp.zeros((Bn, P, SB_BLOCK, H, d), F32)
    (_, o_hi, o_lo), _ = lax.scan(step, (acc0, o0, o0), jnp.arange(nb + 1))
    out = jnp.concatenate([o_lo, o_hi[:, ::-1]], axis=1).reshape(Bn, nb * SB_BLOCK, H * d)
    return out[:, :S]


def sb_cached(q, k, v, k_cache, v_cache):
    Bn, T, H, d = q.shape
    past = k_cache.shape[1]
    k_all = jnp.concatenate([k_cache, k], axis=1).astype(F32)
    v_all = jnp.concatenate([v_cache, v], axis=1).astype(F32)
    q_pos = past + jnp.arange(T)
    k_pos = jnp.arange(past + T)
    z = jnp.einsum('bqhd,bkhd->bhqk', q.astype(F32), k_all) * (C_HEAD ** -0.5)
    mask = k_pos[None, :] < q_pos[:, None]
    lf = jnp.where(mask, jax.nn.log_sigmoid(-z), 0.0)
    later = lax.cumsum(lf, axis=3, reverse=True) - lf
    att = jnp.where(mask, jnp.exp(lf + z + later), 0.0)
    return jnp.einsum('bhqk,bkhd->bqhd', att, v_all).reshape(Bn, T, H * d)


def causal_dwconv(u, hist, w, b):
    T = u.shape[1]
    ext = jnp.concatenate([hist, u], axis=1)
    y = b
    for i in range(D_CONV):
        y = y + ext[:, i:i + T] * w[i]
    return y, ext[:, -(D_CONV - 1):]


def mlstm_chunk_step(carry, inp):
    c, n, m = carry
    q, k, v, ig, lf = inp
    L = q.shape[1]
    b = jnp.cumsum(lf, axis=1).transpose(0, 2, 1)
    i_ = ig.transpose(0, 2, 1)
    causal = jnp.tril(jnp.ones((L, L), dtype=bool))
    log_d = jnp.where(causal, b[..., :, None] - b[..., None, :] + i_[..., None, :], -jnp.inf)
    m_inter = b + m[..., None]
    m_t = jnp.maximum(m_inter, jnp.max(log_d, axis=-1))
    dmat = jnp.exp(log_d - m_t[..., None])
    inter = jnp.exp(m_inter - m_t)
    w_qk = jnp.einsum('bthd,bshd->bhts', q, k) * dmat
    num = (jnp.einsum('bhts,bshv->bthv', w_qk, v)
           + jnp.einsum('bht,bthk,bhkv->bthv', inter, q, c))
    den = jnp.sum(w_qk, axis=-1) + inter * jnp.einsum('bthk,bhk->bht', q, n)
    den = jnp.maximum(jnp.abs(den), jnp.exp(-m_t))
    h = num / den.transpose(0, 2, 1)[..., None]
    m_new = m_t[..., -1]
    b_last = b[..., -1]
    carry_decay = jnp.exp(b_last + m - m_new)
    w_s = jnp.exp(b_last[..., None] - b + i_ - m_new[..., None])
    c_new = carry_decay[..., None, None] * c + jnp.einsum('bhs,bshk,bshv->bhkv', w_s, k, v)
    n_new = carry_decay[..., None] * n + jnp.einsum('bhs,bshk->bhk', w_s, k)
    return (c_new, n_new, m_new), h


def mlstm_chunkwise(q, k, v, ig, lf, c0, n0, m0):
    Bn, T = q.shape[0], q.shape[1]
    L = min(CHUNK, T)
    nc = T // L
    split = lambda t: jnp.moveaxis(t.reshape((Bn, nc, L) + t.shape[2:]), 1, 0)
    (c, n, m), hs = lax.scan(mlstm_chunk_step, (c0, n0, m0), tuple(split(t) for t in (q, k, v, ig, lf)))
    return jnp.moveaxis(hs, 0, 1).reshape(Bn, T, D_HEADS, D_HEAD), c, n, m


def mlstm_mixer(u, conv_hist, c0, n0, m0, conv_w, conv_b, i_bias, f_bias, norm_w):
    Bn, T, _ = u.shape
    W = BRANCH_WIDTH
    qk, new_conv = causal_dwconv(u[..., :2 * W], conv_hist, conv_w, conv_b)
    qk = jax.nn.silu(qk.astype(F32))
    q = qk[..., :W].reshape(Bn, T, D_HEADS, D_HEAD)
    k = qk[..., W:].reshape(Bn, T, D_HEADS, D_HEAD) * (D_HEAD ** -0.5)
    uf = u.astype(F32)
    v = uf[..., 2 * W:3 * W].reshape(Bn, T, D_HEADS, D_HEAD)
    o = jax.nn.sigmoid(uf[..., 3 * W:4 * W])
    ig = uf[..., 4 * W:4 * W + D_HEADS] + i_bias
    lf = jax.nn.log_sigmoid(uf[..., 4 * W + D_HEADS:] + f_bias)
    h, c, n, m = mlstm_chunkwise(q, k, v, ig, lf, c0.astype(F32), n0.astype(F32), m0.astype(F32))
    h = head_standardize(h, D_NORM_EPS).reshape(Bn, T, W) * norm_w
    return o * h, new_conv, c, n, m


def swiglu(x, wg, wu, wd):
    return (jax.nn.silu(x @ wg) * (x @ wu)) @ wd


def moe_ffn(x, router, wg, wu, wd):
    logits = (x @ router).astype(F32)
    top_v, top_i = lax.top_k(logits, TOP_K)
    top_w = jax.nn.softmax(top_v, axis=-1)
    gates = jnp.sum(jax.nn.one_hot(top_i, N_EXPERTS, dtype=F32) * top_w[..., None], axis=-2).astype(x.dtype)
    out = gates[..., 0:1] * swiglu(x, wg[0], wu[0], wd[0])
    for e in range(1, N_EXPERTS):
        out = out + gates[..., e:e + 1] * swiglu(x, wg[e], wu[e], wd[e])
    return out


def run_trunk(x, start_pos, cache_sb_k, cache_sb_v, wkv0, shift0, pool0, conv0, c0, n0, m0,
              norm_mix, norm_ffn, norm_final, w_in, rwkv_mu, rwkv_w0, rwkv_w2, rwkv_a0, rwkv_a2,
              rwkv_g2, rwkv_k_k, rwkv_k_a, rwkv_r_k, rwkv_ln_w, rwkv_ln_b, pool_w, pool_scale,
              mlstm_conv_w, mlstm_conv_b, mlstm_i_bias, mlstm_f_bias, mlstm_norm_w,
              w_branch, w_merge_gate, w_out, ffn_w_gate, ffn_w_up, ffn_w_down,
              moe_router, moe_w_gate, moe_w_up, moe_w_down):
    Bn, T, _ = x.shape
    W = BRANCH_WIDTH
    dt = x.dtype
    ks, vs, wkvs, shifts, pools, convs, cs, ns, ms = ([] for _ in range(9))
    for l in range(DEPTH):
        h = rmsnorm(x, norm_mix[l])
        u = h @ w_in[l]
        out_a, sh, wkv = rwkv7_mixer(u[..., :OFF_B], shift0[l], wkv0[l], rwkv_mu[l], rwkv_w0[l],
                                     rwkv_w2[l], rwkv_a0[l], rwkv_a2[l], rwkv_g2[l], rwkv_k_k[l],
                                     rwkv_k_a[l], rwkv_r_k[l], rwkv_ln_w[l], rwkv_ln_b[l])
        out_b, ph = pool_mixer(u[..., OFF_B:OFF_C], pool0[l], start_pos, pool_w[l], pool_scale[l])
        u_c = u[..., OFF_C:OFF_D]
        q = u_c[..., :W].reshape(Bn, T, C_HEADS, C_HEAD)
        k = u_c[..., W:2 * W].reshape(Bn, T, C_HEADS, C_HEAD)
        v = u_c[..., 2 * W:].reshape(Bn, T, C_HEADS, C_HEAD)
        if cache_sb_k is None:
            out_c = sb_prompt(q.astype(F32), k.astype(F32), v.astype(F32))
        else:
            out_c = sb_cached(q, k, v, cache_sb_k[l], cache_sb_v[l])
        out_d, cv, c, n, m = mlstm_mixer(u[..., OFF_D:OFF_G], conv0[l], c0[l], n0[l], m0[l], mlstm_conv_w[l],
                                         mlstm_conv_b[l], mlstm_i_bias[l], mlstm_f_bias[l], mlstm_norm_w[l])
        g_lat = u[..., OFF_G:]
        merged = None
        for i, br in enumerate((out_a, out_b, out_c, out_d)):
            term = jax.nn.sigmoid(g_lat @ w_merge_gate[l, i]) * (br.astype(dt) @ w_branch[l, i])
            merged = term if merged is None else merged + term
        x = x + merged @ w_out[l]
        h2 = rmsnorm(x, norm_ffn[l])
        if l % 2 == 0:
            x = x + swiglu(h2, ffn_w_gate[l // 2], ffn_w_up[l // 2], ffn_w_down[l // 2])
        else:
            x = x + moe_ffn(h2, moe_router[l // 2], moe_w_gate[l // 2], moe_w_up[l // 2], moe_w_down[l // 2])
        ks.append(k)
        vs.append(v)
        wkvs.append(wkv)
        shifts.append(sh)
        pools.append(ph)
        convs.append(cv)
        cs.append(c)
        ns.append(n)
        ms.append(m)
    y = rmsnorm(x, norm_final)
    st = lambda lst: jnp.stack(lst, axis=0)
    return y, (st(ks), st(vs), st(wkvs), st(shifts), st(pools), st(convs), st(cs), st(ns), st(ms))


def setup_inputs(seed: int = 0) -> dict:
    key = jax.random.key(seed)
    keys = jax.random.split(key, 48)
    nrm = lambda i, shape, s=1.0: s * jax.random.normal(keys[i], shape, F32)
    unif = lambda i, shape: jax.random.uniform(keys[i], shape, F32)
    L = DEPTH
    W = BRANCH_WIDTH
    D = D_MODEL
    return {
        'x_prompt': nrm(0, (BATCH, SEQ, D)),
        'x_sample': nrm(1, (DEC_BATCH, DEC_SEQ, D)),
        'cache_sb_k': nrm(2, (L, DEC_BATCH, PAST_LEN, C_HEADS, C_HEAD)),
        'cache_sb_v': nrm(3, (L, DEC_BATCH, PAST_LEN, C_HEADS, C_HEAD)),
        'state_rwkv_wkv': nrm(4, (L, DEC_BATCH, A_HEADS, A_HEAD, A_HEAD), 0.5),
        'state_rwkv_shift': nrm(5, (L, DEC_BATCH, A_IN)),
        'state_pool': nrm(6, (L, DEC_BATCH, B_HIST, W)),
        'state_mlstm_conv': nrm(7, (L, DEC_BATCH, D_CONV - 1, 2 * W)),
        'state_mlstm_c': nrm(8, (L, DEC_BATCH, D_HEADS, D_HEAD, D_HEAD), 0.1),
        'state_mlstm_n': nrm(9, (L, DEC_BATCH, D_HEADS, D_HEAD), 0.1),
        'state_mlstm_m': nrm(10, (L, DEC_BATCH, D_HEADS)),
        'norm_mix': 1.0 + nrm(11, (L, D), 0.02),
        'norm_ffn': 1.0 + nrm(12, (L, D), 0.02),
        'norm_final': 1.0 + nrm(13, (D,), 0.02),
        'w_in': nrm(14, (L, D, IN_WIDTH), D ** -0.5),
        'rwkv_mu': unif(15, (L, A_IN)),
        'rwkv_w0': -6.0 + 7.0 * unif(16, (L, W)),
        'rwkv_w2': nrm(17, (L, A_DECAY_LORA, W), 0.1 * A_DECAY_LORA ** -0.5),
        'rwkv_a0': nrm(18, (L, W), 0.1),
        'rwkv_a2': nrm(19, (L, A_ICL_LORA, W), 0.1 * A_ICL_LORA ** -0.5),
        'rwkv_g2': nrm(20, (L, A_GATE_LORA, W), A_GATE_LORA ** -0.5),
        'rwkv_k_k': 0.85 + nrm(21, (L, W), 0.02),
        'rwkv_k_a': 1.0 + nrm(22, (L, W), 0.02),
        'rwkv_r_k': -0.04 + nrm(23, (L, A_HEADS, A_HEAD), 0.02),
        'rwkv_ln_w': 1.0 + nrm(24, (L, W), 0.02),
        'rwkv_ln_b': nrm(25, (L, W), 0.02),
        'pool_w': nrm(26, (L, B_GROUPS, B_GROUP_WIDTH, B_GROUP_WIDTH), B_GROUP_WIDTH ** -0.5),
        'pool_scale': 1.0 + nrm(27, (L, W), 0.1),
        'mlstm_conv_w': nrm(28, (L, D_CONV, 2 * W), D_CONV ** -0.5),
        'mlstm_conv_b': nrm(29, (L, 2 * W), 0.02),
        'mlstm_i_bias': nrm(30, (L, D_HEADS), 0.1),
        'mlstm_f_bias': jnp.linspace(3.0, 6.0, D_HEADS)[None, :] + nrm(31, (L, D_HEADS), 0.1),
        'mlstm_norm_w': 1.0 + nrm(32, (L, W), 0.02),
        'w_branch': nrm(33, (L, N_BRANCH, W, D), W ** -0.5),
        'w_merge_gate': nrm(34, (L, N_BRANCH, GATE_RANK, D), GATE_RANK ** -0.5),
        'w_out': nrm(35, (L, D, D), D ** -0.5),
        'ffn_w_gate': nrm(36, (N_DENSE, D, D_FF), D ** -0.5),
        'ffn_w_up': nrm(37, (N_DENSE, D, D_FF), D ** -0.5),
        'ffn_w_down': nrm(38, (N_DENSE, D_FF, D), D_FF ** -0.5),
        'moe_router': nrm(39, (N_MOE, D, N_EXPERTS), D ** -0.5),
        'moe_w_gate': nrm(40, (N_MOE, N_EXPERTS, D, D_FF_EXPERT), D ** -0.5),
        'moe_w_up': nrm(41, (N_MOE, N_EXPERTS, D, D_FF_EXPERT), D ** -0.5),
        'moe_w_down': nrm(42, (N_MOE, N_EXPERTS, D_FF_EXPERT, D), D_FF_EXPERT ** -0.5),
    }


def reference(x_prompt, x_sample, cache_sb_k, cache_sb_v, state_rwkv_wkv, state_rwkv_shift,
              state_pool, state_mlstm_conv, state_mlstm_c, state_mlstm_n, state_mlstm_m,
              norm_mix, norm_ffn, norm_final, w_in, rwkv_mu, rwkv_w0, rwkv_w2, rwkv_a0, rwkv_a2,
              rwkv_g2, rwkv_k_k, rwkv_k_a, rwkv_r_k, rwkv_ln_w, rwkv_ln_b, pool_w, pool_scale,
              mlstm_conv_w, mlstm_conv_b, mlstm_i_bias, mlstm_f_bias, mlstm_norm_w,
              w_branch, w_merge_gate, w_out, ffn_w_gate, ffn_w_up, ffn_w_down,
              moe_router, moe_w_gate, moe_w_up, moe_w_down):
    weights = (norm_mix, norm_ffn, norm_final, w_in, rwkv_mu, rwkv_w0, rwkv_w2, rwkv_a0, rwkv_a2,
               rwkv_g2, rwkv_k_k, rwkv_k_a, rwkv_r_k, rwkv_ln_w, rwkv_ln_b, pool_w, pool_scale,
               mlstm_conv_w, mlstm_conv_b, mlstm_i_bias, mlstm_f_bias, mlstm_norm_w,
               w_branch, w_merge_gate, w_out, ffn_w_gate, ffn_w_up, ffn_w_down,
               moe_router, moe_w_gate, moe_w_up, moe_w_down)
    bp = x_prompt.shape[0]
    dt = x_prompt.dtype
    y_p, (k_p, v_p, wkv_p, sh_p, pool_p, conv_p, c_p, n_p, m_p) = run_trunk(
        x_prompt, 0, None, None,
        jnp.zeros((DEPTH, bp, A_HEADS, A_HEAD, A_HEAD), F32),
        jnp.zeros((DEPTH, bp, A_IN), dt),
        jnp.zeros((DEPTH, bp, B_HIST, BRANCH_WIDTH), dt),
        jnp.zeros((DEPTH, bp, D_CONV - 1, 2 * BRANCH_WIDTH), dt),
        jnp.zeros((DEPTH, bp, D_HEADS, D_HEAD, D_HEAD), F32),
        jnp.zeros((DEPTH, bp, D_HEADS, D_HEAD), F32),
        jnp.zeros((DEPTH, bp, D_HEADS), F32),
        *weights)
    y_s, (k_s, v_s, wkv_s, sh_s, pool_s, conv_s, c_s, n_s, m_s) = run_trunk(
        x_sample, cache_sb_k.shape[2], cache_sb_k, cache_sb_v, state_rwkv_wkv, state_rwkv_shift,
        state_pool, state_mlstm_conv, state_mlstm_c, state_mlstm_n, state_mlstm_m,
        *weights)
    return (y_p, y_s, k_p, v_p, wkv_p, sh_p, pool_p, conv_p, c_p, n_p, m_p,
            k_s, v_s, wkv_s, sh_s, pool_s, conv_s, c_s, n_s, m_s)
```

```python
import functools

import jax
import jax.numpy as jnp
from jax import lax
from jax.experimental import pallas as pl
from jax.experimental.pallas import tpu as pltpu

F32 = jnp.float32
BF16 = jnp.bfloat16

D_MODEL = 2048
DEPTH = 2
W = 512
A_HEAD, A_HEADS = 64, 8
A_LORA = (32, 32, 96)
A_IN = 3 * W + sum(A_LORA)
A_GN_EPS = 64e-5
B_WINDOWS = (2, 4, 8, 16)
B_HIST = 15
C_HEAD, C_HEADS = 64, 8
D_HEADS, D_HEAD = 4, 128
D_CONV = 4
D_NORM_EPS = 1e-6
GATE_RANK = 256
OFF_B = A_IN
OFF_C = OFF_B + W
OFF_D = OFF_C + 3 * W
OFF_G = OFF_D + 4 * W + 2 * D_HEADS
IN_WIDTH = OFF_G + GATE_RANK
D_FF = 5632
N_EXPERTS = 8
D_FF_EXPERT = D_FF // N_EXPERTS
NORM_EPS = 1e-6

LANE = 128
V7X_VMEM_BYTES = 64 * 1024 * 1024
VMEM_LIMIT = V7X_VMEM_BYTES - 8 * 1024 * 1024

SLOT = {name: i for i, name in enumerate(
    ("dq", "dk", "dv", "do", "ar", "ak", "av", "b", "cq", "ck", "cv"))}
COL_LORA = 11 * W
COL_G = COL_LORA + 256
COL_DG = COL_G + 256
IN_TN = 1280
WP = 6400
LORA_PAD = 256
EXPERT_PAD = 768

RWKV_CHUNK = 64
MLSTM_CHUNK = 64
SB_BLOCK = 128


def _params(sem, vmem=VMEM_LIMIT):
    return pltpu.CompilerParams(dimension_semantics=sem, vmem_limit_bytes=vmem)


def _split2(x):
    hi = x.astype(BF16)
    lo = (x - hi.astype(F32)).astype(BF16)
    return hi, lo


def _split3(x):
    hi = x.astype(BF16)
    r1 = x - hi.astype(F32)
    mid = r1.astype(BF16)
    lo = (r1 - mid.astype(F32)).astype(BF16)
    return hi, mid, lo


def _dot(a, b):
    return jnp.dot(a, b, preferred_element_type=F32)


def _dot_nt(a, b):
    return lax.dot_general(a, b, (((1,), (1,)), ((), ())), preferred_element_type=F32)


def _dot_rhs01(x, m01):
    hi, mid, lo = _split3(x)
    return _dot(hi, m01) + _dot(mid, m01) + _dot(lo, m01)


def _dot_lhs01(m01, x):
    hi, mid, lo = _split3(x)
    return _dot(m01, hi) + _dot(m01, mid) + _dot(m01, lo)


def _sigmoid(x):
    return 1.0 / (1.0 + jnp.exp(-x))


def _log_sigmoid(x):
    return jnp.minimum(x, 0.0) - jnp.log1p(jnp.exp(-jnp.abs(x)))


def _rmsnorm(x, g):
    ms = jnp.mean(x * x, axis=-1, keepdims=True)
    return x * lax.rsqrt(ms + NORM_EPS) * g


def _inproj_body(x_ref, g_ref, w_ref, o_ref, h_scr):
    @pl.when(pl.program_id(1) == 0)
    def _():
        h_scr[...] = _rmsnorm(x_ref[...], g_ref[...]).astype(BF16)

    o_ref[...] = _dot(h_scr[...], w_ref[...])


def _inproj(x2d, g, wp):
    n = x2d.shape[0]
    tm = min(512, n)
    return pl.pallas_call(
        _inproj_body,
        grid=(n // tm, WP // IN_TN),
        in_specs=[pl.BlockSpec((tm, D_MODEL), lambda i, j: (i, 0)),
                  pl.BlockSpec((1, D_MODEL), lambda i, j: (0, 0)),
                  pl.BlockSpec((D_MODEL, IN_TN), lambda i, j: (0, j))],
        out_specs=pl.BlockSpec((tm, IN_TN), lambda i, j: (i, j)),
        out_shape=jax.ShapeDtypeStruct((n, WP), F32),
        scratch_shapes=[pltpu.VMEM((tm, D_MODEL), BF16)],
        compiler_params=_params(("parallel", "arbitrary")),
        name="inproj",
    )(x2d, g.reshape(1, D_MODEL), wp)


def _shift_rows(u, carry_row):
    rolled = pltpu.roll(u, 1, axis=0)
    row = lax.broadcasted_iota(jnp.int32, u.shape, 0)
    return jnp.where(row == 0, carry_row, rolled)


def _rwkv_prep_body(ur_ref, uk_ref, uv_ref, ul_ref, sr_ref, sk_ref, sv_ref, sl_ref,
                    mu_ref, mul_ref, w0_ref, a0_ref, kk_ref, ka_ref,
                    w2_ref, a2_ref, g2_ref, ones_ref,
                    r_out, k_out, v_out, kk_out, b_out, lw_out, g_out,
                    cr, ck, cv, cl):
    tm = ur_ref.shape[0]

    @pl.when(pl.program_id(1) == 0)
    def _():
        cr[0:1, :] = sr_ref[...]
        ck[0:1, :] = sk_ref[...]
        cv[0:1, :] = sv_ref[...]
        cl[0:1, :] = sl_ref[...]

    def mix(u_ref, c_ref, mu):
        u = u_ref[...]
        prev = _shift_rows(u, c_ref[0:1, :])
        c_ref[0:1, :] = u[tm - 1:tm, :]
        return u + (prev - u) * mu

    r = mix(ur_ref, cr, mu_ref[0:1, :])
    k = mix(uk_ref, ck, mu_ref[1:2, :])
    v = mix(uv_ref, cv, mu_ref[2:3, :])
    xl = mix(ul_ref, cl, mul_ref[...])

    w_lora = _dot(jnp.tanh(xl).astype(BF16), w2_ref[...])
    a_lora = _dot(xl.astype(BF16), a2_ref[...])
    g = _dot(_sigmoid(xl).astype(BF16), g2_ref[...])

    y = -(w0_ref[...] + w_lora)
    w_log = -(jnp.maximum(y, 0.0) + jnp.log1p(jnp.exp(-jnp.abs(y)))) - 0.5
    log_decay = -jnp.exp(w_log)
    a = _sigmoid(a0_ref[...] + a_lora)

    kk = k * kk_ref[...]
    hi, lo = _split2(kk * kk)
    ss = _dot(hi, ones_ref[...]) + _dot(lo, ones_ref[...])
    kk = kk / jnp.maximum(jnp.sqrt(ss), 1e-12)
    k2 = k * (1.0 + (a - 1.0) * ka_ref[...])

    r_out[...] = r
    k_out[...] = k2
    v_out[...] = v
    kk_out[...] = kk
    b_out[...] = kk * a
    lw_out[...] = log_decay
    g_out[...] = g


def _rwkv_prep(u3, shift_prev, p, tm):
    bn, t, _ = u3.shape
    sr = shift_prev[:, None, 0:W]
    sk = shift_prev[:, None, W:2 * W]
    sv = shift_prev[:, None, 2 * W:3 * W]
    sl = jnp.pad(shift_prev[:, None, 3 * W:], ((0, 0), (0, 0), (0, LORA_PAD - sum(A_LORA))))
    row = lambda c: pl.BlockSpec((None, 1, c), lambda b, i: (b, 0, 0))
    full = lambda a: pl.BlockSpec(a.shape, lambda b, i: (0,) * a.ndim)
    ublk = lambda slot: pl.BlockSpec((None, tm, W), lambda b, i, s=slot: (b, i, s))
    consts = (p["mu_rkv"], p["mu_l"], p["w0"], p["a0"], p["k_k"], p["k_a"],
              p["w2p"], p["a2p"], p["g2p"], p["head_ones"])
    out = jax.ShapeDtypeStruct((bn, t, W), F32)
    oblk = pl.BlockSpec((None, tm, W), lambda b, i: (b, i, 0))
    return pl.pallas_call(
        _rwkv_prep_body,
        grid=(bn, t // tm),
        in_specs=[ublk(SLOT["ar"]), ublk(SLOT["ak"]), ublk(SLOT["av"]),
                  pl.BlockSpec((None, tm, LORA_PAD), lambda b, i: (b, i, COL_LORA // LORA_PAD)),
                  row(W), row(W), row(W), row(LORA_PAD)] + [full(c) for c in consts],
        out_specs=[oblk] * 7,
        out_shape=[out] * 7,
        scratch_shapes=[pltpu.VMEM((8, W), F32)] * 3 + [pltpu.VMEM((8, LORA_PAD), F32)],
        compiler_params=_params(("parallel", "arbitrary")),
        name="rwkv_prep",
    )(u3, u3, u3, u3, sr, sk, sv, sl, *consts)


def _rwkv_chunk_body(r_ref, k_ref, v_ref, kk_ref, b_ref, lw_ref, s0_ref,
                     y_ref, sout_ref, s_scr, *, n_chunks):
    c_len = RWKV_CHUNK

    @pl.when(pl.program_id(1) == 0)
    def _():
        s_scr[...] = s0_ref[...]

    row = lax.broadcasted_iota(jnp.int32, (c_len, c_len), 0)
    col = lax.broadcasted_iota(jnp.int32, (c_len, c_len), 1)
    tri_incl = (row >= col).astype(BF16)
    lower = row >= col
    strict = row > col
    eye = (row == col).astype(F32)

    def chunk(c, carry):
        off = pl.multiple_of(c * c_len, c_len)
        sl = pl.ds(off, c_len)
        for h in range(A_HEADS):
            r = r_ref[h, sl, :]
            k = k_ref[h, sl, :]
            v = v_ref[h, sl, :]
            kk = kk_ref[h, sl, :]
            b = b_ref[h, sl, :]
            logw = lw_ref[h, sl, :]
            lw = _dot_lhs01(tri_incl, logw)
            lw_prev = lw - logw
            lw_last = lw[c_len - 1:c_len, :]
            e_neg = jnp.exp(-lw)
            e_end = jnp.exp(lw_last - lw)
            kkm = (kk * jnp.exp(lw_prev)).astype(BF16)
            rm = r * jnp.exp(lw)
            kp = (k * e_neg).astype(BF16)
            bp = (b * e_neg).astype(BF16)
            kpp = (k * e_end).astype(BF16)
            bpp = (b * e_end).astype(BF16)
            vb = v.astype(BF16)
            rmb = rm.astype(BF16)

            a_vk = jnp.where(strict, _dot_nt(kkm, kp), 0.0)
            a_pb = jnp.where(strict, _dot_nt(kkm, bp), 0.0)
            rk = jnp.where(lower, _dot_nt(rmb, kp), 0.0)
            rb = jnp.where(lower, _dot_nt(rmb, bp), 0.0)

            n_pow = -a_pb
            t_inv = eye + n_pow
            for _ in range(5):
                nb = n_pow.astype(BF16)
                n_pow = _dot(nb, nb)
                t_inv = t_inv + _dot(t_inv.astype(BF16), n_pow.astype(BF16))
            tb = t_inv.astype(BF16)

            kkt = _dot(tb, kkm)
            pv = _dot(tb, _dot(a_vk.astype(BF16), vb).astype(BF16))
            kktb = kkt.astype(BF16)
            pvb = pv.astype(BF16)
            rbb = rb.astype(BF16)
            m_mat = eye * jnp.exp(lw_last) - _dot(kkt.T.astype(BF16), bpp)
            n_mat = _dot(v.T.astype(BF16), kpp) - _dot(pv.T.astype(BF16), bpp)
            q_mat = rm - _dot(rbb, kktb)
            y0 = _dot(rk.astype(BF16), vb) - _dot(rbb, pvb)

            s = s_scr[h]
            s_hi, s_lo = _split2(s)
            m_hi, m_lo = _split2(m_mat)
            y_ref[h, sl, :] = _dot_nt(q_mat.astype(BF16), s_hi) + y0
            s_scr[h] = _dot(s_hi, m_hi) + _dot(s_lo, m_hi) + _dot(s_hi, m_lo) + n_mat
        return carry

    lax.fori_loop(0, n_chunks, chunk, 0)
    sout_ref[...] = s_scr[...]


def _rwkv_chunk(r, k, v, kk, b, lw, s0, tm):
    bn, hh, t, d = r.shape
    blk = pl.BlockSpec((None, hh, tm, d), lambda bi, i: (bi, 0, i, 0))
    sblk = pl.BlockSpec((None, hh, d, d), lambda bi, i: (bi, 0, 0, 0))
    return pl.pallas_call(
        functools.partial(_rwkv_chunk_body, n_chunks=tm // RWKV_CHUNK),
        grid=(bn, t // tm),
        in_specs=[blk] * 6 + [sblk],
        out_specs=[blk, sblk],
        out_shape=[jax.ShapeDtypeStruct((bn, hh, t, d), F32),
                   jax.ShapeDtypeStruct((bn, hh, d, d), F32)],
        scratch_shapes=[pltpu.VMEM((hh, d, d), F32)],
        compiler_params=_params(("parallel", "arbitrary")),
        name="rwkv_chunk",
    )(r, k, v, kk, b, lw, s0)


def _rwkv_post_body(y_ref, r_ref, k_ref, v_ref, g_ref, rk_ref, lnw_ref, lnb_ref, ones_ref, o_ref):
    y = y_ref[...]
    ones = ones_ref[...]

    def head_sum(x):
        hi, lo = _split2(x)
        return _dot(hi, ones) + _dot(lo, ones)

    mu = head_sum(y) * (1.0 / A_HEAD)
    yc = y - mu
    var = head_sum(yc * yc) * (1.0 / A_HEAD)
    yn = yc * lax.rsqrt(var + A_GN_EPS) * lnw_ref[...] + lnb_ref[...]
    bonus = head_sum(r_ref[...] * k_ref[...] * rk_ref[...]) * v_ref[...]
    o_ref[...] = ((yn + bonus) * g_ref[...]).astype(BF16)


def _rwkv_post(y, r, k, v, g, p, tm):
    n = y.shape[0]
    blk = pl.BlockSpec((tm, W), lambda i: (i, 0))
    full = lambda a: pl.BlockSpec(a.shape, lambda i: (0,) * a.ndim)
    consts = (p["r_k"], p["ln_w"], p["ln_b"], p["head_ones"])
    return pl.pallas_call(
        _rwkv_post_body,
        grid=(n // tm,),
        in_specs=[blk] * 5 + [full(c) for c in consts],
        out_specs=blk,
        out_shape=jax.ShapeDtypeStruct((n, W), BF16),
        compiler_params=_params(("parallel",)),
        name="rwkv_post",
    )(y, r, k, v, g, *consts)


def _pool_body(u_ref, hist_ref, w_ref, scale_ref, o_ref, carry, *, start_pos):
    tm = u_ref.shape[0]
    i = pl.program_id(1)

    @pl.when(i == 0)
    def _():
        carry[...] = hist_ref[...]

    x = u_ref[...]
    ext = jnp.concatenate([carry[...], x], axis=0)
    carry[...] = x[tm - 16:tm, :]
    s = ext
    sums = []
    for sh in (1, 2, 4, 8):
        s = s + pltpu.roll(s, sh, axis=0)
        sums.append(s[16:16 + tm, :])
    pos = start_pos + i * tm + lax.broadcasted_iota(jnp.int32, (tm, 1), 0)
    for gi, wlen in enumerate(B_WINDOWS):
        cs = slice(gi * LANE, (gi + 1) * LANE)
        cnt = jnp.minimum(wlen, pos + 1).astype(F32)
        d = sums[gi][:, cs] / cnt - x[:, cs]
        o_ref[:, cs] = (_dot(d.astype(BF16), w_ref[gi]) * scale_ref[:, cs]).astype(BF16)


def _pool(u3, hist, pool_w, pool_scale, start_pos, tm):
    bn, t, _ = u3.shape
    hist16 = jnp.pad(hist, ((0, 0), (1, 0), (0, 0)))
    return pl.pallas_call(
        functools.partial(_pool_body, start_pos=start_pos),
        grid=(bn, t // tm),
        in_specs=[pl.BlockSpec((None, tm, W), lambda b, i: (b, i, SLOT["b"])),
                  pl.BlockSpec((None, 16, W), lambda b, i: (b, 0, 0)),
                  pl.BlockSpec((4, LANE, LANE), lambda b, i: (0, 0, 0)),
                  pl.BlockSpec((1, W), lambda b, i: (0, 0))],
        out_specs=pl.BlockSpec((None, tm, W), lambda b, i: (b, i, 0)),
        out_shape=jax.ShapeDtypeStruct((bn, t, W), BF16),
        scratch_shapes=[pltpu.VMEM((16, W), F32)],
        compiler_params=_params(("parallel", "arbitrary")),
        name="pool",
    )(u3, hist16, pool_w.astype(BF16), pool_scale.reshape(1, W))


def _sb_body(q_ref, k_ref, v_ref, o_ref, *, tq, q_off, n_masked):
    qi = pl.program_id(2)
    q = q_ref[...]
    q_start = q_off + qi * tq
    first_masked = q_start // SB_BLOCK
    scale = C_HEAD ** -0.5

    jj = lax.broadcasted_iota(jnp.int32, (SB_BLOCK, 2 * SB_BLOCK), 0)
    ss = lax.broadcasted_iota(jnp.int32, (SB_BLOCK, 2 * SB_BLOCK), 1)
    cs = jnp.logical_or(ss >= SB_BLOCK, jj > ss).astype(BF16)
    q_pos = q_start + lax.broadcasted_iota(jnp.int32, (tq, SB_BLOCK), 0)
    k_lane = lax.broadcasted_iota(jnp.int32, (tq, SB_BLOCK), 1)

    def block(kb, acc, o, masked):
        ks = pl.multiple_of(kb * SB_BLOCK, SB_BLOCK)
        kblk = k_ref[pl.ds(ks, SB_BLOCK), :]
        vblk = v_ref[pl.ds(ks, SB_BLOCK), :]
        z = _dot_nt(q, kblk) * scale
        lf = _log_sigmoid(-z)
        if masked:
            mask = (ks + k_lane) < q_pos
            lf = jnp.where(mask, lf, 0.0)
        hi, lo = _split2(lf)
        lt = _dot(hi, cs) + _dot(lo, cs)
        att = jnp.exp(lf + z + lt[:, :SB_BLOCK] + acc)
        if masked:
            att = jnp.where(mask, att, 0.0)
        o = o + _dot(att.astype(BF16), vblk)
        return acc + lt[:, SB_BLOCK:], o

    acc = jnp.zeros((tq, SB_BLOCK), F32)
    o = jnp.zeros((tq, C_HEAD), F32)
    for m in range(n_masked - 1, -1, -1):
        acc, o = block(first_masked + m, acc, o, True)

    def body(i, carry):
        return block(first_masked - 1 - i, carry[0], carry[1], False)

    acc, o = lax.fori_loop(0, first_masked, body, (acc, o))
    o_ref[...] = o.astype(BF16)


def _sb_attention(q, k, v, q_off, tq):
    bn, hh, t, d = q.shape
    sk = k.shape[2]
    sk_pad = -(-sk // SB_BLOCK) * SB_BLOCK
    if sk_pad != sk:
        pad = ((0, 0), (0, 0), (0, sk_pad - sk), (0, 0))
        k, v = jnp.pad(k, pad), jnp.pad(v, pad)
    assert tq % SB_BLOCK == 0 or t == tq
    n_masked = -(-((q_off % SB_BLOCK) + tq) // SB_BLOCK)
    kv = pl.BlockSpec((None, None, sk_pad, d), lambda b, h, i: (b, h, 0, 0))
    qo = pl.BlockSpec((None, None, tq, d), lambda b, h, i: (b, h, i, 0))
    return pl.pallas_call(
        functools.partial(_sb_body, tq=tq, q_off=q_off, n_masked=n_masked),
        grid=(bn, hh, t // tq),
        in_specs=[qo, kv, kv],
        out_specs=qo,
        out_shape=jax.ShapeDtypeStruct((bn, hh, t, d), BF16),
        compiler_params=_params(("parallel", "parallel", "arbitrary")),
        name="sb_attention",
    )(q, k, v)


def _mlstm_body(qk_ref, v_ref, o_ref, g_ref, hist_ref, c0_ref, n0_ref, m0_ref,
                cw_ref, cb_ref, gb_ref, nw_ref,
                out_ref, conv_out, c_out, n_out, m_out,
                carry, q_scr, k_scr, c_scr, n_scr, m_scr, *, n_chunks):
    tm = qk_ref.shape[0]
    cl = MLSTM_CHUNK

    @pl.when(pl.program_id(1) == 0)
    def _():
        carry[...] = hist_ref[...]
        c_scr[...] = c0_ref[...]
        n_scr[...] = n0_ref[...]
        m_scr[...] = m0_ref[...]

    x = qk_ref[...]
    ext = jnp.concatenate([carry[...], x], axis=0)
    carry[...] = x[tm - 8:tm, :]
    conv = cb_ref[...] + x * cw_ref[D_CONV - 1:D_CONV, :]
    for sh in range(1, D_CONV):
        conv = conv + pltpu.roll(ext, sh, axis=0)[8:8 + tm, :] * cw_ref[D_CONV - 1 - sh:D_CONV - sh, :]
    conv = conv * _sigmoid(conv)
    q_scr[...] = conv[:, :W]
    k_scr[...] = conv[:, W:] * (D_HEAD ** -0.5)

    row = lax.broadcasted_iota(jnp.int32, (cl, cl), 0)
    col = lax.broadcasted_iota(jnp.int32, (cl, cl), 1)
    causal = row >= col
    tri_incl = causal.astype(BF16)

    def chunk(c, carry_):
        off = pl.multiple_of(c * cl, cl)
        sl = pl.ds(off, cl)
        gpre = g_ref[sl, :] + gb_ref[...]
        lfa = _log_sigmoid(gpre)
        bcol = _dot_lhs01(tri_incl, lfa)
        g_t = gpre.T
        b_t = bcol.T
        for h in range(D_HEADS):
            hs = slice(h * D_HEAD, (h + 1) * D_HEAD)
            q = q_scr[sl, hs]
            k = k_scr[sl, hs]
            v = v_ref[sl, hs]
            ig_col = gpre[:, h:h + 1]
            b_col = bcol[:, D_HEADS + h:D_HEADS + h + 1]
            ig_row = g_t[h:h + 1, :]
            b_row = b_t[D_HEADS + h:D_HEADS + h + 1, :]
            m_prev = m_scr[h:h + 1, 0:1]
            log_d = jnp.where(causal, b_col - b_row + ig_row, -jnp.inf)
            m_inter = b_col + m_prev
            m_t = jnp.maximum(m_inter, jnp.max(log_d, axis=-1, keepdims=True))
            dmat = jnp.exp(log_d - m_t)
            inter = jnp.exp(m_inter - m_t)
            qb = q.astype(BF16)
            vb = v.astype(BF16)
            w_qk = _dot_nt(qb, k.astype(BF16)) * dmat
            num = _dot(w_qk.astype(BF16), vb) + inter * _dot(qb, c_scr[h].astype(BF16))
            qn = jnp.sum(q * n_scr[h:h + 1, :], axis=-1, keepdims=True)
            den = jnp.sum(w_qk, axis=-1, keepdims=True) + inter * qn
            den = jnp.maximum(jnp.abs(den), jnp.exp(-m_t))
            hv = num / den
            m_new = m_t[cl - 1:cl, :]
            b_last = b_col[cl - 1:cl, :]
            decay = jnp.exp(b_last + m_prev - m_new)
            w_s = jnp.exp(b_last - b_col + ig_col - m_new)
            kw = k * w_s
            c_scr[h] = decay * c_scr[h] + _dot(kw.T.astype(BF16), vb)
            n_scr[h:h + 1, :] = decay * n_scr[h:h + 1, :] + jnp.sum(kw, axis=0, keepdims=True)
            m_scr[h:h + 1, :] = jnp.broadcast_to(m_new, (1, LANE))
            mu = jnp.mean(hv, axis=-1, keepdims=True)
            hc = hv - mu
            var = jnp.mean(hc * hc, axis=-1, keepdims=True)
            hn = hc * lax.rsqrt(var + D_NORM_EPS) * nw_ref[:, hs]
            out_ref[sl, hs] = (_sigmoid(o_ref[sl, hs]) * hn).astype(BF16)
        return carry_

    lax.fori_loop(0, n_chunks, chunk, 0)
    conv_out[...] = carry[...]
    c_out[...] = c_scr[...]
    n_out[...] = n_scr[...]
    m_out[...] = m_scr[...]


def _mlstm(u3, conv_hist, c0, n0, m0, p, tm):
    bn, t, _ = u3.shape
    hist8 = jnp.pad(conv_hist, ((0, 0), (8 - (D_CONV - 1), 0), (0, 0)))
    n0p = jnp.pad(n0, ((0, 0), (0, 8 - D_HEADS), (0, 0)))
    m0p = jnp.pad(jnp.broadcast_to(m0[:, :, None], (bn, D_HEADS, LANE)), ((0, 0), (0, 8 - D_HEADS), (0, 0)))
    full = lambda a: pl.BlockSpec(a.shape, lambda b, i: (0,) * a.ndim)
    per_b = lambda *s: pl.BlockSpec((None,) + s, lambda b, i: (b,) + (0,) * len(s))
    consts = (p["conv_w"], p["conv_b"], p["gate_bias"], p["norm_w"])
    outs = pl.pallas_call(
        functools.partial(_mlstm_body, n_chunks=tm // MLSTM_CHUNK),
        grid=(bn, t // tm),
        in_specs=[pl.BlockSpec((None, tm, 2 * W), lambda b, i: (b, i, 0)),
                  pl.BlockSpec((None, tm, W), lambda b, i: (b, i, SLOT["dv"])),
                  pl.BlockSpec((None, tm, W), lambda b, i: (b, i, SLOT["do"])),
                  pl.BlockSpec((None, tm, LANE), lambda b, i: (b, i, COL_DG // LANE)),
                  per_b(8, 2 * W), per_b(D_HEADS, D_HEAD, D_HEAD), per_b(8, LANE), per_b(8, LANE)]
                 + [full(c) for c in consts],
        out_specs=[pl.BlockSpec((None, tm, W), lambda b, i: (b, i, 0)),
                   per_b(8, 2 * W), per_b(D_HEADS, D_HEAD, D_HEAD), per_b(8, LANE), per_b(8, LANE)],
        out_shape=[jax.ShapeDtypeStruct((bn, t, W), BF16),
                   jax.ShapeDtypeStruct((bn, 8, 2 * W), F32),
                   jax.ShapeDtypeStruct((bn, D_HEADS, D_HEAD, D_HEAD), F32),
                   jax.ShapeDtypeStruct((bn, 8, LANE), F32),
                   jax.ShapeDtypeStruct((bn, 8, LANE), F32)],
        scratch_shapes=[pltpu.VMEM((8, 2 * W), F32), pltpu.VMEM((tm, W), F32), pltpu.VMEM((tm, W), F32),
                        pltpu.VMEM((D_HEADS, D_HEAD, D_HEAD), F32), pltpu.VMEM((8, LANE), F32),
                        pltpu.VMEM((8, LANE), F32)],
        compiler_params=_params(("parallel", "arbitrary")),
        name="mlstm",
    )(u3, u3, u3, u3, hist8, c0, n0p, m0p, *consts)
    out_d, conv8, c, n8, m8 = outs
    return out_d, conv8[:, 8 - (D_CONV - 1):], c, n8[:, :D_HEADS], m8[:, :D_HEADS, 0]


def _merge_body(x_ref, g_ref, a_ref, b_ref, c_ref, d_ref, wg_ref, wb_ref, wo_ref, o_ref):
    g = g_ref[...].astype(BF16)
    merged = None
    for i, br in enumerate((a_ref, b_ref, c_ref, d_ref)):
        term = _sigmoid(_dot(g, wg_ref[i])) * _dot(br[...], wb_ref[i])
        merged = term if merged is None else merged + term
    o_ref[...] = x_ref[...] + _dot(merged.astype(BF16), wo_ref[...])


def _merge(x2d, u2d, branches, wg, wb, wo, tm):
    n = x2d.shape[0]
    once = lambda a: pl.BlockSpec(a.shape, lambda i: (0,) * a.ndim, pipeline_mode=pl.Buffered(1))
    bblk = pl.BlockSpec((tm, W), lambda i: (i, 0))
    xblk = pl.BlockSpec((tm, D_MODEL), lambda i: (i, 0))
    return pl.pallas_call(
        _merge_body,
        grid=(n // tm,),
        in_specs=[xblk, pl.BlockSpec((tm, GATE_RANK), lambda i: (i, COL_G // GATE_RANK))]
                 + [bblk] * 4 + [once(wg), once(wb), once(wo)],
        out_specs=xblk,
        out_shape=jax.ShapeDtypeStruct((n, D_MODEL), F32),
        compiler_params=_params(("parallel",)),
        name="merge",
    )(x2d, u2d, *branches, wg, wb, wo)


def _ffn_epilogue(acc, gf_ref, final_norm):
    return _rmsnorm(acc, gf_ref[...]) if final_norm else acc


def _ffn_dense_body(x_ref, g_ref, wg_ref, wu_ref, wd_ref, gf_ref, o_ref, h_scr, acc, *, final_norm):
    j = pl.program_id(1)

    @pl.when(j == 0)
    def _():
        x = x_ref[...]
        h_scr[...] = _rmsnorm(x, g_ref[...]).astype(BF16)
        acc[...] = x

    h = h_scr[...]
    a = _dot(h, wg_ref[...])
    act = (a * _sigmoid(a)) * _dot(h, wu_ref[...])
    acc[...] += _dot(act.astype(BF16), wd_ref[...])

    @pl.when(j == pl.num_programs(1) - 1)
    def _():
        o_ref[...] = _ffn_epilogue(acc[...], gf_ref, final_norm)


def _ffn_dense(x2d, g, wgate, wup, wdown, g_final, final_norm, tm, tf=512):
    n = x2d.shape[0]
    xblk = pl.BlockSpec((tm, D_MODEL), lambda i, j: (i, 0))
    vec = pl.BlockSpec((1, D_MODEL), lambda i, j: (0, 0))
    return pl.pallas_call(
        functools.partial(_ffn_dense_body, final_norm=final_norm),
        grid=(n // tm, D_FF // tf),
        in_specs=[xblk, vec,
                  pl.BlockSpec((D_MODEL, tf), lambda i, j: (0, j)),
                  pl.BlockSpec((D_MODEL, tf), lambda i, j: (0, j)),
                  pl.BlockSpec((tf, D_MODEL), lambda i, j: (j, 0)), vec],
        out_specs=xblk,
        out_shape=jax.ShapeDtypeStruct((n, D_MODEL), F32),
        scratch_shapes=[pltpu.VMEM((tm, D_MODEL), BF16), pltpu.VMEM((tm, D_MODEL), F32)],
        compiler_params=_params(("parallel", "arbitrary")),
        name="ffn_dense",
    )(x2d, g.reshape(1, D_MODEL), wgate, wup, wdown, g_final.reshape(1, D_MODEL))


def _ffn_moe_body(x_ref, g_ref, rt_ref, wg_ref, wu_ref, wd_ref, gf_ref, o_ref,
                  h_scr, acc, gates, *, final_norm):
    e = pl.program_id(1)

    @pl.when(e == 0)
    def _():
        x = x_ref[...]
        h = _rmsnorm(x, g_ref[...])
        h_scr[...] = h.astype(BF16)
        acc[...] = x
        h_hi, h_lo = _split2(h)
        r_hi, r_lo = _split2(rt_ref[...])
        logits = _dot(h_hi, r_hi) + _dot(h_lo, r_hi) + _dot(h_hi, r_lo)
        lane = lax.broadcasted_iota(jnp.int32, logits.shape, 1)
        logits = jnp.where(lane < N_EXPERTS, logits, -jnp.inf)
        v1 = jnp.max(logits, axis=-1, keepdims=True)
        i1 = jnp.min(jnp.where(logits == v1, lane, LANE), axis=-1, keepdims=True)
        rest = jnp.where(lane == i1, -jnp.inf, logits)
        v2 = jnp.max(rest, axis=-1, keepdims=True)
        i2 = jnp.min(jnp.where(rest == v2, lane, LANE), axis=-1, keepdims=True)
        e2 = jnp.exp(v2 - v1)
        den = 1.0 + e2
        gates[...] = jnp.where(lane == i1, 1.0 / den, 0.0) + jnp.where(lane == i2, e2 / den, 0.0)

    h = h_scr[...]
    lane = lax.broadcasted_iota(jnp.int32, gates.shape, 1)
    gate = jnp.sum(jnp.where(lane == e, gates[...], 0.0), axis=-1, keepdims=True)
    a = _dot(h, wg_ref[...])
    act = (a * _sigmoid(a)) * _dot(h, wu_ref[...])
    acc[...] += gate * _dot(act.astype(BF16), wd_ref[...])

    @pl.when(e == pl.num_programs(1) - 1)
    def _():
        o_ref[...] = _ffn_epilogue(acc[...], gf_ref, final_norm)


def _ffn_moe(x2d, g, router, wgate, wup, wdown, g_final, final_norm, tm):
    n = x2d.shape[0]
    xblk = pl.BlockSpec((tm, D_MODEL), lambda i, j: (i, 0))
    vec = pl.BlockSpec((1, D_MODEL), lambda i, j: (0, 0))
    return pl.pallas_call(
        functools.partial(_ffn_moe_body, final_norm=final_norm),
        grid=(n // tm, N_EXPERTS),
        in_specs=[xblk, vec,
                  pl.BlockSpec((D_MODEL, LANE), lambda i, j: (0, 0)),
                  pl.BlockSpec((None, D_MODEL, EXPERT_PAD), lambda i, j: (j, 0, 0)),
                  pl.BlockSpec((None, D_MODEL, EXPERT_PAD), lambda i, j: (j, 0, 0)),
                  pl.BlockSpec((None, EXPERT_PAD, D_MODEL), lambda i, j: (j, 0, 0)), vec],
        out_specs=xblk,
        out_shape=jax.ShapeDtypeStruct((n, D_MODEL), F32),
        scratch_shapes=[pltpu.VMEM((tm, D_MODEL), BF16), pltpu.VMEM((tm, D_MODEL), F32),
                        pltpu.VMEM((tm, LANE), F32)],
        compiler_params=_params(("parallel", "arbitrary")),
        name="ffn_moe",
    )(x2d, g.reshape(1, D_MODEL), router, wgate, wup, wdown, g_final.reshape(1, D_MODEL))


def _pack_w_in(w):
    a, b, c, d = w[:, :OFF_B], w[:, OFF_B:OFF_C], w[:, OFF_C:OFF_D], w[:, OFF_D:OFF_G]
    g = w[:, OFF_G:]
    zeros = lambda k: jnp.zeros((D_MODEL, k), w.dtype)
    lora = a[:, 3 * W:]
    gates = d[:, 4 * W:]
    cols = [d[:, :4 * W], a[:, :3 * W], b, c,
            lora, zeros(LORA_PAD - lora.shape[1]), g,
            gates, zeros(LANE - gates.shape[1]), zeros(WP - COL_DG - LANE)]
    return jnp.concatenate(cols, axis=1).astype(BF16)


def _layer_params(l, wts):
    (norm_mix, norm_ffn, norm_final, w_in, rwkv_mu, rwkv_w0, rwkv_w2, rwkv_a0, rwkv_a2,
     rwkv_g2, rwkv_k_k, rwkv_k_a, rwkv_r_k, rwkv_ln_w, rwkv_ln_b, pool_w, pool_scale,
     mlstm_conv_w, mlstm_conv_b, mlstm_i_bias, mlstm_f_bias, mlstm_norm_w,
     w_branch, w_merge_gate, w_out, ffn_w_gate, ffn_w_up, ffn_w_down,
     moe_router, moe_w_gate, moe_w_up, moe_w_down) = wts
    n_l = sum(A_LORA)
    row = lambda v: v.reshape(1, -1)

    def lora_pad(wm, start):
        return jnp.pad(wm, ((start, LORA_PAD - start - wm.shape[0]), (0, 0))).astype(BF16)

    hid = jnp.arange(W) // A_HEAD
    p = {
        "norm_mix": norm_mix[l], "norm_ffn": norm_ffn[l],
        "wp": _pack_w_in(w_in[l]),
        "rwkv": {
            "mu_rkv": rwkv_mu[l, :3 * W].reshape(3, W),
            "mu_l": jnp.pad(rwkv_mu[l, 3 * W:], (0, LORA_PAD - n_l)).reshape(1, LORA_PAD),
            "w0": row(rwkv_w0[l]), "a0": row(rwkv_a0[l]),
            "k_k": row(rwkv_k_k[l]), "k_a": row(rwkv_k_a[l]),
            "w2p": lora_pad(rwkv_w2[l], 0),
            "a2p": lora_pad(rwkv_a2[l], A_LORA[0]),
            "g2p": lora_pad(rwkv_g2[l], A_LORA[0] + A_LORA[1]),
            "head_ones": (hid[:, None] == hid[None, :]).astype(BF16),
            "r_k": row(rwkv_r_k[l]), "ln_w": row(rwkv_ln_w[l]), "ln_b": row(rwkv_ln_b[l]),
        },
        "pool_w": pool_w[l], "pool_scale": pool_scale[l],
        "mlstm": {
            "conv_w": mlstm_conv_w[l], "conv_b": row(mlstm_conv_b[l]),
            "gate_bias": jnp.pad(jnp.concatenate([mlstm_i_bias[l], mlstm_f_bias[l]]),
                                 (0, LANE - 2 * D_HEADS)).reshape(1, LANE),
            "norm_w": row(mlstm_norm_w[l]),
        },
        "wg": w_merge_gate[l].astype(BF16), "wb": w_branch[l].astype(BF16), "wo": w_out[l].astype(BF16),
    }
    if l % 2 == 0:
        p["ffn"] = (ffn_w_gate[l // 2].astype(BF16), ffn_w_up[l // 2].astype(BF16),
                    ffn_w_down[l // 2].astype(BF16))
    else:
        pe = EXPERT_PAD - D_FF_EXPERT
        p["moe"] = (jnp.pad(moe_router[l // 2], ((0, 0), (0, LANE - N_EXPERTS))),
                    jnp.pad(moe_w_gate[l // 2], ((0, 0), (0, 0), (0, pe))).astype(BF16),
                    jnp.pad(moe_w_up[l // 2], ((0, 0), (0, 0), (0, pe))).astype(BF16),
                    jnp.pad(moe_w_down[l // 2], ((0, 0), (0, pe), (0, 0))).astype(BF16))
    return p


def _heads_major(x3, heads):
    bn, t, c = x3.shape
    return x3.reshape(bn, t, heads, c // heads).transpose(0, 2, 1, 3)


def _heads_minor(x4):
    bn, hh, t, d = x4.shape
    return x4.transpose(0, 2, 1, 3).reshape(bn, t, hh * d)


def _slot(u3, name):
    s = SLOT[name] * W
    return u3[:, :, s:s + W]


def _run_trunk(x, start_pos, cache_k, cache_v, wkv0, shift0, pool0, conv0, c0, n0, m0,
               layers, norm_final):
    bn, t, _ = x.shape
    n = bn * t
    tm_tok = min(256, t)
    tm_row = min(512, n)
    x2 = x.reshape(n, D_MODEL)
    ks, vs, wkvs, shifts, pools, convs, cs, ns, ms = ([] for _ in range(9))
    for l, p in enumerate(layers):
        u2 = _inproj(x2, p["norm_mix"], p["wp"])
        u3 = u2.reshape(bn, t, WP)

        r, k2, v, kk, b, lw, g = _rwkv_prep(u3, shift0[l], p["rwkv"], tm_tok)
        hm = lambda a: _heads_major(a, A_HEADS)
        y, wkv = _rwkv_chunk(hm(r), hm(k2), hm(v), hm(kk), hm(b), hm(lw), wkv0[l], tm_tok)
        flat = lambda a: a.reshape(n, W)
        out_a = _rwkv_post(flat(_heads_minor(y)), flat(r), flat(k2), flat(v), flat(g), p["rwkv"], tm_row)

        out_b = _pool(u3, pool0[l], p["pool_w"], p["pool_scale"], start_pos, tm_tok).reshape(n, W)

        q_c, k_c, v_c = _slot(u3, "cq"), _slot(u3, "ck"), _slot(u3, "cv")
        hb = lambda a: _heads_major(a.astype(BF16), C_HEADS)
        if cache_k is None:
            out_c = _sb_attention(hb(q_c), hb(k_c), hb(v_c), 0, min(256, t))
        else:
            past = cache_k.shape[2]
            k_all = jnp.concatenate([cache_k[l].reshape(bn, past, W), k_c], axis=1)
            v_all = jnp.concatenate([cache_v[l].reshape(bn, past, W), v_c], axis=1)
            out_c = _sb_attention(hb(q_c), hb(k_all), hb(v_all), past, t)
        out_c = _heads_minor(out_c).reshape(n, W)

        out_d, conv_new, c_new, n_new, m_new = _mlstm(u3, conv0[l], c0[l], n0[l], m0[l], p["mlstm"], tm_tok)

        x2 = _merge(x2, u2, (out_a, out_b, out_c, out_d.reshape(n, W)), p["wg"], p["wb"], p["wo"],
                    min(256, n))
        last = l == len(layers) - 1
        if "ffn" in p:
            x2 = _ffn_dense(x2, p["norm_ffn"], *p["ffn"], norm_final, last, tm_row)
        else:
            x2 = _ffn_moe(x2, p["norm_ffn"], *p["moe"], norm_final, last, tm_row)

        ks.append(k_c.reshape(bn, t, C_HEADS, C_HEAD))
        vs.append(v_c.reshape(bn, t, C_HEADS, C_HEAD))
        wkvs.append(wkv)
        u_last = u3[:, -1]
        shifts.append(jnp.concatenate(
            [u_last[:, SLOT["ar"] * W:SLOT["ar"] * W + 3 * W], u_last[:, COL_LORA:COL_LORA + sum(A_LORA)]], axis=1))
        pools.append(jnp.concatenate([pool0[l], _slot(u3, "b")[:, -B_HIST:]], axis=1)[:, -B_HIST:])
        convs.append(conv_new)
        cs.append(c_new)
        ns.append(n_new)
        ms.append(m_new)
    st = lambda lst: jnp.stack(lst, axis=0)
    return (x2.reshape(bn, t, D_MODEL),
            (st(ks), st(vs), st(wkvs), st(shifts), st(pools), st(convs), st(cs), st(ns), st(ms)))


def kernel(x_prompt, x_sample, cache_sb_k, cache_sb_v, state_rwkv_wkv, state_rwkv_shift, state_pool,
           state_mlstm_conv, state_mlstm_c, state_mlstm_n, state_mlstm_m, norm_mix, norm_ffn, norm_final,
           w_in, rwkv_mu, rwkv_w0, rwkv_w2, rwkv_a0, rwkv_a2, rwkv_g2, rwkv_k_k, rwkv_k_a, rwkv_r_k,
           rwkv_ln_w, rwkv_ln_b, pool_w, pool_scale, mlstm_conv_w, mlstm_conv_b, mlstm_i_bias,
           mlstm_f_bias, mlstm_norm_w, w_branch, w_merge_gate, w_out, ffn_w_gate, ffn_w_up, ffn_w_down,
           moe_router, moe_w_gate, moe_w_up, moe_w_down):
    wts = (norm_mix, norm_ffn, norm_final, w_in, rwkv_mu, rwkv_w0, rwkv_w2, rwkv_a0, rwkv_a2,
           rwkv_g2, rwkv_k_k, rwkv_k_a, rwkv_r_k, rwkv_ln_w, rwkv_ln_b, pool_w, pool_scale,
           mlstm_conv_w, mlstm_conv_b, mlstm_i_bias, mlstm_f_bias, mlstm_norm_w,
           w_branch, w_merge_gate, w_out, ffn_w_gate, ffn_w_up, ffn_w_down,
           moe_router, moe_w_gate, moe_w_up, moe_w_down)
    depth = w_in.shape[0]
    layers = [_layer_params(l, wts) for l in range(depth)]
    bp = x_prompt.shape[0]
    z = lambda *s: jnp.zeros((depth, bp) + s, F32)
    y_p, st_p = _run_trunk(
        x_prompt, 0, None, None, z(A_HEADS, A_HEAD, A_HEAD), z(A_IN), z(B_HIST, W),
        z(D_CONV - 1, 2 * W), z(D_HEADS, D_HEAD, D_HEAD), z(D_HEADS, D_HEAD), z(D_HEADS),
        layers, norm_final)
    y_s, st_s = _run_trunk(
        x_sample, cache_sb_k.shape[2], cache_sb_k, cache_sb_v, state_rwkv_wkv, state_rwkv_shift,
        state_pool, state_mlstm_conv, state_mlstm_c, state_mlstm_n, state_mlstm_m,
        layers, norm_final)
    return (y_p, y_s) + tuple(st_p) + tuple(st_s)
```

```python
import functools

import jax
import jax.numpy as jnp
from jax import lax
from jax.experimental import pallas as pl
from jax.experimental.pallas import tpu as pltpu

F32 = jnp.float32
BF16 = jnp.bfloat16

D_MODEL = 2048
DEPTH = 2
W = 512
A_HEAD, A_HEADS = 64, 8
A_LORA = (32, 32, 96)
A_IN = 3 * W + sum(A_LORA)
A_GN_EPS = 64e-5
B_WINDOWS = (2, 4, 8, 16)
B_HIST = 15
C_HEAD, C_HEADS = 64, 8
D_HEADS, D_HEAD = 4, 128
D_CONV = 4
D_NORM_EPS = 1e-6
GATE_RANK = 256
OFF_B = A_IN
OFF_C = OFF_B + W
OFF_D = OFF_C + 3 * W
OFF_G = OFF_D + 4 * W + 2 * D_HEADS
IN_WIDTH = OFF_G + GATE_RANK
D_FF = 5632
N_EXPERTS = 8
D_FF_EXPERT = D_FF // N_EXPERTS
NORM_EPS = 1e-6

LANE = 128
V7X_VMEM_BYTES = 64 * 1024 * 1024
VMEM_LIMIT = V7X_VMEM_BYTES - 8 * 1024 * 1024

SLOT = {name: i for i, name in enumerate(
    ("dq", "dk", "dv", "do", "ar", "ak", "av", "b", "cq", "ck", "cv"))}
COL_LORA = 11 * W
COL_G = COL_LORA + 256
COL_DG = COL_G + 256
IN_TN = 1280
WP = 6400
LORA_PAD = 256
EXPERT_PAD = 768

RWKV_CHUNK = 64
MLSTM_CHUNK = 64
SB_BLOCK = 128
SB_TQ = 1024


def _params(sem, vmem=VMEM_LIMIT):
    return pltpu.CompilerParams(dimension_semantics=sem, vmem_limit_bytes=vmem)


def _split2(x):
    hi = x.astype(BF16)
    lo = (x - hi.astype(F32)).astype(BF16)
    return hi, lo


def _split3(x):
    hi = x.astype(BF16)
    r1 = x - hi.astype(F32)
    mid = r1.astype(BF16)
    lo = (r1 - mid.astype(F32)).astype(BF16)
    return hi, mid, lo


def _dot(a, b):
    return jnp.dot(a, b, preferred_element_type=F32)


def _dot_nt(a, b):
    return lax.dot_general(a, b, (((1,), (1,)), ((), ())), preferred_element_type=F32)


def _bdot(a, b):
    return lax.dot_general(a, b, (((2,), (1,)), ((0,), (0,))), preferred_element_type=F32)


def _bdot_nt(a, b):
    return lax.dot_general(a, b, (((2,), (2,)), ((0,), (0,))), preferred_element_type=F32)


def _bdot_tn(a, b):
    return lax.dot_general(a, b, (((1,), (1,)), ((0,), (0,))), preferred_element_type=F32)


def _dot_rhs01(x, m01):
    hi, mid, lo = _split3(x)
    return _dot(hi, m01) + _dot(mid, m01) + _dot(lo, m01)


def _dot_lhs01(m01, x):
    hi, mid, lo = _split3(x)
    return _dot(m01, hi) + _dot(m01, mid) + _dot(m01, lo)


def _sigmoid(x):
    return 1.0 / (1.0 + jnp.exp(-x))


def _log_sigmoid(x):
    return jnp.minimum(x, 0.0) - jnp.log1p(jnp.exp(-jnp.abs(x)))


def _rmsnorm(x, g):
    ms = jnp.mean(x * x, axis=-1, keepdims=True)
    return x * lax.rsqrt(ms + NORM_EPS) * g


def _inproj_body(x_ref, g_ref, w_ref, o_ref, h_scr):
    @pl.when(pl.program_id(1) == 0)
    def _():
        h_scr[...] = _rmsnorm(x_ref[...], g_ref[...]).astype(BF16)

    o_ref[...] = _dot(h_scr[...], w_ref[...])


def _inproj(x2d, g, wp):
    n = x2d.shape[0]
    tm = min(512, n)
    return pl.pallas_call(
        _inproj_body,
        grid=(n // tm, WP // IN_TN),
        in_specs=[pl.BlockSpec((tm, D_MODEL), lambda i, j: (i, 0)),
                  pl.BlockSpec((1, D_MODEL), lambda i, j: (0, 0)),
                  pl.BlockSpec((D_MODEL, IN_TN), lambda i, j: (0, j))],
        out_specs=pl.BlockSpec((tm, IN_TN), lambda i, j: (i, j)),
        out_shape=jax.ShapeDtypeStruct((n, WP), F32),
        scratch_shapes=[pltpu.VMEM((tm, D_MODEL), BF16)],
        compiler_params=_params(("parallel", "arbitrary")),
        name="inproj",
    )(x2d, g.reshape(1, D_MODEL), wp)


def _shift_rows(u, carry_row):
    rolled = pltpu.roll(u, 1, axis=0)
    row = lax.broadcasted_iota(jnp.int32, u.shape, 0)
    return jnp.where(row == 0, carry_row, rolled)


def _rwkv_prep_body(ur_ref, uk_ref, uv_ref, ul_ref, sr_ref, sk_ref, sv_ref, sl_ref,
                    mu_ref, mul_ref, w0_ref, a0_ref, kk_ref, ka_ref,
                    w2_ref, a2_ref, g2_ref, ones_ref,
                    r_out, k_out, v_out, kk_out, b_out, lw_out, g_out,
                    cr, ck, cv, cl):
    tm = ur_ref.shape[0]

    @pl.when(pl.program_id(1) == 0)
    def _():
        cr[0:1, :] = sr_ref[...]
        ck[0:1, :] = sk_ref[...]
        cv[0:1, :] = sv_ref[...]
        cl[0:1, :] = sl_ref[...]

    def mix(u_ref, c_ref, mu):
        u = u_ref[...]
        prev = _shift_rows(u, c_ref[0:1, :])
        c_ref[0:1, :] = u[tm - 1:tm, :]
        return u + (prev - u) * mu

    r = mix(ur_ref, cr, mu_ref[0:1, :])
    k = mix(uk_ref, ck, mu_ref[1:2, :])
    v = mix(uv_ref, cv, mu_ref[2:3, :])
    xl = mix(ul_ref, cl, mul_ref[...])

    w_lora = _dot(jnp.tanh(xl).astype(BF16), w2_ref[...])
    a_lora = _dot(xl.astype(BF16), a2_ref[...])
    g = _dot(_sigmoid(xl).astype(BF16), g2_ref[...])

    y = -(w0_ref[...] + w_lora)
    w_log = -(jnp.maximum(y, 0.0) + jnp.log1p(jnp.exp(-jnp.abs(y)))) - 0.5
    log_decay = -jnp.exp(w_log)
    a = _sigmoid(a0_ref[...] + a_lora)

    kk = k * kk_ref[...]
    hi, lo = _split2(kk * kk)
    ss = _dot(hi, ones_ref[...]) + _dot(lo, ones_ref[...])
    kk = kk / jnp.maximum(jnp.sqrt(ss), 1e-12)
    k2 = k * (1.0 + (a - 1.0) * ka_ref[...])

    r_out[...] = r
    k_out[...] = k2
    v_out[...] = v
    kk_out[...] = kk
    b_out[...] = kk * a
    lw_out[...] = log_decay
    g_out[...] = g


def _rwkv_prep(u3, shift_prev, p, tm):
    bn, t, _ = u3.shape
    sr = shift_prev[:, None, 0:W]
    sk = shift_prev[:, None, W:2 * W]
    sv = shift_prev[:, None, 2 * W:3 * W]
    sl = jnp.pad(shift_prev[:, None, 3 * W:], ((0, 0), (0, 0), (0, LORA_PAD - sum(A_LORA))))
    row = lambda c: pl.BlockSpec((None, 1, c), lambda b, i: (b, 0, 0))
    full = lambda a: pl.BlockSpec(a.shape, lambda b, i: (0,) * a.ndim)
    ublk = lambda slot: pl.BlockSpec((None, tm, W), lambda b, i, s=slot: (b, i, s))
    consts = (p["mu_rkv"], p["mu_l"], p["w0"], p["a0"], p["k_k"], p["k_a"],
              p["w2p"], p["a2p"], p["g2p"], p["head_ones"])
    out = jax.ShapeDtypeStruct((bn, t, W), F32)
    oblk = pl.BlockSpec((None, tm, W), lambda b, i: (b, i, 0))
    return pl.pallas_call(
        _rwkv_prep_body,
        grid=(bn, t // tm),
        in_specs=[ublk(SLOT["ar"]), ublk(SLOT["ak"]), ublk(SLOT["av"]),
                  pl.BlockSpec((None, tm, LORA_PAD), lambda b, i: (b, i, COL_LORA // LORA_PAD)),
                  row(W), row(W), row(W), row(LORA_PAD)] + [full(c) for c in consts],
        out_specs=[oblk] * 7,
        out_shape=[out] * 7,
        scratch_shapes=[pltpu.VMEM((8, W), F32)] * 3 + [pltpu.VMEM((8, LORA_PAD), F32)],
        compiler_params=_params(("parallel", "arbitrary")),
        name="rwkv_prep",
    )(u3, u3, u3, u3, sr, sk, sv, sl, *consts)


def _rwkv_chunk_body(r_ref, k_ref, v_ref, kk_ref, b_ref, lw_ref, s0_ref,
                     y_ref, sout_ref, s_scr, *, n_chunks):
    c_len = RWKV_CHUNK

    @pl.when(pl.program_id(1) == 0)
    def _():
        s_scr[...] = s0_ref[...]

    hh = A_HEADS
    row = lax.broadcasted_iota(jnp.int32, (hh, c_len, c_len), 1)
    col = lax.broadcasted_iota(jnp.int32, (hh, c_len, c_len), 2)
    tri_incl = (row >= col).astype(BF16)
    lower = row >= col
    strict = row > col
    eye = (row == col).astype(F32)

    def chunk(c, carry):
        off = pl.multiple_of(c * c_len, c_len)
        sl = pl.ds(off, c_len)
        r = r_ref[:, sl, :]
        k = k_ref[:, sl, :]
        v = v_ref[:, sl, :]
        kk = kk_ref[:, sl, :]
        b = b_ref[:, sl, :]
        logw = lw_ref[:, sl, :]
        w_hi, w_mid, w_lo = _split3(logw)
        lw = _bdot(tri_incl, w_hi) + _bdot(tri_incl, w_mid) + _bdot(tri_incl, w_lo)
        lw_prev = lw - logw
        lw_last = lw[:, c_len - 1:c_len, :]
        e_neg = jnp.exp(-lw)
        e_end = jnp.exp(lw_last - lw)
        kkm = (kk * jnp.exp(lw_prev)).astype(BF16)
        rm = r * jnp.exp(lw)
        kp = (k * e_neg).astype(BF16)
        bp = (b * e_neg).astype(BF16)
        kpp = (k * e_end).astype(BF16)
        bpp = (b * e_end).astype(BF16)
        vb = v.astype(BF16)
        rmb = rm.astype(BF16)

        a_vk = jnp.where(strict, _bdot_nt(kkm, kp), 0.0)
        a_pb = jnp.where(strict, _bdot_nt(kkm, bp), 0.0)
        rk = jnp.where(lower, _bdot_nt(rmb, kp), 0.0)
        rb = jnp.where(lower, _bdot_nt(rmb, bp), 0.0)

        n_pow = -a_pb
        t_inv = eye + n_pow
        for _ in range(5):
            nb = n_pow.astype(BF16)
            n_pow = _bdot(nb, nb)
            t_inv = t_inv + _bdot(t_inv.astype(BF16), n_pow.astype(BF16))
        tb = t_inv.astype(BF16)

        kktb = _bdot(tb, kkm).astype(BF16)
        pvb = _bdot(tb, _bdot(a_vk.astype(BF16), vb).astype(BF16)).astype(BF16)
        rbb = rb.astype(BF16)
        m_mat = eye * jnp.exp(lw_last) - _bdot_tn(kktb, bpp)
        n_mat = _bdot_tn(vb, kpp) - _bdot_tn(pvb, bpp)
        q_mat = rm - _bdot(rbb, kktb)
        y0 = _bdot(rk.astype(BF16), vb) - _bdot(rbb, pvb)

        s_hi, s_lo = _split2(s_scr[...])
        m_hi, m_lo = _split2(m_mat)
        y_ref[:, sl, :] = _bdot_nt(q_mat.astype(BF16), s_hi) + y0
        s_scr[...] = _bdot(s_hi, m_hi) + _bdot(s_lo, m_hi) + _bdot(s_hi, m_lo) + n_mat
        return carry

    lax.fori_loop(0, n_chunks, chunk, 0)
    sout_ref[...] = s_scr[...]


def _rwkv_chunk(r, k, v, kk, b, lw, s0, tm):
    bn, hh, t, d = r.shape
    blk = pl.BlockSpec((None, hh, tm, d), lambda bi, i: (bi, 0, i, 0))
    sblk = pl.BlockSpec((None, hh, d, d), lambda bi, i: (bi, 0, 0, 0))
    return pl.pallas_call(
        functools.partial(_rwkv_chunk_body, n_chunks=tm // RWKV_CHUNK),
        grid=(bn, t // tm),
        in_specs=[blk] * 6 + [sblk],
        out_specs=[blk, sblk],
        out_shape=[jax.ShapeDtypeStruct((bn, hh, t, d), F32),
                   jax.ShapeDtypeStruct((bn, hh, d, d), F32)],
        scratch_shapes=[pltpu.VMEM((hh, d, d), F32)],
        compiler_params=_params(("parallel", "arbitrary")),
        name="rwkv_chunk",
    )(r, k, v, kk, b, lw, s0)


def _rwkv_post_body(y_ref, r_ref, k_ref, v_ref, g_ref, rk_ref, lnw_ref, lnb_ref, ones_ref, o_ref):
    y = y_ref[...]
    ones = ones_ref[...]

    def head_sum(x):
        hi, lo = _split2(x)
        return _dot(hi, ones) + _dot(lo, ones)

    mu = head_sum(y) * (1.0 / A_HEAD)
    yc = y - mu
    var = head_sum(yc * yc) * (1.0 / A_HEAD)
    yn = yc * lax.rsqrt(var + A_GN_EPS) * lnw_ref[...] + lnb_ref[...]
    bonus = head_sum(r_ref[...] * k_ref[...] * rk_ref[...]) * v_ref[...]
    o_ref[...] = ((yn + bonus) * g_ref[...]).astype(BF16)


def _rwkv_post(y, r, k, v, g, p, tm):
    n = y.shape[0]
    blk = pl.BlockSpec((tm, W), lambda i: (i, 0))
    full = lambda a: pl.BlockSpec(a.shape, lambda i: (0,) * a.ndim)
    consts = (p["r_k"], p["ln_w"], p["ln_b"], p["head_ones"])
    return pl.pallas_call(
        _rwkv_post_body,
        grid=(n // tm,),
        in_specs=[blk] * 5 + [full(c) for c in consts],
        out_specs=blk,
        out_shape=jax.ShapeDtypeStruct((n, W), BF16),
        compiler_params=_params(("parallel",)),
        name="rwkv_post",
    )(y, r, k, v, g, *consts)


def _pool_body(u_ref, hist_ref, w_ref, scale_ref, o_ref, carry, *, start_pos):
    tm = u_ref.shape[0]
    i = pl.program_id(1)

    @pl.when(i == 0)
    def _():
        carry[...] = hist_ref[...]

    x = u_ref[...]
    ext = jnp.concatenate([carry[...], x], axis=0)
    carry[...] = x[tm - 16:tm, :]
    s = ext
    sums = []
    for sh in (1, 2, 4, 8):
        s = s + pltpu.roll(s, sh, axis=0)
        sums.append(s[16:16 + tm, :])
    pos = start_pos + i * tm + lax.broadcasted_iota(jnp.int32, (tm, 1), 0)
    for gi, wlen in enumerate(B_WINDOWS):
        cs = slice(gi * LANE, (gi + 1) * LANE)
        cnt = jnp.minimum(wlen, pos + 1).astype(F32)
        d = sums[gi][:, cs] / cnt - x[:, cs]
        o_ref[:, cs] = (_dot(d.astype(BF16), w_ref[gi]) * scale_ref[:, cs]).astype(BF16)


def _pool(u3, hist, pool_w, pool_scale, start_pos, tm):
    bn, t, _ = u3.shape
    hist16 = jnp.pad(hist, ((0, 0), (1, 0), (0, 0)))
    return pl.pallas_call(
        functools.partial(_pool_body, start_pos=start_pos),
        grid=(bn, t // tm),
        in_specs=[pl.BlockSpec((None, tm, W), lambda b, i: (b, i, SLOT["b"])),
                  pl.BlockSpec((None, 16, W), lambda b, i: (b, 0, 0)),
                  pl.BlockSpec((4, LANE, LANE), lambda b, i: (0, 0, 0)),
                  pl.BlockSpec((1, W), lambda b, i: (0, 0))],
        out_specs=pl.BlockSpec((None, tm, W), lambda b, i: (b, i, 0)),
        out_shape=jax.ShapeDtypeStruct((bn, t, W), BF16),
        scratch_shapes=[pltpu.VMEM((16, W), F32)],
        compiler_params=_params(("parallel", "arbitrary")),
        name="pool",
    )(u3, hist16, pool_w.astype(BF16), pool_scale.reshape(1, W))


def _sb_body(q_ref, k_ref, v_ref, o_ref, *, tq, q_off, n_masked):
    qi = pl.program_id(2)
    q = q_ref[...] * (C_HEAD ** -0.5)
    q_start = q_off + qi * tq
    first_masked = q_start // SB_BLOCK

    jj = lax.broadcasted_iota(jnp.int32, (2 * SB_BLOCK, 2 * SB_BLOCK), 0) % SB_BLOCK
    ss = lax.broadcasted_iota(jnp.int32, (2 * SB_BLOCK, 2 * SB_BLOCK), 1)
    cs = jnp.where(jnp.logical_or(ss >= SB_BLOCK, jj > ss), -1.0, 0.0).astype(BF16)
    q_pos = q_start + lax.broadcasted_iota(jnp.int32, (tq, SB_BLOCK), 0)
    k_lane = lax.broadcasted_iota(jnp.int32, (tq, SB_BLOCK), 1)

    def block(kb, acc, o, masked):
        ks = pl.multiple_of(kb * SB_BLOCK, SB_BLOCK)
        kblk = k_ref[pl.ds(ks, SB_BLOCK), :]
        vblk = v_ref[pl.ds(ks, SB_BLOCK), :]
        z = _dot_nt(q, kblk)
        zp = jnp.maximum(z, 0.0)
        zn = jnp.minimum(z, 0.0)
        l1p = jnp.log(1.0 + jnp.exp(zn - zp))
        nlf = zp + l1p
        s = zn - l1p
        if masked:
            mask = (ks + k_lane) < q_pos
            nlf = jnp.where(mask, nlf, 0.0)
        hi, lo = _split2(nlf)
        lt = _dot(jnp.concatenate([hi, lo], axis=1), cs)
        att = jnp.exp(s + lt[:, :SB_BLOCK] + acc)
        if masked:
            att = jnp.where(mask, att, 0.0)
        o = o + _dot(att.astype(BF16), vblk)
        return acc + lt[:, SB_BLOCK:], o

    acc = jnp.zeros((tq, SB_BLOCK), F32)
    o = jnp.zeros((tq, C_HEAD), F32)
    acc, o = lax.fori_loop(
        0, n_masked, lambda i, c: block(first_masked + n_masked - 1 - i, c[0], c[1], True), (acc, o))
    acc, o = lax.fori_loop(
        0, first_masked, lambda i, c: block(first_masked - 1 - i, c[0], c[1], False), (acc, o))
    o_ref[...] = o.astype(BF16)


def _sb_attention(q, k, v, q_off, tq):
    bn, hh, t, d = q.shape
    sk = k.shape[2]
    sk_pad = -(-sk // SB_BLOCK) * SB_BLOCK
    if sk_pad != sk:
        pad = ((0, 0), (0, 0), (0, sk_pad - sk), (0, 0))
        k, v = jnp.pad(k, pad), jnp.pad(v, pad)
    assert tq % SB_BLOCK == 0 or t == tq
    n_masked = -(-((q_off % SB_BLOCK) + tq) // SB_BLOCK)
    kv = pl.BlockSpec((None, None, sk_pad, d), lambda b, h, i: (b, h, 0, 0))
    qo = pl.BlockSpec((None, None, tq, d), lambda b, h, i: (b, h, i, 0))
    return pl.pallas_call(
        functools.partial(_sb_body, tq=tq, q_off=q_off, n_masked=n_masked),
        grid=(bn, hh, t // tq),
        in_specs=[qo, kv, kv],
        out_specs=qo,
        out_shape=jax.ShapeDtypeStruct((bn, hh, t, d), BF16),
        compiler_params=_params(("parallel", "parallel", "arbitrary")),
        name="sb_attention",
    )(q, k, v)


def _mlstm_body(qk_ref, v_ref, o_ref, g_ref, hist_ref, c0_ref, n0_ref, m0_ref,
                cw_ref, cb_ref, gb_ref, nw_ref,
                out_ref, conv_out, c_out, n_out, m_out,
                carry, q_scr, k_scr, c_scr, n_scr, m_scr, *, n_chunks):
    tm = qk_ref.shape[0]
    cl = MLSTM_CHUNK

    @pl.when(pl.program_id(1) == 0)
    def _():
        carry[...] = hist_ref[...]
        c_scr[...] = c0_ref[...]
        n_scr[...] = n0_ref[...]
        m_scr[...] = m0_ref[...]

    x = qk_ref[...]
    ext = jnp.concatenate([carry[...], x], axis=0)
    carry[...] = x[tm - 8:tm, :]
    conv = cb_ref[...] + x * cw_ref[D_CONV - 1:D_CONV, :]
    for sh in range(1, D_CONV):
        conv = conv + pltpu.roll(ext, sh, axis=0)[8:8 + tm, :] * cw_ref[D_CONV - 1 - sh:D_CONV - sh, :]
    conv = conv * _sigmoid(conv)
    q_scr[...] = conv[:, :W]
    k_scr[...] = conv[:, W:] * (D_HEAD ** -0.5)

    row = lax.broadcasted_iota(jnp.int32, (cl, cl), 0)
    col = lax.broadcasted_iota(jnp.int32, (cl, cl), 1)
    causal = row >= col
    tri_incl = causal.astype(BF16)

    def chunk(c, carry_):
        off = pl.multiple_of(c * cl, cl)
        sl = pl.ds(off, cl)
        gpre = g_ref[sl, :] + gb_ref[...]
        lfa = _log_sigmoid(gpre)
        bcol = _dot_lhs01(tri_incl, lfa)
        g_t = gpre.T
        b_t = bcol.T
        for h in range(D_HEADS):
            hs = slice(h * D_HEAD, (h + 1) * D_HEAD)
            q = q_scr[sl, hs]
            k = k_scr[sl, hs]
            v = v_ref[sl, hs]
            ig_col = gpre[:, h:h + 1]
            b_col = bcol[:, D_HEADS + h:D_HEADS + h + 1]
            ig_row = g_t[h:h + 1, :]
            b_row = b_t[D_HEADS + h:D_HEADS + h + 1, :]
            m_prev = m_scr[h:h + 1, 0:1]
            log_d = jnp.where(causal, b_col - b_row + ig_row, -jnp.inf)
            m_inter = b_col + m_prev
            m_t = jnp.maximum(m_inter, jnp.max(log_d, axis=-1, keepdims=True))
            dmat = jnp.exp(log_d - m_t)
            inter = jnp.exp(m_inter - m_t)
            qb = q.astype(BF16)
            vb = v.astype(BF16)
            w_qk = _dot_nt(qb, k.astype(BF16)) * dmat
            num = _dot(w_qk.astype(BF16), vb) + inter * _dot(qb, c_scr[h].astype(BF16))
            qn = jnp.sum(q * n_scr[h:h + 1, :], axis=-1, keepdims=True)
            den = jnp.sum(w_qk, axis=-1, keepdims=True) + inter * qn
            den = jnp.maximum(jnp.abs(den), jnp.exp(-m_t))
            hv = num / den
            m_new = m_t[cl - 1:cl, :]
            b_last = b_col[cl - 1:cl, :]
            decay = jnp.exp(b_last + m_prev - m_new)
            w_s = jnp.exp(b_last - b_col + ig_col - m_new)
            kw = k * w_s
            c_scr[h] = decay * c_scr[h] + _dot(kw.T.astype(BF16), vb)
            n_scr[h:h + 1, :] = decay * n_scr[h:h + 1, :] + jnp.sum(kw, axis=0, keepdims=True)
            m_scr[h:h + 1, :] = jnp.broadcast_to(m_new, (1, LANE))
            mu = jnp.mean(hv, axis=-1, keepdims=True)
            hc = hv - mu
            var = jnp.mean(hc * hc, axis=-1, keepdims=True)
            hn = hc * lax.rsqrt(var + D_NORM_EPS) * nw_ref[:, hs]
            out_ref[sl, hs] = (_sigmoid(o_ref[sl, hs]) * hn).astype(BF16)
        return carry_

    lax.fori_loop(0, n_chunks, chunk, 0)
    conv_out[...] = carry[...]
    c_out[...] = c_scr[...]
    n_out[...] = n_scr[...]
    m_out[...] = m_scr[...]


def _mlstm(u3, conv_hist, c0, n0, m0, p, tm):
    bn, t, _ = u3.shape
    hist8 = jnp.pad(conv_hist, ((0, 0), (8 - (D_CONV - 1), 0), (0, 0)))
    n0p = jnp.pad(n0, ((0, 0), (0, 8 - D_HEADS), (0, 0)))
    m0p = jnp.pad(jnp.broadcast_to(m0[:, :, None], (bn, D_HEADS, LANE)), ((0, 0), (0, 8 - D_HEADS), (0, 0)))
    full = lambda a: pl.BlockSpec(a.shape, lambda b, i: (0,) * a.ndim)
    per_b = lambda *s: pl.BlockSpec((None,) + s, lambda b, i: (b,) + (0,) * len(s))
    consts = (p["conv_w"], p["conv_b"], p["gate_bias"], p["norm_w"])
    outs = pl.pallas_call(
        functools.partial(_mlstm_body, n_chunks=tm // MLSTM_CHUNK),
        grid=(bn, t // tm),
        in_specs=[pl.BlockSpec((None, tm, 2 * W), lambda b, i: (b, i, 0)),
                  pl.BlockSpec((None, tm, W), lambda b, i: (b, i, SLOT["dv"])),
                  pl.BlockSpec((None, tm, W), lambda b, i: (b, i, SLOT["do"])),
                  pl.BlockSpec((None, tm, LANE), lambda b, i: (b, i, COL_DG // LANE)),
                  per_b(8, 2 * W), per_b(D_HEADS, D_HEAD, D_HEAD), per_b(8, LANE), per_b(8, LANE)]
                 + [full(c) for c in consts],
        out_specs=[pl.BlockSpec((None, tm, W), lambda b, i: (b, i, 0)),
                   per_b(8, 2 * W), per_b(D_HEADS, D_HEAD, D_HEAD), per_b(8, LANE), per_b(8, LANE)],
        out_shape=[jax.ShapeDtypeStruct((bn, t, W), BF16),
                   jax.ShapeDtypeStruct((bn, 8, 2 * W), F32),
                   jax.ShapeDtypeStruct((bn, D_HEADS, D_HEAD, D_HEAD), F32),
                   jax.ShapeDtypeStruct((bn, 8, LANE), F32),
                   jax.ShapeDtypeStruct((bn, 8, LANE), F32)],
        scratch_shapes=[pltpu.VMEM((8, 2 * W), F32), pltpu.VMEM((tm, W), F32), pltpu.VMEM((tm, W), F32),
                        pltpu.VMEM((D_HEADS, D_HEAD, D_HEAD), F32), pltpu.VMEM((8, LANE), F32),
                        pltpu.VMEM((8, LANE), F32)],
        compiler_params=_params(("parallel", "arbitrary")),
        name="mlstm",
    )(u3, u3, u3, u3, hist8, c0, n0p, m0p, *consts)
    out_d, conv8, c, n8, m8 = outs
    return out_d, conv8[:, 8 - (D_CONV - 1):], c, n8[:, :D_HEADS], m8[:, :D_HEADS, 0]


def _merge_body(x_ref, g_ref, a_ref, b_ref, c_ref, d_ref, wg_ref, wb_ref, wo_ref, o_ref):
    g = g_ref[...].astype(BF16)
    merged = None
    for i, br in enumerate((a_ref, b_ref, c_ref, d_ref)):
        term = _sigmoid(_dot(g, wg_ref[i])) * _dot(br[...], wb_ref[i])
        merged = term if merged is None else merged + term
    o_ref[...] = x_ref[...] + _dot(merged.astype(BF16), wo_ref[...])


def _merge(x2d, u2d, branches, wg, wb, wo, tm):
    n = x2d.shape[0]
    once = lambda a: pl.BlockSpec(a.shape, lambda i: (0,) * a.ndim, pipeline_mode=pl.Buffered(1))
    bblk = pl.BlockSpec((tm, W), lambda i: (i, 0))
    xblk = pl.BlockSpec((tm, D_MODEL), lambda i: (i, 0))
    return pl.pallas_call(
        _merge_body,
        grid=(n // tm,),
        in_specs=[xblk, pl.BlockSpec((tm, GATE_RANK), lambda i: (i, COL_G // GATE_RANK))]
                 + [bblk] * 4 + [once(wg), once(wb), once(wo)],
        out_specs=xblk,
        out_shape=jax.ShapeDtypeStruct((n, D_MODEL), F32),
        compiler_params=_params(("parallel",)),
        name="merge",
    )(x2d, u2d, *branches, wg, wb, wo)


def _ffn_epilogue(acc, gf_ref, final_norm):
    return _rmsnorm(acc, gf_ref[...]) if final_norm else acc


def _ffn_dense_body(x_ref, g_ref, wg_ref, wu_ref, wd_ref, gf_ref, o_ref, h_scr, acc, *, final_norm):
    j = pl.program_id(1)

    @pl.when(j == 0)
    def _():
        x = x_ref[...]
        h_scr[...] = _rmsnorm(x, g_ref[...]).astype(BF16)
        acc[...] = x

    h = h_scr[...]
    a = _dot(h, wg_ref[...])
    act = (a * _sigmoid(a)) * _dot(h, wu_ref[...])
    acc[...] += _dot(act.astype(BF16), wd_ref[...])

    @pl.when(j == pl.num_programs(1) - 1)
    def _():
        o_ref[...] = _ffn_epilogue(acc[...], gf_ref, final_norm)


def _ffn_dense(x2d, g, wgate, wup, wdown, g_final, final_norm, tm, tf=512):
    n = x2d.shape[0]
    xblk = pl.BlockSpec((tm, D_MODEL), lambda i, j: (i, 0))
    vec = pl.BlockSpec((1, D_MODEL), lambda i, j: (0, 0))
    return pl.pallas_call(
        functools.partial(_ffn_dense_body, final_norm=final_norm),
        grid=(n // tm, D_FF // tf),
        in_specs=[xblk, vec,
                  pl.BlockSpec((D_MODEL, tf), lambda i, j: (0, j)),
                  pl.BlockSpec((D_MODEL, tf), lambda i, j: (0, j)),
                  pl.BlockSpec((tf, D_MODEL), lambda i, j: (j, 0)), vec],
        out_specs=xblk,
        out_shape=jax.ShapeDtypeStruct((n, D_MODEL), F32),
        scratch_shapes=[pltpu.VMEM((tm, D_MODEL), BF16), pltpu.VMEM((tm, D_MODEL), F32)],
        compiler_params=_params(("parallel", "arbitrary")),
        name="ffn_dense",
    )(x2d, g.reshape(1, D_MODEL), wgate, wup, wdown, g_final.reshape(1, D_MODEL))


def _ffn_moe_body(x_ref, g_ref, rt_ref, wg_ref, wu_ref, wd_ref, gf_ref, o_ref,
                  h_scr, acc, gates, *, final_norm):
    e = pl.program_id(1)

    @pl.when(e == 0)
    def _():
        x = x_ref[...]
        h = _rmsnorm(x, g_ref[...])
        h_scr[...] = h.astype(BF16)
        acc[...] = x
        h_hi, h_lo = _split2(h)
        r_hi, r_lo = _split2(rt_ref[...])
        logits = _dot(h_hi, r_hi) + _dot(h_lo, r_hi) + _dot(h_hi, r_lo)
        lane = lax.broadcasted_iota(jnp.int32, logits.shape, 1)
        logits = jnp.where(lane < N_EXPERTS, logits, -jnp.inf)
        v1 = jnp.max(logits, axis=-1, keepdims=True)
        i1 = jnp.min(jnp.where(logits == v1, lane, LANE), axis=-1, keepdims=True)
        rest = jnp.where(lane == i1, -jnp.inf, logits)
        v2 = jnp.max(rest, axis=-1, keepdims=True)
        i2 = jnp.min(jnp.where(rest == v2, lane, LANE), axis=-1, keepdims=True)
        e2 = jnp.exp(v2 - v1)
        den = 1.0 + e2
        gates[...] = jnp.where(lane == i1, 1.0 / den, 0.0) + jnp.where(lane == i2, e2 / den, 0.0)

    h = h_scr[...]
    lane = lax.broadcasted_iota(jnp.int32, gates.shape, 1)
    gate = jnp.sum(jnp.where(lane == e, gates[...], 0.0), axis=-1, keepdims=True)
    a = _dot(h, wg_ref[...])
    act = (a * _sigmoid(a)) * _dot(h, wu_ref[...])
    acc[...] += gate * _dot(act.astype(BF16), wd_ref[...])

    @pl.when(e == pl.num_programs(1) - 1)
    def _():
        o_ref[...] = _ffn_epilogue(acc[...], gf_ref, final_norm)


def _ffn_moe(x2d, g, router, wgate, wup, wdown, g_final, final_norm, tm):
    n = x2d.shape[0]
    xblk = pl.BlockSpec((tm, D_MODEL), lambda i, j: (i, 0))
    vec = pl.BlockSpec((1, D_MODEL), lambda i, j: (0, 0))
    return pl.pallas_call(
        functools.partial(_ffn_moe_body, final_norm=final_norm),
        grid=(n // tm, N_EXPERTS),
        in_specs=[xblk, vec,
                  pl.BlockSpec((D_MODEL, LANE), lambda i, j: (0, 0)),
                  pl.BlockSpec((None, D_MODEL, EXPERT_PAD), lambda i, j: (j, 0, 0)),
                  pl.BlockSpec((None, D_MODEL, EXPERT_PAD), lambda i, j: (j, 0, 0)),
                  pl.BlockSpec((None, EXPERT_PAD, D_MODEL), lambda i, j: (j, 0, 0)), vec],
        out_specs=xblk,
        out_shape=jax.ShapeDtypeStruct((n, D_MODEL), F32),
        scratch_shapes=[pltpu.VMEM((tm, D_MODEL), BF16), pltpu.VMEM((tm, D_MODEL), F32),
                        pltpu.VMEM((tm, LANE), F32)],
        compiler_params=_params(("parallel", "arbitrary")),
        name="ffn_moe",
    )(x2d, g.reshape(1, D_MODEL), router, wgate, wup, wdown, g_final.reshape(1, D_MODEL))


def _pack_w_in(w):
    a, b, c, d = w[:, :OFF_B], w[:, OFF_B:OFF_C], w[:, OFF_C:OFF_D], w[:, OFF_D:OFF_G]
    g = w[:, OFF_G:]
    zeros = lambda k: jnp.zeros((D_MODEL, k), w.dtype)
    lora = a[:, 3 * W:]
    gates = d[:, 4 * W:]
    cols = [d[:, :4 * W], a[:, :3 * W], b, c,
            lora, zeros(LORA_PAD - lora.shape[1]), g,
            gates, zeros(LANE - gates.shape[1]), zeros(WP - COL_DG - LANE)]
    return jnp.concatenate(cols, axis=1).astype(BF16)


def _layer_params(l, wts):
    (norm_mix, norm_ffn, norm_final, w_in, rwkv_mu, rwkv_w0, rwkv_w2, rwkv_a0, rwkv_a2,
     rwkv_g2, rwkv_k_k, rwkv_k_a, rwkv_r_k, rwkv_ln_w, rwkv_ln_b, pool_w, pool_scale,
     mlstm_conv_w, mlstm_conv_b, mlstm_i_bias, mlstm_f_bias, mlstm_norm_w,
     w_branch, w_merge_gate, w_out, ffn_w_gate, ffn_w_up, ffn_w_down,
     moe_router, moe_w_gate, moe_w_up, moe_w_down) = wts
    n_l = sum(A_LORA)
    row = lambda v: v.reshape(1, -1)

    def lora_pad(wm, start):
        return jnp.pad(wm, ((start, LORA_PAD - start - wm.shape[0]), (0, 0))).astype(BF16)

    hid = jnp.arange(W) // A_HEAD
    p = {
        "norm_mix": norm_mix[l], "norm_ffn": norm_ffn[l],
        "wp": _pack_w_in(w_in[l]),
        "rwkv": {
            "mu_rkv": rwkv_mu[l, :3 * W].reshape(3, W),
            "mu_l": jnp.pad(rwkv_mu[l, 3 * W:], (0, LORA_PAD - n_l)).reshape(1, LORA_PAD),
            "w0": row(rwkv_w0[l]), "a0": row(rwkv_a0[l]),
            "k_k": row(rwkv_k_k[l]), "k_a": row(rwkv_k_a[l]),
            "w2p": lora_pad(rwkv_w2[l], 0),
            "a2p": lora_pad(rwkv_a2[l], A_LORA[0]),
            "g2p": lora_pad(rwkv_g2[l], A_LORA[0] + A_LORA[1]),
            "head_ones": (hid[:, None] == hid[None, :]).astype(BF16),
            "r_k": row(rwkv_r_k[l]), "ln_w": row(rwkv_ln_w[l]), "ln_b": row(rwkv_ln_b[l]),
        },
        "pool_w": pool_w[l], "pool_scale": pool_scale[l],
        "mlstm": {
            "conv_w": mlstm_conv_w[l], "conv_b": row(mlstm_conv_b[l]),
            "gate_bias": jnp.pad(jnp.concatenate([mlstm_i_bias[l], mlstm_f_bias[l]]),
                                 (0, LANE - 2 * D_HEADS)).reshape(1, LANE),
            "norm_w": row(mlstm_norm_w[l]),
        },
        "wg": w_merge_gate[l].astype(BF16), "wb": w_branch[l].astype(BF16), "wo": w_out[l].astype(BF16),
    }
    if l % 2 == 0:
        p["ffn"] = (ffn_w_gate[l // 2].astype(BF16), ffn_w_up[l // 2].astype(BF16),
                    ffn_w_down[l // 2].astype(BF16))
    else:
        pe = EXPERT_PAD - D_FF_EXPERT
        p["moe"] = (jnp.pad(moe_router[l // 2], ((0, 0), (0, LANE - N_EXPERTS))),
                    jnp.pad(moe_w_gate[l // 2], ((0, 0), (0, 0), (0, pe))).astype(BF16),
                    jnp.pad(moe_w_up[l // 2], ((0, 0), (0, 0), (0, pe))).astype(BF16),
                    jnp.pad(moe_w_down[l // 2], ((0, 0), (0, pe), (0, 0))).astype(BF16))
    return p


def _heads_major(x3, heads):
    bn, t, c = x3.shape
    return x3.reshape(bn, t, heads, c // heads).transpose(0, 2, 1, 3)


def _heads_minor(x4):
    bn, hh, t, d = x4.shape
    return x4.transpose(0, 2, 1, 3).reshape(bn, t, hh * d)


def _slot(u3, name):
    s = SLOT[name] * W
    return u3[:, :, s:s + W]


def _run_trunk(x, start_pos, cache_k, cache_v, wkv0, shift0, pool0, conv0, c0, n0, m0,
               layers, norm_final):
    bn, t, _ = x.shape
    n = bn * t
    tm_tok = min(256, t)
    tm_row = min(512, n)
    x2 = x.reshape(n, D_MODEL)
    ks, vs, wkvs, shifts, pools, convs, cs, ns, ms = ([] for _ in range(9))
    for l, p in enumerate(layers):
        u2 = _inproj(x2, p["norm_mix"], p["wp"])
        u3 = u2.reshape(bn, t, WP)

        r, k2, v, kk, b, lw, g = _rwkv_prep(u3, shift0[l], p["rwkv"], tm_tok)
        hm = lambda a: _heads_major(a, A_HEADS)
        y, wkv = _rwkv_chunk(hm(r), hm(k2), hm(v), hm(kk), hm(b), hm(lw), wkv0[l], tm_tok)
        flat = lambda a: a.reshape(n, W)
        out_a = _rwkv_post(flat(_heads_minor(y)), flat(r), flat(k2), flat(v), flat(g), p["rwkv"], tm_row)

        out_b = _pool(u3, pool0[l], p["pool_w"], p["pool_scale"], start_pos, tm_tok).reshape(n, W)

        q_c, k_c, v_c = _slot(u3, "cq"), _slot(u3, "ck"), _slot(u3, "cv")
        hb = lambda a: _heads_major(a.astype(BF16), C_HEADS)
        if cache_k is None:
            out_c = _sb_attention(hb(q_c), hb(k_c), hb(v_c), 0, min(SB_TQ, t))
        else:
            past = cache_k.shape[2]
            k_all = jnp.concatenate([cache_k[l].reshape(bn, past, W), k_c], axis=1)
            v_all = jnp.concatenate([cache_v[l].reshape(bn, past, W), v_c], axis=1)
            out_c = _sb_attention(hb(q_c), hb(k_all), hb(v_all), past, t)
        out_c = _heads_minor(out_c).reshape(n, W)

        out_d, conv_new, c_new, n_new, m_new = _mlstm(u3, conv0[l], c0[l], n0[l], m0[l], p["mlstm"], tm_tok)

        x2 = _merge(x2, u2, (out_a, out_b, out_c, out_d.reshape(n, W)), p["wg"], p["wb"], p["wo"],
                    min(256, n))
        last = l == len(layers) - 1
        if "ffn" in p:
            x2 = _ffn_dense(x2, p["norm_ffn"], *p["ffn"], norm_final, last, tm_row)
        else:
            x2 = _ffn_moe(x2, p["norm_ffn"], *p["moe"], norm_final, last, tm_row)

        ks.append(k_c.reshape(bn, t, C_HEADS, C_HEAD))
        vs.append(v_c.reshape(bn, t, C_HEADS, C_HEAD))
        wkvs.append(wkv)
        u_last = u3[:, -1]
        shifts.append(jnp.concatenate(
            [u_last[:, SLOT["ar"] * W:SLOT["ar"] * W + 3 * W], u_last[:, COL_LORA:COL_LORA + sum(A_LORA)]], axis=1))
        pools.append(jnp.concatenate([pool0[l], _slot(u3, "b")[:, -B_HIST:]], axis=1)[:, -B_HIST:])
        convs.append(conv_new)
        cs.append(c_new)
        ns.append(n_new)
        ms.append(m_new)
    st = lambda lst: jnp.stack(lst, axis=0)
    return (x2.reshape(bn, t, D_MODEL),
            (st(ks), st(vs), st(wkvs), st(shifts), st(pools), st(convs), st(cs), st(ns), st(ms)))


def kernel(x_prompt, x_sample, cache_sb_k, cache_sb_v, state_rwkv_wkv, state_rwkv_shift, state_pool,
           state_mlstm_conv, state_mlstm_c, state_mlstm_n, state_mlstm_m, norm_mix, norm_ffn, norm_final,
           w_in, rwkv_mu, rwkv_w0, rwkv_w2, rwkv_a0, rwkv_a2, rwkv_g2, rwkv_k_k, rwkv_k_a, rwkv_r_k,
           rwkv_ln_w, rwkv_ln_b, pool_w, pool_scale, mlstm_conv_w, mlstm_conv_b, mlstm_i_bias,
           mlstm_f_bias, mlstm_norm_w, w_branch, w_merge_gate, w_out, ffn_w_gate, ffn_w_up, ffn_w_down,
           moe_router, moe_w_gate, moe_w_up, moe_w_down):
    wts = (norm_mix, norm_ffn, norm_final, w_in, rwkv_mu, rwkv_w0, rwkv_w2, rwkv_a0, rwkv_a2,
           rwkv_g2, rwkv_k_k, rwkv_k_a, rwkv_r_k, rwkv_ln_w, rwkv_ln_b, pool_w, pool_scale,
           mlstm_conv_w, mlstm_conv_b, mlstm_i_bias, mlstm_f_bias, mlstm_norm_w,
           w_branch, w_merge_gate, w_out, ffn_w_gate, ffn_w_up, ffn_w_down,
           moe_router, moe_w_gate, moe_w_up, moe_w_down)
    depth = w_in.shape[0]
    layers = [_layer_params(l, wts) for l in range(depth)]
    bp = x_prompt.shape[0]
    z = lambda *s: jnp.zeros((depth, bp) + s, F32)
    y_p, st_p = _run_trunk(
        x_prompt, 0, None, None, z(A_HEADS, A_HEAD, A_HEAD), z(A_IN), z(B_HIST, W),
        z(D_CONV - 1, 2 * W), z(D_HEADS, D_HEAD, D_HEAD), z(D_HEADS, D_HEAD), z(D_HEADS),
        layers, norm_final)
    y_s, st_s = _run_trunk(
        x_sample, cache_sb_k.shape[2], cache_sb_k, cache_sb_v, state_rwkv_wkv, state_rwkv_shift,
        state_pool, state_mlstm_conv, state_mlstm_c, state_mlstm_n, state_mlstm_m,
        layers, norm_final)
    return (y_p, y_s) + tuple(st_p) + tuple(st_s)
```

```python
import functools

import jax
import jax.numpy as jnp
from jax import lax
from jax.experimental import pallas as pl
from jax.experimental.pallas import tpu as pltpu

F32 = jnp.float32
BF16 = jnp.bfloat16

D_MODEL = 2048
DEPTH = 2
W = 512
A_HEAD, A_HEADS = 64, 8
A_LORA = (32, 32, 96)
A_IN = 3 * W + sum(A_LORA)
A_GN_EPS = 64e-5
B_WINDOWS = (2, 4, 8, 16)
B_HIST = 15
C_HEAD, C_HEADS = 64, 8
D_HEADS, D_HEAD = 4, 128
D_CONV = 4
D_NORM_EPS = 1e-6
GATE_RANK = 256
OFF_B = A_IN
OFF_C = OFF_B + W
OFF_D = OFF_C + 3 * W
OFF_G = OFF_D + 4 * W + 2 * D_HEADS
IN_WIDTH = OFF_G + GATE_RANK
D_FF = 5632
N_EXPERTS = 8
D_FF_EXPERT = D_FF // N_EXPERTS
NORM_EPS = 1e-6

LANE = 128
V7X_VMEM_BYTES = 64 * 1024 * 1024
VMEM_LIMIT = V7X_VMEM_BYTES - 8 * 1024 * 1024

SLOT = {name: i for i, name in enumerate(
    ("dq", "dk", "dv", "do", "ar", "ak", "av", "b", "cq", "ck", "cv"))}
COL_LORA = 11 * W
COL_G = COL_LORA + 256
COL_DG = COL_G + 256
IN_TN = 1280
WP = 6400
LORA_PAD = 256
EXPERT_PAD = 768

RWKV_CHUNK = 64
MLSTM_CHUNK = 64
SB_BLOCK = 128
SB_TQ = 512
SB_DEAD = -120.0


def _params(sem, vmem=VMEM_LIMIT):
    return pltpu.CompilerParams(dimension_semantics=sem, vmem_limit_bytes=vmem)


def _split2(x):
    hi = x.astype(BF16)
    lo = (x - hi.astype(F32)).astype(BF16)
    return hi, lo


def _split3(x):
    hi = x.astype(BF16)
    r1 = x - hi.astype(F32)
    mid = r1.astype(BF16)
    lo = (r1 - mid.astype(F32)).astype(BF16)
    return hi, mid, lo


def _dot(a, b):
    return jnp.dot(a, b, preferred_element_type=F32)


def _dot_nt(a, b):
    return lax.dot_general(a, b, (((1,), (1,)), ((), ())), preferred_element_type=F32)


def _bdot(a, b):
    return lax.dot_general(a, b, (((2,), (1,)), ((0,), (0,))), preferred_element_type=F32)


def _bdot_nt(a, b):
    return lax.dot_general(a, b, (((2,), (2,)), ((0,), (0,))), preferred_element_type=F32)


def _bdot_tn(a, b):
    return lax.dot_general(a, b, (((1,), (1,)), ((0,), (0,))), preferred_element_type=F32)


def _dot_rhs01(x, m01):
    hi, mid, lo = _split3(x)
    return _dot(hi, m01) + _dot(mid, m01) + _dot(lo, m01)


def _dot_lhs01(m01, x):
    hi, mid, lo = _split3(x)
    return _dot(m01, hi) + _dot(m01, mid) + _dot(m01, lo)


def _sigmoid(x):
    return 1.0 / (1.0 + jnp.exp(-x))


def _log_sigmoid(x):
    return jnp.minimum(x, 0.0) - jnp.log1p(jnp.exp(-jnp.abs(x)))


def _rmsnorm(x, g):
    ms = jnp.mean(x * x, axis=-1, keepdims=True)
    return x * lax.rsqrt(ms + NORM_EPS) * g


def _inproj_body(x_ref, g_ref, w_ref, o_ref, h_scr):
    @pl.when(pl.program_id(1) == 0)
    def _():
        h_scr[...] = _rmsnorm(x_ref[...], g_ref[...]).astype(BF16)

    o_ref[...] = _dot(h_scr[...], w_ref[...])


def _inproj(x2d, g, wp):
    n = x2d.shape[0]
    tm = min(512, n)
    return pl.pallas_call(
        _inproj_body,
        grid=(n // tm, WP // IN_TN),
        in_specs=[pl.BlockSpec((tm, D_MODEL), lambda i, j: (i, 0)),
                  pl.BlockSpec((1, D_MODEL), lambda i, j: (0, 0)),
                  pl.BlockSpec((D_MODEL, IN_TN), lambda i, j: (0, j))],
        out_specs=pl.BlockSpec((tm, IN_TN), lambda i, j: (i, j)),
        out_shape=jax.ShapeDtypeStruct((n, WP), F32),
        scratch_shapes=[pltpu.VMEM((tm, D_MODEL), BF16)],
        compiler_params=_params(("parallel", "arbitrary")),
        name="inproj",
    )(x2d, g.reshape(1, D_MODEL), wp)


def _shift_rows(u, carry_row):
    rolled = pltpu.roll(u, 1, axis=0)
    row = lax.broadcasted_iota(jnp.int32, u.shape, 0)
    return jnp.where(row == 0, carry_row, rolled)


def _rwkv_prep_body(ur_ref, uk_ref, uv_ref, ul_ref, sr_ref, sk_ref, sv_ref, sl_ref,
                    mu_ref, mul_ref, w0_ref, a0_ref, kk_ref, ka_ref,
                    w2_ref, a2_ref, g2_ref, ones_ref,
                    r_out, k_out, v_out, kk_out, b_out, lw_out, g_out,
                    cr, ck, cv, cl):
    tm = ur_ref.shape[0]

    @pl.when(pl.program_id(1) == 0)
    def _():
        cr[0:1, :] = sr_ref[...]
        ck[0:1, :] = sk_ref[...]
        cv[0:1, :] = sv_ref[...]
        cl[0:1, :] = sl_ref[...]

    def mix(u_ref, c_ref, mu):
        u = u_ref[...]
        prev = _shift_rows(u, c_ref[0:1, :])
        c_ref[0:1, :] = u[tm - 1:tm, :]
        return u + (prev - u) * mu

    r = mix(ur_ref, cr, mu_ref[0:1, :])
    k = mix(uk_ref, ck, mu_ref[1:2, :])
    v = mix(uv_ref, cv, mu_ref[2:3, :])
    xl = mix(ul_ref, cl, mul_ref[...])

    w_lora = _dot(jnp.tanh(xl).astype(BF16), w2_ref[...])
    a_lora = _dot(xl.astype(BF16), a2_ref[...])
    g = _dot(_sigmoid(xl).astype(BF16), g2_ref[...])

    y = -(w0_ref[...] + w_lora)
    w_log = -(jnp.maximum(y, 0.0) + jnp.log1p(jnp.exp(-jnp.abs(y)))) - 0.5
    log_decay = -jnp.exp(w_log)
    a = _sigmoid(a0_ref[...] + a_lora)

    kk = k * kk_ref[...]
    hi, lo = _split2(kk * kk)
    ss = _dot(hi, ones_ref[...]) + _dot(lo, ones_ref[...])
    kk = kk / jnp.maximum(jnp.sqrt(ss), 1e-12)
    k2 = k * (1.0 + (a - 1.0) * ka_ref[...])

    r_out[...] = r
    k_out[...] = k2
    v_out[...] = v
    kk_out[...] = kk
    b_out[...] = kk * a
    lw_out[...] = log_decay
    g_out[...] = g


def _rwkv_prep(u3, shift_prev, p, tm):
    bn, t, _ = u3.shape
    sr = shift_prev[:, None, 0:W]
    sk = shift_prev[:, None, W:2 * W]
    sv = shift_prev[:, None, 2 * W:3 * W]
    sl = jnp.pad(shift_prev[:, None, 3 * W:], ((0, 0), (0, 0), (0, LORA_PAD - sum(A_LORA))))
    row = lambda c: pl.BlockSpec((None, 1, c), lambda b, i: (b, 0, 0))
    full = lambda a: pl.BlockSpec(a.shape, lambda b, i: (0,) * a.ndim)
    ublk = lambda slot: pl.BlockSpec((None, tm, W), lambda b, i, s=slot: (b, i, s))
    consts = (p["mu_rkv"], p["mu_l"], p["w0"], p["a0"], p["k_k"], p["k_a"],
              p["w2p"], p["a2p"], p["g2p"], p["head_ones"])
    out = jax.ShapeDtypeStruct((bn, t, W), F32)
    oblk = pl.BlockSpec((None, tm, W), lambda b, i: (b, i, 0))
    return pl.pallas_call(
        _rwkv_prep_body,
        grid=(bn, t // tm),
        in_specs=[ublk(SLOT["ar"]), ublk(SLOT["ak"]), ublk(SLOT["av"]),
                  pl.BlockSpec((None, tm, LORA_PAD), lambda b, i: (b, i, COL_LORA // LORA_PAD)),
                  row(W), row(W), row(W), row(LORA_PAD)] + [full(c) for c in consts],
        out_specs=[oblk] * 7,
        out_shape=[out] * 7,
        scratch_shapes=[pltpu.VMEM((8, W), F32)] * 3 + [pltpu.VMEM((8, LORA_PAD), F32)],
        compiler_params=_params(("parallel", "arbitrary")),
        name="rwkv_prep",
    )(u3, u3, u3, u3, sr, sk, sv, sl, *consts)


def _rwkv_chunk_body(r_ref, k_ref, v_ref, kk_ref, b_ref, lw_ref, s0_ref,
                     y_ref, sout_ref, s_scr, *, n_chunks):
    c_len = RWKV_CHUNK

    @pl.when(pl.program_id(1) == 0)
    def _():
        s_scr[...] = s0_ref[...]

    hh = A_HEADS
    row = lax.broadcasted_iota(jnp.int32, (hh, c_len, c_len), 1)
    col = lax.broadcasted_iota(jnp.int32, (hh, c_len, c_len), 2)
    tri_incl = (row >= col).astype(BF16)
    lower = row >= col
    strict = row > col
    eye = (row == col).astype(F32)

    def chunk(c, carry):
        off = pl.multiple_of(c * c_len, c_len)
        sl = pl.ds(off, c_len)
        r = r_ref[:, sl, :]
        k = k_ref[:, sl, :]
        v = v_ref[:, sl, :]
        kk = kk_ref[:, sl, :]
        b = b_ref[:, sl, :]
        logw = lw_ref[:, sl, :]
        w_hi, w_mid, w_lo = _split3(logw)
        lw = _bdot(tri_incl, w_hi) + _bdot(tri_incl, w_mid) + _bdot(tri_incl, w_lo)
        lw_prev = lw - logw
        lw_last = lw[:, c_len - 1:c_len, :]
        e_neg = jnp.exp(-lw)
        e_end = jnp.exp(lw_last - lw)
        kkm = (kk * jnp.exp(lw_prev)).astype(BF16)
        rm = r * jnp.exp(lw)
        kp = (k * e_neg).astype(BF16)
        bp = (b * e_neg).astype(BF16)
        kpp = (k * e_end).astype(BF16)
        bpp = (b * e_end).astype(BF16)
        vb = v.astype(BF16)
        rmb = rm.astype(BF16)

        a_vk = jnp.where(strict, _bdot_nt(kkm, kp), 0.0)
        a_pb = jnp.where(strict, _bdot_nt(kkm, bp), 0.0)
        rk = jnp.where(lower, _bdot_nt(rmb, kp), 0.0)
        rb = jnp.where(lower, _bdot_nt(rmb, bp), 0.0)

        n_pow = -a_pb
        t_inv = eye + n_pow
        for _ in range(5):
            nb = n_pow.astype(BF16)
            n_pow = _bdot(nb, nb)
            t_inv = t_inv + _bdot(t_inv.astype(BF16), n_pow.astype(BF16))
        tb = t_inv.astype(BF16)

        kktb = _bdot(tb, kkm).astype(BF16)
        pvb = _bdot(tb, _bdot(a_vk.astype(BF16), vb).astype(BF16)).astype(BF16)
        rbb = rb.astype(BF16)
        m_mat = eye * jnp.exp(lw_last) - _bdot_tn(kktb, bpp)
        n_mat = _bdot_tn(vb, kpp) - _bdot_tn(pvb, bpp)
        q_mat = rm - _bdot(rbb, kktb)
        y0 = _bdot(rk.astype(BF16), vb) - _bdot(rbb, pvb)

        s_hi, s_lo = _split2(s_scr[...])
        m_hi, m_lo = _split2(m_mat)
        y_ref[:, sl, :] = _bdot_nt(q_mat.astype(BF16), s_hi) + y0
        s_scr[...] = _bdot(s_hi, m_hi) + _bdot(s_lo, m_hi) + _bdot(s_hi, m_lo) + n_mat
        return carry

    lax.fori_loop(0, n_chunks, chunk, 0)
    sout_ref[...] = s_scr[...]


def _rwkv_chunk(r, k, v, kk, b, lw, s0, tm):
    bn, hh, t, d = r.shape
    blk = pl.BlockSpec((None, hh, tm, d), lambda bi, i: (bi, 0, i, 0))
    sblk = pl.BlockSpec((None, hh, d, d), lambda bi, i: (bi, 0, 0, 0))
    return pl.pallas_call(
        functools.partial(_rwkv_chunk_body, n_chunks=tm // RWKV_CHUNK),
        grid=(bn, t // tm),
        in_specs=[blk] * 6 + [sblk],
        out_specs=[blk, sblk],
        out_shape=[jax.ShapeDtypeStruct((bn, hh, t, d), F32),
                   jax.ShapeDtypeStruct((bn, hh, d, d), F32)],
        scratch_shapes=[pltpu.VMEM((hh, d, d), F32)],
        compiler_params=_params(("parallel", "arbitrary")),
        name="rwkv_chunk",
    )(r, k, v, kk, b, lw, s0)


def _rwkv_post_body(y_ref, r_ref, k_ref, v_ref, g_ref, rk_ref, lnw_ref, lnb_ref, ones_ref, o_ref):
    y = y_ref[...]
    ones = ones_ref[...]

    def head_sum(x):
        hi, lo = _split2(x)
        return _dot(hi, ones) + _dot(lo, ones)

    mu = head_sum(y) * (1.0 / A_HEAD)
    yc = y - mu
    var = head_sum(yc * yc) * (1.0 / A_HEAD)
    yn = yc * lax.rsqrt(var + A_GN_EPS) * lnw_ref[...] + lnb_ref[...]
    bonus = head_sum(r_ref[...] * k_ref[...] * rk_ref[...]) * v_ref[...]
    o_ref[...] = ((yn + bonus) * g_ref[...]).astype(BF16)


def _rwkv_post(y, r, k, v, g, p, tm):
    n = y.shape[0]
    blk = pl.BlockSpec((tm, W), lambda i: (i, 0))
    full = lambda a: pl.BlockSpec(a.shape, lambda i: (0,) * a.ndim)
    consts = (p["r_k"], p["ln_w"], p["ln_b"], p["head_ones"])
    return pl.pallas_call(
        _rwkv_post_body,
        grid=(n // tm,),
        in_specs=[blk] * 5 + [full(c) for c in consts],
        out_specs=blk,
        out_shape=jax.ShapeDtypeStruct((n, W), BF16),
        compiler_params=_params(("parallel",)),
        name="rwkv_post",
    )(y, r, k, v, g, *consts)


def _pool_body(u_ref, hist_ref, w_ref, scale_ref, o_ref, carry, *, start_pos):
    tm = u_ref.shape[0]
    i = pl.program_id(1)

    @pl.when(i == 0)
    def _():
        carry[...] = hist_ref[...]

    x = u_ref[...]
    ext = jnp.concatenate([carry[...], x], axis=0)
    carry[...] = x[tm - 16:tm, :]
    s = ext
    sums = []
    for sh in (1, 2, 4, 8):
        s = s + pltpu.roll(s, sh, axis=0)
        sums.append(s[16:16 + tm, :])
    pos = start_pos + i * tm + lax.broadcasted_iota(jnp.int32, (tm, 1), 0)
    for gi, wlen in enumerate(B_WINDOWS):
        cs = slice(gi * LANE, (gi + 1) * LANE)
        cnt = jnp.minimum(wlen, pos + 1).astype(F32)
        d = sums[gi][:, cs] / cnt - x[:, cs]
        o_ref[:, cs] = (_dot(d.astype(BF16), w_ref[gi]) * scale_ref[:, cs]).astype(BF16)


def _pool(u3, hist, pool_w, pool_scale, start_pos, tm):
    bn, t, _ = u3.shape
    hist16 = jnp.pad(hist, ((0, 0), (1, 0), (0, 0)))
    return pl.pallas_call(
        functools.partial(_pool_body, start_pos=start_pos),
        grid=(bn, t // tm),
        in_specs=[pl.BlockSpec((None, tm, W), lambda b, i: (b, i, SLOT["b"])),
                  pl.BlockSpec((None, 16, W), lambda b, i: (b, 0, 0)),
                  pl.BlockSpec((4, LANE, LANE), lambda b, i: (0, 0, 0)),
                  pl.BlockSpec((1, W), lambda b, i: (0, 0))],
        out_specs=pl.BlockSpec((None, tm, W), lambda b, i: (b, i, 0)),
        out_shape=jax.ShapeDtypeStruct((bn, t, W), BF16),
        scratch_shapes=[pltpu.VMEM((16, W), F32)],
        compiler_params=_params(("parallel", "arbitrary")),
        name="pool",
    )(u3, hist16, pool_w.astype(BF16), pool_scale.reshape(1, W))


def _sb_body(q_ref, k_ref, v_ref, o_ref, acc_scr, o_scr, *, tq, q_off, n_masked):
    qi = pl.program_id(2)
    q = q_ref[...] * (C_HEAD ** -0.5)
    q_start = q_off + qi * tq
    first_masked = q_start // SB_BLOCK

    jj = lax.broadcasted_iota(jnp.int32, (2 * SB_BLOCK, 2 * SB_BLOCK), 0) % SB_BLOCK
    ss = lax.broadcasted_iota(jnp.int32, (2 * SB_BLOCK, 2 * SB_BLOCK), 1)
    cs = jnp.where(jnp.logical_or(ss >= SB_BLOCK, jj > ss), -1.0, 0.0).astype(BF16)
    aligned = q_off % SB_BLOCK == 0 and tq % SB_BLOCK == 0

    def block(kb, r0, masked):
        rows = slice(r0, tq)
        ks = pl.multiple_of(kb * SB_BLOCK, SB_BLOCK)
        kblk = k_ref[pl.ds(ks, SB_BLOCK), :]
        vblk = v_ref[pl.ds(ks, SB_BLOCK), :]
        z = _dot_nt(q[rows], kblk)
        zp = jnp.maximum(z, 0.0)
        zn = jnp.minimum(z, 0.0)
        l1p = jnp.log(1.0 + jnp.exp(zn - zp))
        nlf = zp + l1p
        s = zn - l1p
        if masked:
            q_pos = q_start + r0 + lax.broadcasted_iota(jnp.int32, z.shape, 0)
            mask = (ks + lax.broadcasted_iota(jnp.int32, z.shape, 1)) < q_pos
            nlf = jnp.where(mask, nlf, 0.0)
        hi, lo = _split2(nlf)
        lt = _dot(jnp.concatenate([hi, lo], axis=1), cs)
        att = jnp.exp(s + lt[:, :SB_BLOCK] + acc_scr[rows, :])
        if masked:
            att = jnp.where(mask, att, 0.0)
        o_scr[rows, :] += _dot(att.astype(BF16), vblk)
        acc_scr[rows, :] += lt[:, SB_BLOCK:]

    acc_scr[...] = jnp.zeros_like(acc_scr)
    o_scr[...] = jnp.zeros_like(o_scr)
    for m in range(n_masked - 1, -1, -1):
        block(first_masked + m, m * SB_BLOCK if aligned else 0, True)

    def live(i):
        return jnp.logical_and(i < first_masked, jnp.max(acc_scr[...]) > SB_DEAD)

    def step(i):
        block(first_masked - 1 - i, 0, False)
        return i + 1

    lax.while_loop(live, step, jnp.int32(0))
    o_ref[...] = o_scr[...].astype(BF16)


def _sb_attention(q, k, v, q_off, tq):
    bn, hh, t, d = q.shape
    sk = k.shape[2]
    sk_pad = -(-sk // SB_BLOCK) * SB_BLOCK
    if sk_pad != sk:
        pad = ((0, 0), (0, 0), (0, sk_pad - sk), (0, 0))
        k, v = jnp.pad(k, pad), jnp.pad(v, pad)
    assert tq % SB_BLOCK == 0 or t == tq
    n_masked = -(-((q_off % SB_BLOCK) + tq) // SB_BLOCK)
    kv = pl.BlockSpec((None, None, sk_pad, d), lambda b, h, i: (b, h, 0, 0))
    qo = pl.BlockSpec((None, None, tq, d), lambda b, h, i: (b, h, i, 0))
    return pl.pallas_call(
        functools.partial(_sb_body, tq=tq, q_off=q_off, n_masked=n_masked),
        grid=(bn, hh, t // tq),
        in_specs=[qo, kv, kv],
        out_specs=qo,
        out_shape=jax.ShapeDtypeStruct((bn, hh, t, d), BF16),
        scratch_shapes=[pltpu.VMEM((tq, SB_BLOCK), F32), pltpu.VMEM((tq, d), F32)],
        compiler_params=_params(("parallel", "parallel", "arbitrary")),
        name="sb_attention",
    )(q, k, v)


def _mlstm_body(qk_ref, v_ref, o_ref, g_ref, hist_ref, c0_ref, n0_ref, m0_ref,
                cw_ref, cb_ref, gb_ref, nw_ref,
                out_ref, conv_out, c_out, n_out, m_out,
                carry, q_scr, k_scr, c_scr, n_scr, m_scr, *, n_chunks):
    tm = qk_ref.shape[0]
    cl = MLSTM_CHUNK

    @pl.when(pl.program_id(1) == 0)
    def _():
        carry[...] = hist_ref[...]
        c_scr[...] = c0_ref[...]
        n_scr[...] = n0_ref[...]
        m_scr[...] = m0_ref[...]

    x = qk_ref[...]
    ext = jnp.concatenate([carry[...], x], axis=0)
    carry[...] = x[tm - 8:tm, :]
    conv = cb_ref[...] + x * cw_ref[D_CONV - 1:D_CONV, :]
    for sh in range(1, D_CONV):
        conv = conv + pltpu.roll(ext, sh, axis=0)[8:8 + tm, :] * cw_ref[D_CONV - 1 - sh:D_CONV - sh, :]
    conv = conv * _sigmoid(conv)
    q_scr[...] = conv[:, :W]
    k_scr[...] = conv[:, W:] * (D_HEAD ** -0.5)

    row = lax.broadcasted_iota(jnp.int32, (cl, cl), 0)
    col = lax.broadcasted_iota(jnp.int32, (cl, cl), 1)
    causal = row >= col
    tri_incl = causal.astype(BF16)

    def chunk(c, carry_):
        off = pl.multiple_of(c * cl, cl)
        sl = pl.ds(off, cl)
        gpre = g_ref[sl, :] + gb_ref[...]
        lfa = _log_sigmoid(gpre)
        bcol = _dot_lhs01(tri_incl, lfa)
        g_t = gpre.T
        b_t = bcol.T
        for h in range(D_HEADS):
            hs = slice(h * D_HEAD, (h + 1) * D_HEAD)
            q = q_scr[sl, hs]
            k = k_scr[sl, hs]
            v = v_ref[sl, hs]
            ig_col = gpre[:, h:h + 1]
            b_col = bcol[:, D_HEADS + h:D_HEADS + h + 1]
            ig_row = g_t[h:h + 1, :]
            b_row = b_t[D_HEADS + h:D_HEADS + h + 1, :]
            m_prev = m_scr[h:h + 1, 0:1]
            log_d = jnp.where(causal, b_col - b_row + ig_row, -jnp.inf)
            m_inter = b_col + m_prev
            m_t = jnp.maximum(m_inter, jnp.max(log_d, axis=-1, keepdims=True))
            dmat = jnp.exp(log_d - m_t)
            inter = jnp.exp(m_inter - m_t)
            qb = q.astype(BF16)
            vb = v.astype(BF16)
            w_qk = _dot_nt(qb, k.astype(BF16)) * dmat
            num = _dot(w_qk.astype(BF16), vb) + inter * _dot(qb, c_scr[h].astype(BF16))
            qn = jnp.sum(q * n_scr[h:h + 1, :], axis=-1, keepdims=True)
            den = jnp.sum(w_qk, axis=-1, keepdims=True) + inter * qn
            den = jnp.maximum(jnp.abs(den), jnp.exp(-m_t))
            hv = num / den
            m_new = m_t[cl - 1:cl, :]
            b_last = b_col[cl - 1:cl, :]
            decay = jnp.exp(b_last + m_prev - m_new)
            w_s = jnp.exp(b_last - b_col + ig_col - m_new)
            kw = k * w_s
            c_scr[h] = decay * c_scr[h] + _dot(kw.T.astype(BF16), vb)
            n_scr[h:h + 1, :] = decay * n_scr[h:h + 1, :] + jnp.sum(kw, axis=0, keepdims=True)
            m_scr[h:h + 1, :] = jnp.broadcast_to(m_new, (1, LANE))
            mu = jnp.mean(hv, axis=-1, keepdims=True)
            hc = hv - mu
            var = jnp.mean(hc * hc, axis=-1, keepdims=True)
            hn = hc * lax.rsqrt(var + D_NORM_EPS) * nw_ref[:, hs]
            out_ref[sl, hs] = (_sigmoid(o_ref[sl, hs]) * hn).astype(BF16)
        return carry_

    lax.fori_loop(0, n_chunks, chunk, 0)
    conv_out[...] = carry[...]
    c_out[...] = c_scr[...]
    n_out[...] = n_scr[...]
    m_out[...] = m_scr[...]


def _mlstm(u3, conv_hist, c0, n0, m0, p, tm):
    bn, t, _ = u3.shape
    hist8 = jnp.pad(conv_hist, ((0, 0), (8 - (D_CONV - 1), 0), (0, 0)))
    n0p = jnp.pad(n0, ((0, 0), (0, 8 - D_HEADS), (0, 0)))
    m0p = jnp.pad(jnp.broadcast_to(m0[:, :, None], (bn, D_HEADS, LANE)), ((0, 0), (0, 8 - D_HEADS), (0, 0)))
    full = lambda a: pl.BlockSpec(a.shape, lambda b, i: (0,) * a.ndim)
    per_b = lambda *s: pl.BlockSpec((None,) + s, lambda b, i: (b,) + (0,) * len(s))
    consts = (p["conv_w"], p["conv_b"], p["gate_bias"], p["norm_w"])
    outs = pl.pallas_call(
        functools.partial(_mlstm_body, n_chunks=tm // MLSTM_CHUNK),
        grid=(bn, t // tm),
        in_specs=[pl.BlockSpec((None, tm, 2 * W), lambda b, i: (b, i, 0)),
                  pl.BlockSpec((None, tm, W), lambda b, i: (b, i, SLOT["dv"])),
                  pl.BlockSpec((None, tm, W), lambda b, i: (b, i, SLOT["do"])),
                  pl.BlockSpec((None, tm, LANE), lambda b, i: (b, i, COL_DG // LANE)),
                  per_b(8, 2 * W), per_b(D_HEADS, D_HEAD, D_HEAD), per_b(8, LANE), per_b(8, LANE)]
                 + [full(c) for c in consts],
        out_specs=[pl.BlockSpec((None, tm, W), lambda b, i: (b, i, 0)),
                   per_b(8, 2 * W), per_b(D_HEADS, D_HEAD, D_HEAD), per_b(8, LANE), per_b(8, LANE)],
        out_shape=[jax.ShapeDtypeStruct((bn, t, W), BF16),
                   jax.ShapeDtypeStruct((bn, 8, 2 * W), F32),
                   jax.ShapeDtypeStruct((bn, D_HEADS, D_HEAD, D_HEAD), F32),
                   jax.ShapeDtypeStruct((bn, 8, LANE), F32),
                   jax.ShapeDtypeStruct((bn, 8, LANE), F32)],
        scratch_shapes=[pltpu.VMEM((8, 2 * W), F32), pltpu.VMEM((tm, W), F32), pltpu.VMEM((tm, W), F32),
                        pltpu.VMEM((D_HEADS, D_HEAD, D_HEAD), F32), pltpu.VMEM((8, LANE), F32),
                        pltpu.VMEM((8, LANE), F32)],
        compiler_params=_params(("parallel", "arbitrary")),
        name="mlstm",
    )(u3, u3, u3, u3, hist8, c0, n0p, m0p, *consts)
    out_d, conv8, c, n8, m8 = outs
    return out_d, conv8[:, 8 - (D_CONV - 1):], c, n8[:, :D_HEADS], m8[:, :D_HEADS, 0]


def _merge_body(x_ref, g_ref, a_ref, b_ref, c_ref, d_ref, wg_ref, wb_ref, wo_ref, o_ref):
    g = g_ref[...].astype(BF16)
    merged = None
    for i, br in enumerate((a_ref, b_ref, c_ref, d_ref)):
        term = _sigmoid(_dot(g, wg_ref[i])) * _dot(br[...], wb_ref[i])
        merged = term if merged is None else merged + term
    o_ref[...] = x_ref[...] + _dot(merged.astype(BF16), wo_ref[...])


def _merge(x2d, u2d, branches, wg, wb, wo, tm):
    n = x2d.shape[0]
    once = lambda a: pl.BlockSpec(a.shape, lambda i: (0,) * a.ndim, pipeline_mode=pl.Buffered(1))
    bblk = pl.BlockSpec((tm, W), lambda i: (i, 0))
    xblk = pl.BlockSpec((tm, D_MODEL), lambda i: (i, 0))
    return pl.pallas_call(
        _merge_body,
        grid=(n // tm,),
        in_specs=[xblk, pl.BlockSpec((tm, GATE_RANK), lambda i: (i, COL_G // GATE_RANK))]
                 + [bblk] * 4 + [once(wg), once(wb), once(wo)],
        out_specs=xblk,
        out_shape=jax.ShapeDtypeStruct((n, D_MODEL), F32),
        compiler_params=_params(("parallel",)),
        name="merge",
    )(x2d, u2d, *branches, wg, wb, wo)


def _ffn_epilogue(acc, gf_ref, final_norm):
    return _rmsnorm(acc, gf_ref[...]) if final_norm else acc


def _ffn_dense_body(x_ref, g_ref, wg_ref, wu_ref, wd_ref, gf_ref, o_ref, h_scr, acc, *, final_norm):
    j = pl.program_id(1)

    @pl.when(j == 0)
    def _():
        x = x_ref[...]
        h_scr[...] = _rmsnorm(x, g_ref[...]).astype(BF16)
        acc[...] = x

    h = h_scr[...]
    a = _dot(h, wg_ref[...])
    act = (a * _sigmoid(a)) * _dot(h, wu_ref[...])
    acc[...] += _dot(act.astype(BF16), wd_ref[...])

    @pl.when(j == pl.num_programs(1) - 1)
    def _():
        o_ref[...] = _ffn_epilogue(acc[...], gf_ref, final_norm)


def _ffn_dense(x2d, g, wgate, wup, wdown, g_final, final_norm, tm, tf=512):
    n = x2d.shape[0]
    xblk = pl.BlockSpec((tm, D_MODEL), lambda i, j: (i, 0))
    vec = pl.BlockSpec((1, D_MODEL), lambda i, j: (0, 0))
    return pl.pallas_call(
        functools.partial(_ffn_dense_body, final_norm=final_norm),
        grid=(n // tm, D_FF // tf),
        in_specs=[xblk, vec,
                  pl.BlockSpec((D_MODEL, tf), lambda i, j: (0, j)),
                  pl.BlockSpec((D_MODEL, tf), lambda i, j: (0, j)),
                  pl.BlockSpec((tf, D_MODEL), lambda i, j: (j, 0)), vec],
        out_specs=xblk,
        out_shape=jax.ShapeDtypeStruct((n, D_MODEL), F32),
        scratch_shapes=[pltpu.VMEM((tm, D_MODEL), BF16), pltpu.VMEM((tm, D_MODEL), F32)],
        compiler_params=_params(("parallel", "arbitrary")),
        name="ffn_dense",
    )(x2d, g.reshape(1, D_MODEL), wgate, wup, wdown, g_final.reshape(1, D_MODEL))


def _ffn_moe_body(x_ref, g_ref, rt_ref, wg_ref, wu_ref, wd_ref, gf_ref, o_ref,
                  h_scr, acc, gates, *, final_norm):
    e = pl.program_id(1)

    @pl.when(e == 0)
    def _():
        x = x_ref[...]
        h = _rmsnorm(x, g_ref[...])
        h_scr[...] = h.astype(BF16)
        acc[...] = x
        h_hi, h_lo = _split2(h)
        r_hi, r_lo = _split2(rt_ref[...])
        logits = _dot(h_hi, r_hi) + _dot(h_lo, r_hi) + _dot(h_hi, r_lo)
        lane = lax.broadcasted_iota(jnp.int32, logits.shape, 1)
        logits = jnp.where(lane < N_EXPERTS, logits, -jnp.inf)
        v1 = jnp.max(logits, axis=-1, keepdims=True)
        i1 = jnp.min(jnp.where(logits == v1, lane, LANE), axis=-1, keepdims=True)
        rest = jnp.where(lane == i1, -jnp.inf, logits)
        v2 = jnp.max(rest, axis=-1, keepdims=True)
        i2 = jnp.min(jnp.where(rest == v2, lane, LANE), axis=-1, keepdims=True)
        e2 = jnp.exp(v2 - v1)
        den = 1.0 + e2
        gates[...] = jnp.where(lane == i1, 1.0 / den, 0.0) + jnp.where(lane == i2, e2 / den, 0.0)

    h = h_scr[...]
    lane = lax.broadcasted_iota(jnp.int32, gates.shape, 1)
    gate = jnp.sum(jnp.where(lane == e, gates[...], 0.0), axis=-1, keepdims=True)
    a = _dot(h, wg_ref[...])
    act = (a * _sigmoid(a)) * _dot(h, wu_ref[...])
    acc[...] += gate * _dot(act.astype(BF16), wd_ref[...])

    @pl.when(e == pl.num_programs(1) - 1)
    def _():
        o_ref[...] = _ffn_epilogue(acc[...], gf_ref, final_norm)


def _ffn_moe(x2d, g, router, wgate, wup, wdown, g_final, final_norm, tm):
    n = x2d.shape[0]
    xblk = pl.BlockSpec((tm, D_MODEL), lambda i, j: (i, 0))
    vec = pl.BlockSpec((1, D_MODEL), lambda i, j: (0, 0))
    return pl.pallas_call(
        functools.partial(_ffn_moe_body, final_norm=final_norm),
        grid=(n // tm, N_EXPERTS),
        in_specs=[xblk, vec,
                  pl.BlockSpec((D_MODEL, LANE), lambda i, j: (0, 0)),
                  pl.BlockSpec((None, D_MODEL, EXPERT_PAD), lambda i, j: (j, 0, 0)),
                  pl.BlockSpec((None, D_MODEL, EXPERT_PAD), lambda i, j: (j, 0, 0)),
                  pl.BlockSpec((None, EXPERT_PAD, D_MODEL), lambda i, j: (j, 0, 0)), vec],
        out_specs=xblk,
        out_shape=jax.ShapeDtypeStruct((n, D_MODEL), F32),
        scratch_shapes=[pltpu.VMEM((tm, D_MODEL), BF16), pltpu.VMEM((tm, D_MODEL), F32),
                        pltpu.VMEM((tm, LANE), F32)],
        compiler_params=_params(("parallel", "arbitrary")),
        name="ffn_moe",
    )(x2d, g.reshape(1, D_MODEL), router, wgate, wup, wdown, g_final.reshape(1, D_MODEL))


def _pack_w_in(w):
    a, b, c, d = w[:, :OFF_B], w[:, OFF_B:OFF_C], w[:, OFF_C:OFF_D], w[:, OFF_D:OFF_G]
    g = w[:, OFF_G:]
    zeros = lambda k: jnp.zeros((D_MODEL, k), w.dtype)
    lora = a[:, 3 * W:]
    gates = d[:, 4 * W:]
    cols = [d[:, :4 * W], a[:, :3 * W], b, c,
            lora, zeros(LORA_PAD - lora.shape[1]), g,
            gates, zeros(LANE - gates.shape[1]), zeros(WP - COL_DG - LANE)]
    return jnp.concatenate(cols, axis=1).astype(BF16)


def _layer_params(l, wts):
    (norm_mix, norm_ffn, norm_final, w_in, rwkv_mu, rwkv_w0, rwkv_w2, rwkv_a0, rwkv_a2,
     rwkv_g2, rwkv_k_k, rwkv_k_a, rwkv_r_k, rwkv_ln_w, rwkv_ln_b, pool_w, pool_scale,
     mlstm_conv_w, mlstm_conv_b, mlstm_i_bias, mlstm_f_bias, mlstm_norm_w,
     w_branch, w_merge_gate, w_out, ffn_w_gate, ffn_w_up, ffn_w_down,
     moe_router, moe_w_gate, moe_w_up, moe_w_down) = wts
    n_l = sum(A_LORA)
    row = lambda v: v.reshape(1, -1)

    def lora_pad(wm, start):
        return jnp.pad(wm, ((start, LORA_PAD - start - wm.shape[0]), (0, 0))).astype(BF16)

    hid = jnp.arange(W) // A_HEAD
    p = {
        "norm_mix": norm_mix[l], "norm_ffn": norm_ffn[l],
        "wp": _pack_w_in(w_in[l]),
        "rwkv": {
            "mu_rkv": rwkv_mu[l, :3 * W].reshape(3, W),
            "mu_l": jnp.pad(rwkv_mu[l, 3 * W:], (0, LORA_PAD - n_l)).reshape(1, LORA_PAD),
            "w0": row(rwkv_w0[l]), "a0": row(rwkv_a0[l]),
            "k_k": row(rwkv_k_k[l]), "k_a": row(rwkv_k_a[l]),
            "w2p": lora_pad(rwkv_w2[l], 0),
            "a2p": lora_pad(rwkv_a2[l], A_LORA[0]),
            "g2p": lora_pad(rwkv_g2[l], A_LORA[0] + A_LORA[1]),
            "head_ones": (hid[:, None] == hid[None, :]).astype(BF16),
            "r_k": row(rwkv_r_k[l]), "ln_w": row(rwkv_ln_w[l]), "ln_b": row(rwkv_ln_b[l]),
        },
        "pool_w": pool_w[l], "pool_scale": pool_scale[l],
        "mlstm": {
            "conv_w": mlstm_conv_w[l], "conv_b": row(mlstm_conv_b[l]),
            "gate_bias": jnp.pad(jnp.concatenate([mlstm_i_bias[l], mlstm_f_bias[l]]),
                                 (0, LANE - 2 * D_HEADS)).reshape(1, LANE),
            "norm_w": row(mlstm_norm_w[l]),
        },
        "wg": w_merge_gate[l].astype(BF16), "wb": w_branch[l].astype(BF16), "wo": w_out[l].astype(BF16),
    }
    if l % 2 == 0:
        p["ffn"] = (ffn_w_gate[l // 2].astype(BF16), ffn_w_up[l // 2].astype(BF16),
                    ffn_w_down[l // 2].astype(BF16))
    else:
        pe = EXPERT_PAD - D_FF_EXPERT
        p["moe"] = (jnp.pad(moe_router[l // 2], ((0, 0), (0, LANE - N_EXPERTS))),
                    jnp.pad(moe_w_gate[l // 2], ((0, 0), (0, 0), (0, pe))).astype(BF16),
                    jnp.pad(moe_w_up[l // 2], ((0, 0), (0, 0), (0, pe))).astype(BF16),
                    jnp.pad(moe_w_down[l // 2], ((0, 0), (0, pe), (0, 0))).astype(BF16))
    return p


def _heads_major(x3, heads):
    bn, t, c = x3.shape
    return x3.reshape(bn, t, heads, c // heads).transpose(0, 2, 1, 3)


def _heads_minor(x4):
    bn, hh, t, d = x4.shape
    return x4.transpose(0, 2, 1, 3).reshape(bn, t, hh * d)


def _slot(u3, name):
    s = SLOT[name] * W
    return u3[:, :, s:s + W]


def _run_trunk(x, start_pos, cache_k, cache_v, wkv0, shift0, pool0, conv0, c0, n0, m0,
               layers, norm_final):
    bn, t, _ = x.shape
    n = bn * t
    tm_tok = min(256, t)
    tm_row = min(512, n)
    x2 = x.reshape(n, D_MODEL)
    ks, vs, wkvs, shifts, pools, convs, cs, ns, ms = ([] for _ in range(9))
    for l, p in enumerate(layers):
        u2 = _inproj(x2, p["norm_mix"], p["wp"])
        u3 = u2.reshape(bn, t, WP)

        r, k2, v, kk, b, lw, g = _rwkv_prep(u3, shift0[l], p["rwkv"], tm_tok)
        hm = lambda a: _heads_major(a, A_HEADS)
        y, wkv = _rwkv_chunk(hm(r), hm(k2), hm(v), hm(kk), hm(b), hm(lw), wkv0[l], tm_tok)
        flat = lambda a: a.reshape(n, W)
        out_a = _rwkv_post(flat(_heads_minor(y)), flat(r), flat(k2), flat(v), flat(g), p["rwkv"], tm_row)

        out_b = _pool(u3, pool0[l], p["pool_w"], p["pool_scale"], start_pos, tm_tok).reshape(n, W)

        q_c, k_c, v_c = _slot(u3, "cq"), _slot(u3, "ck"), _slot(u3, "cv")
        hb = lambda a: _heads_major(a.astype(BF16), C_HEADS)
        if cache_k is None:
            out_c = _sb_attention(hb(q_c), hb(k_c), hb(v_c), 0, min(SB_TQ, t))
        else:
            past = cache_k.shape[2]
            k_all = jnp.concatenate([cache_k[l].reshape(bn, past, W), k_c], axis=1)
            v_all = jnp.concatenate([cache_v[l].reshape(bn, past, W), v_c], axis=1)
            out_c = _sb_attention(hb(q_c), hb(k_all), hb(v_all), past, t)
        out_c = _heads_minor(out_c).reshape(n, W)

        out_d, conv_new, c_new, n_new, m_new = _mlstm(u3, conv0[l], c0[l], n0[l], m0[l], p["mlstm"], tm_tok)

        x2 = _merge(x2, u2, (out_a, out_b, out_c, out_d.reshape(n, W)), p["wg"], p["wb"], p["wo"],
                    min(256, n))
        last = l == len(layers) - 1
        if "ffn" in p:
            x2 = _ffn_dense(x2, p["norm_ffn"], *p["ffn"], norm_final, last, tm_row)
        else:
            x2 = _ffn_moe(x2, p["norm_ffn"], *p["moe"], norm_final, last, tm_row)

        ks.append(k_c.reshape(bn, t, C_HEADS, C_HEAD))
        vs.append(v_c.reshape(bn, t, C_HEADS, C_HEAD))
        wkvs.append(wkv)
        u_last = u3[:, -1]
        shifts.append(jnp.concatenate(
            [u_last[:, SLOT["ar"] * W:SLOT["ar"] * W + 3 * W], u_last[:, COL_LORA:COL_LORA + sum(A_LORA)]], axis=1))
        pools.append(jnp.concatenate([pool0[l], _slot(u3, "b")[:, -B_HIST:]], axis=1)[:, -B_HIST:])
        convs.append(conv_new)
        cs.append(c_new)
        ns.append(n_new)
        ms.append(m_new)
    st = lambda lst: jnp.stack(lst, axis=0)
    return (x2.reshape(bn, t, D_MODEL),
            (st(ks), st(vs), st(wkvs), st(shifts), st(pools), st(convs), st(cs), st(ns), st(ms)))


def kernel(x_prompt, x_sample, cache_sb_k, cache_sb_v, state_rwkv_wkv, state_rwkv_shift, state_pool,
           state_mlstm_conv, state_mlstm_c, state_mlstm_n, state_mlstm_m, norm_mix, norm_ffn, norm_final,
           w_in, rwkv_mu, rwkv_w0, rwkv_w2, rwkv_a0, rwkv_a2, rwkv_g2, rwkv_k_k, rwkv_k_a, rwkv_r_k,
           rwkv_ln_w, rwkv_ln_b, pool_w, pool_scale, mlstm_conv_w, mlstm_conv_b, mlstm_i_bias,
           mlstm_f_bias, mlstm_norm_w, w_branch, w_merge_gate, w_out, ffn_w_gate, ffn_w_up, ffn_w_down,
           moe_router, moe_w_gate, moe_w_up, moe_w_down):
    wts = (norm_mix, norm_ffn, norm_final, w_in, rwkv_mu, rwkv_w0, rwkv_w2, rwkv_a0, rwkv_a2,
           rwkv_g2, rwkv_k_k, rwkv_k_a, rwkv_r_k, rwkv_ln_w, rwkv_ln_b, pool_w, pool_scale,
           mlstm_conv_w, mlstm_conv_b, mlstm_i_bias, mlstm_f_bias, mlstm_norm_w,
           w_branch, w_merge_gate, w_out, ffn_w_gate, ffn_w_up, ffn_w_down,
           moe_router, moe_w_gate, moe_w_up, moe_w_down)
    depth = w_in.shape[0]
    layers = [_layer_params(l, wts) for l in range(depth)]
    bp = x_prompt.shape[0]
    z = lambda *s: jnp.zeros((depth, bp) + s, F32)
    y_p, st_p = _run_trunk(
        x_prompt, 0, None, None, z(A_HEADS, A_HEAD, A_HEAD), z(A_IN), z(B_HIST, W),
        z(D_CONV - 1, 2 * W), z(D_HEADS, D_HEAD, D_HEAD), z(D_HEADS, D_HEAD), z(D_HEADS),
        layers, norm_final)
    y_s, st_s = _run_trunk(
        x_sample, cache_sb_k.shape[2], cache_sb_k, cache_sb_v, state_rwkv_wkv, state_rwkv_shift,
        state_pool, state_mlstm_conv, state_mlstm_c, state_mlstm_n, state_mlstm_m,
        layers, norm_final)
    return (y_p, y_s) + tuple(st_p) + tuple(st_s)
```

```python
import functools

import jax
import jax.numpy as jnp
from jax import lax
from jax.experimental import pallas as pl
from jax.experimental.pallas import tpu as pltpu

F32 = jnp.float32
BF16 = jnp.bfloat16

D_MODEL = 2048
DEPTH = 2
W = 512
A_HEAD, A_HEADS = 64, 8
A_LORA = (32, 32, 96)
A_IN = 3 * W + sum(A_LORA)
A_GN_EPS = 64e-5
B_WINDOWS = (2, 4, 8, 16)
B_HIST = 15
C_HEAD, C_HEADS = 64, 8
D_HEADS, D_HEAD = 4, 128
D_CONV = 4
D_NORM_EPS = 1e-6
GATE_RANK = 256
OFF_B = A_IN
OFF_C = OFF_B + W
OFF_D = OFF_C + 3 * W
OFF_G = OFF_D + 4 * W + 2 * D_HEADS
IN_WIDTH = OFF_G + GATE_RANK
D_FF = 5632
N_EXPERTS = 8
D_FF_EXPERT = D_FF // N_EXPERTS
NORM_EPS = 1e-6

LANE = 128
V7X_VMEM_BYTES = 64 * 1024 * 1024
VMEM_LIMIT = V7X_VMEM_BYTES - 8 * 1024 * 1024

SLOT = {name: i for i, name in enumerate(
    ("dq", "dk", "dv", "do", "ar", "ak", "av", "b", "cq", "ck", "cv"))}
COL_LORA = 11 * W
COL_G = COL_LORA + 256
COL_DG = COL_G + 256
IN_TN = 1280
WP = 6400
LORA_PAD = 256
EXPERT_PAD = 768

RWKV_CHUNK = 64
MLSTM_CHUNK = 64
SB_BLOCK = 128
SB_TQ = 512
SB_DEAD = -120.0


def _params(sem, vmem=VMEM_LIMIT):
    return pltpu.CompilerParams(dimension_semantics=sem, vmem_limit_bytes=vmem)


def _split2(x):
    hi = x.astype(BF16)
    lo = (x - hi.astype(F32)).astype(BF16)
    return hi, lo


def _split3(x):
    hi = x.astype(BF16)
    r1 = x - hi.astype(F32)
    mid = r1.astype(BF16)
    lo = (r1 - mid.astype(F32)).astype(BF16)
    return hi, mid, lo


def _dot(a, b):
    return jnp.dot(a, b, preferred_element_type=F32)


def _dot_nt(a, b):
    return lax.dot_general(a, b, (((1,), (1,)), ((), ())), preferred_element_type=F32)


def _bdot(a, b):
    return lax.dot_general(a, b, (((2,), (1,)), ((0,), (0,))), preferred_element_type=F32)


def _bdot_nt(a, b):
    return lax.dot_general(a, b, (((2,), (2,)), ((0,), (0,))), preferred_element_type=F32)


def _bdot_tn(a, b):
    return lax.dot_general(a, b, (((1,), (1,)), ((0,), (0,))), preferred_element_type=F32)


def _dot_lhs01(m01, x):
    hi, mid, lo = _split3(x)
    return _dot(m01, hi) + _dot(m01, mid) + _dot(m01, lo)


def _sigmoid(x):
    return 1.0 / (1.0 + jnp.exp(-x))


def _log_sigmoid(x):
    return jnp.minimum(x, 0.0) - jnp.log1p(jnp.exp(-jnp.abs(x)))


def _rmsnorm(x, g):
    ms = jnp.mean(x * x, axis=-1, keepdims=True)
    return x * lax.rsqrt(ms + NORM_EPS) * g


def _inproj_body(x_ref, g_ref, w_ref, o_ref, h_scr):
    @pl.when(pl.program_id(1) == 0)
    def _():
        h_scr[...] = _rmsnorm(x_ref[...], g_ref[...]).astype(BF16)

    o_ref[...] = _dot(h_scr[...], w_ref[...])


def _inproj(x2d, g, wp):
    n = x2d.shape[0]
    tm = min(512, n)
    return pl.pallas_call(
        _inproj_body,
        grid=(n // tm, WP // IN_TN),
        in_specs=[pl.BlockSpec((tm, D_MODEL), lambda i, j: (i, 0)),
                  pl.BlockSpec((1, D_MODEL), lambda i, j: (0, 0)),
                  pl.BlockSpec((D_MODEL, IN_TN), lambda i, j: (0, j))],
        out_specs=pl.BlockSpec((tm, IN_TN), lambda i, j: (i, j)),
        out_shape=jax.ShapeDtypeStruct((n, WP), F32),
        scratch_shapes=[pltpu.VMEM((tm, D_MODEL), BF16)],
        compiler_params=_params(("parallel", "arbitrary")),
        name="inproj",
    )(x2d, g.reshape(1, D_MODEL), wp)


def _shift_rows(u, carry_row):
    rolled = pltpu.roll(u, 1, axis=0)
    row = lax.broadcasted_iota(jnp.int32, u.shape, 0)
    return jnp.where(row == 0, carry_row, rolled)


def _rwkv_prep(ur_ref, uk_ref, uv_ref, ul_ref, sr_ref, sk_ref, sv_ref, sl_ref,
               mu_ref, mul_ref, w0_ref, a0_ref, kk_ref, ka_ref, w2_ref, a2_ref, g2_ref, ones_ref,
               cr, ck, cv, cl):
    tm = ur_ref.shape[0]

    @pl.when(pl.program_id(1) == 0)
    def _():
        cr[0:1, :] = sr_ref[...]
        ck[0:1, :] = sk_ref[...]
        cv[0:1, :] = sv_ref[...]
        cl[0:1, :] = sl_ref[...]

    def mix(u_ref, c_ref, mu):
        u = u_ref[...]
        prev = _shift_rows(u, c_ref[0:1, :])
        c_ref[0:1, :] = u[tm - 1:tm, :]
        return u + (prev - u) * mu

    r = mix(ur_ref, cr, mu_ref[0:1, :])
    k = mix(uk_ref, ck, mu_ref[1:2, :])
    v = mix(uv_ref, cv, mu_ref[2:3, :])
    xl = mix(ul_ref, cl, mul_ref[...])

    w_lora = _dot(jnp.tanh(xl).astype(BF16), w2_ref[...])
    a_lora = _dot(xl.astype(BF16), a2_ref[...])
    g = _dot(_sigmoid(xl).astype(BF16), g2_ref[...])

    y = -(w0_ref[...] + w_lora)
    w_log = -(jnp.maximum(y, 0.0) + jnp.log1p(jnp.exp(-jnp.abs(y)))) - 0.5
    log_decay = -jnp.exp(w_log)
    a = _sigmoid(a0_ref[...] + a_lora)

    kk = k * kk_ref[...]
    hi, lo = _split2(kk * kk)
    ss = _dot(hi, ones_ref[...]) + _dot(lo, ones_ref[...])
    kk = kk / jnp.maximum(jnp.sqrt(ss), 1e-12)
    k2 = k * (1.0 + (a - 1.0) * ka_ref[...])

    return r, k2, v, kk, kk * a, log_decay, g


def _rwkv_chunks(r_ref, k_ref, v_ref, kk_ref, b_ref, lw_ref, y_ref, s_scr, n_chunks):
    c_len = RWKV_CHUNK
    hh = A_HEADS
    row = lax.broadcasted_iota(jnp.int32, (hh, c_len, c_len), 1)
    col = lax.broadcasted_iota(jnp.int32, (hh, c_len, c_len), 2)
    tri_incl = (row >= col).astype(BF16)
    lower = row >= col
    strict = row > col
    eye = (row == col).astype(F32)

    def chunk(c, carry):
        off = pl.multiple_of(c * c_len, c_len)
        sl = pl.ds(off, c_len)
        r = r_ref[:, sl, :]
        k = k_ref[:, sl, :]
        v = v_ref[:, sl, :]
        kk = kk_ref[:, sl, :]
        b = b_ref[:, sl, :]
        logw = lw_ref[:, sl, :]
        w_hi, w_mid, w_lo = _split3(logw)
        lw = _bdot(tri_incl, w_hi) + _bdot(tri_incl, w_mid) + _bdot(tri_incl, w_lo)
        lw_prev = lw - logw
        lw_last = lw[:, c_len - 1:c_len, :]
        e_neg = jnp.exp(-lw)
        e_end = jnp.exp(lw_last - lw)
        kkm = (kk * jnp.exp(lw_prev)).astype(BF16)
        rm = r * jnp.exp(lw)
        kp = (k * e_neg).astype(BF16)
        bp = (b * e_neg).astype(BF16)
        kpp = (k * e_end).astype(BF16)
        bpp = (b * e_end).astype(BF16)
        vb = v.astype(BF16)
        rmb = rm.astype(BF16)

        a_vk = jnp.where(strict, _bdot_nt(kkm, kp), 0.0)
        a_pb = jnp.where(strict, _bdot_nt(kkm, bp), 0.0)
        rk = jnp.where(lower, _bdot_nt(rmb, kp), 0.0)
        rb = jnp.where(lower, _bdot_nt(rmb, bp), 0.0)

        n_pow = -a_pb
        t_inv = eye + n_pow
        for _ in range(5):
            nb = n_pow.astype(BF16)
            n_pow = _bdot(nb, nb)
            t_inv = t_inv + _bdot(t_inv.astype(BF16), n_pow.astype(BF16))
        tb = t_inv.astype(BF16)

        kktb = _bdot(tb, kkm).astype(BF16)
        pvb = _bdot(tb, _bdot(a_vk.astype(BF16), vb).astype(BF16)).astype(BF16)
        rbb = rb.astype(BF16)
        m_mat = eye * jnp.exp(lw_last) - _bdot_tn(kktb, bpp)
        n_mat = _bdot_tn(vb, kpp) - _bdot_tn(pvb, bpp)
        q_mat = rm - _bdot(rbb, kktb)
        y0 = _bdot(rk.astype(BF16), vb) - _bdot(rbb, pvb)

        s_hi, s_lo = _split2(s_scr[...])
        m_hi, m_lo = _split2(m_mat)
        y_ref[:, sl, :] = _bdot_nt(q_mat.astype(BF16), s_hi) + y0
        s_scr[...] = _bdot(s_hi, m_hi) + _bdot(s_lo, m_hi) + _bdot(s_hi, m_lo) + n_mat
        return carry

    lax.fori_loop(0, n_chunks, chunk, 0)


def _rwkv_post(y, r, k, v, g, r_k, ln_w, ln_b, ones):
    def head_sum(x):
        hi, lo = _split2(x)
        return _dot(hi, ones) + _dot(lo, ones)

    mu = head_sum(y) * (1.0 / A_HEAD)
    yc = y - mu
    var = head_sum(yc * yc) * (1.0 / A_HEAD)
    yn = yc * lax.rsqrt(var + A_GN_EPS) * ln_w + ln_b
    bonus = head_sum(r * k * r_k) * v
    return ((yn + bonus) * g).astype(BF16)


def _rwkv_body(ur_ref, uk_ref, uv_ref, ul_ref, sr_ref, sk_ref, sv_ref, sl_ref,
               mu_ref, mul_ref, w0_ref, a0_ref, kk_ref, ka_ref, w2_ref, a2_ref, g2_ref, ones_ref,
               rk_ref, lnw_ref, lnb_ref, s0_ref,
               o_ref, sout_ref,
               cr, ck, cv, cl, s_scr, r_hm, k_hm, v_hm, kk_hm, b_hm, lw_hm, y_hm, *, n_chunks):
    @pl.when(pl.program_id(1) == 0)
    def _():
        s_scr[...] = s0_ref[...]

    r, k, v, kk, b, lw, g = _rwkv_prep(
        ur_ref, uk_ref, uv_ref, ul_ref, sr_ref, sk_ref, sv_ref, sl_ref, mu_ref, mul_ref, w0_ref, a0_ref,
        kk_ref, ka_ref, w2_ref, a2_ref, g2_ref, ones_ref, cr, ck, cv, cl)
    for x, ref in ((r, r_hm), (k, k_hm), (v, v_hm), (kk, kk_hm), (b, b_hm), (lw, lw_hm)):
        for h in range(A_HEADS):
            ref[h] = x[:, h * A_HEAD:(h + 1) * A_HEAD]
    _rwkv_chunks(r_hm, k_hm, v_hm, kk_hm, b_hm, lw_hm, y_hm, s_scr, n_chunks)
    sout_ref[...] = s_scr[...]
    y = jnp.concatenate([y_hm[h] for h in range(A_HEADS)], axis=1)
    o_ref[...] = _rwkv_post(y, r, k, v, g, rk_ref[...], lnw_ref[...], lnb_ref[...], ones_ref[...])


def _rwkv(u3, shift_prev, s0, p, tm):
    bn, t, _ = u3.shape
    sr = shift_prev[:, None, 0:W]
    sk = shift_prev[:, None, W:2 * W]
    sv = shift_prev[:, None, 2 * W:3 * W]
    sl = jnp.pad(shift_prev[:, None, 3 * W:], ((0, 0), (0, 0), (0, LORA_PAD - sum(A_LORA))))
    row = lambda c: pl.BlockSpec((None, 1, c), lambda b, i: (b, 0, 0))
    full = lambda a: pl.BlockSpec(a.shape, lambda b, i: (0,) * a.ndim)
    ublk = lambda slot: pl.BlockSpec((None, tm, W), lambda b, i, s=slot: (b, i, s))
    consts = (p["mu_rkv"], p["mu_l"], p["w0"], p["a0"], p["k_k"], p["k_a"],
              p["w2p"], p["a2p"], p["g2p"], p["head_ones"], p["r_k"], p["ln_w"], p["ln_b"])
    sblk = pl.BlockSpec((None, A_HEADS, A_HEAD, A_HEAD), lambda b, i: (b, 0, 0, 0))
    head_major = pltpu.VMEM((A_HEADS, tm, A_HEAD), F32)
    return pl.pallas_call(
        functools.partial(_rwkv_body, n_chunks=tm // RWKV_CHUNK),
        grid=(bn, t // tm),
        in_specs=[ublk(SLOT["ar"]), ublk(SLOT["ak"]), ublk(SLOT["av"]),
                  pl.BlockSpec((None, tm, LORA_PAD), lambda b, i: (b, i, COL_LORA // LORA_PAD)),
                  row(W), row(W), row(W), row(LORA_PAD)] + [full(c) for c in consts] + [sblk],
        out_specs=[pl.BlockSpec((None, tm, W), lambda b, i: (b, i, 0)), sblk],
        out_shape=[jax.ShapeDtypeStruct((bn, t, W), BF16),
                   jax.ShapeDtypeStruct((bn, A_HEADS, A_HEAD, A_HEAD), F32)],
        scratch_shapes=[pltpu.VMEM((8, W), F32)] * 3
                       + [pltpu.VMEM((8, LORA_PAD), F32), pltpu.VMEM((A_HEADS, A_HEAD, A_HEAD), F32)]
                       + [head_major] * 7,
        compiler_params=_params(("parallel", "arbitrary")),
        name="rwkv",
    )(u3, u3, u3, u3, sr, sk, sv, sl, *consts, s0)


def _pool_body(u_ref, hist_ref, w_ref, scale_ref, o_ref, carry, *, start_pos):
    tm = u_ref.shape[0]
    i = pl.program_id(1)

    @pl.when(i == 0)
    def _():
        carry[...] = hist_ref[...]

    x = u_ref[...]
    ext = jnp.concatenate([carry[...], x], axis=0)
    carry[...] = x[tm - 16:tm, :]
    s = ext
    sums = []
    for sh in (1, 2, 4, 8):
        s = s + pltpu.roll(s, sh, axis=0)
        sums.append(s[16:16 + tm, :])
    pos = start_pos + i * tm + lax.broadcasted_iota(jnp.int32, (tm, 1), 0)
    for gi, wlen in enumerate(B_WINDOWS):
        cs = slice(gi * LANE, (gi + 1) * LANE)
        cnt = jnp.minimum(wlen, pos + 1).astype(F32)
        d = sums[gi][:, cs] / cnt - x[:, cs]
        o_ref[:, cs] = (_dot(d.astype(BF16), w_ref[gi]) * scale_ref[:, cs]).astype(BF16)


def _pool(u3, hist, pool_w, pool_scale, start_pos, tm):
    bn, t, _ = u3.shape
    hist16 = jnp.pad(hist, ((0, 0), (1, 0), (0, 0)))
    return pl.pallas_call(
        functools.partial(_pool_body, start_pos=start_pos),
        grid=(bn, t // tm),
        in_specs=[pl.BlockSpec((None, tm, W), lambda b, i: (b, i, SLOT["b"])),
                  pl.BlockSpec((None, 16, W), lambda b, i: (b, 0, 0)),
                  pl.BlockSpec((4, LANE, LANE), lambda b, i: (0, 0, 0)),
                  pl.BlockSpec((1, W), lambda b, i: (0, 0))],
        out_specs=pl.BlockSpec((None, tm, W), lambda b, i: (b, i, 0)),
        out_shape=jax.ShapeDtypeStruct((bn, t, W), BF16),
        scratch_shapes=[pltpu.VMEM((16, W), F32)],
        compiler_params=_params(("parallel", "arbitrary")),
        name="pool",
    )(u3, hist16, pool_w.astype(BF16), pool_scale.reshape(1, W))


def _sb_body(q_ref, k_ref, v_ref, o_ref, acc_scr, o_scr, *, tq, q_off, n_masked):
    qi = pl.program_id(2)
    q = q_ref[...] * (C_HEAD ** -0.5)
    q_start = q_off + qi * tq
    first_masked = q_start // SB_BLOCK

    jj = lax.broadcasted_iota(jnp.int32, (2 * SB_BLOCK, 2 * SB_BLOCK), 0) % SB_BLOCK
    ss = lax.broadcasted_iota(jnp.int32, (2 * SB_BLOCK, 2 * SB_BLOCK), 1)
    cs = jnp.where(jnp.logical_or(ss >= SB_BLOCK, jj > ss), -1.0, 0.0).astype(BF16)
    aligned = q_off % SB_BLOCK == 0 and tq % SB_BLOCK == 0

    def block(kb, r0, masked):
        rows = slice(r0, tq)
        ks = pl.multiple_of(kb * SB_BLOCK, SB_BLOCK)
        kblk = k_ref[pl.ds(ks, SB_BLOCK), :]
        vblk = v_ref[pl.ds(ks, SB_BLOCK), :]
        z = _dot_nt(q[rows], kblk)
        zp = jnp.maximum(z, 0.0)
        zn = jnp.minimum(z, 0.0)
        l1p = jnp.log(1.0 + jnp.exp(zn - zp))
        nlf = zp + l1p
        s = zn - l1p
        if masked:
            q_pos = q_start + r0 + lax.broadcasted_iota(jnp.int32, z.shape, 0)
            mask = (ks + lax.broadcasted_iota(jnp.int32, z.shape, 1)) < q_pos
            nlf = jnp.where(mask, nlf, 0.0)
        hi, lo = _split2(nlf)
        lt = _dot(jnp.concatenate([hi, lo], axis=1), cs)
        att = jnp.exp(s + lt[:, :SB_BLOCK] + acc_scr[rows, :])
        if masked:
            att = jnp.where(mask, att, 0.0)
        o_scr[rows, :] += _dot(att.astype(BF16), vblk)
        acc_scr[rows, :] += lt[:, SB_BLOCK:]

    acc_scr[...] = jnp.zeros_like(acc_scr)
    o_scr[...] = jnp.zeros_like(o_scr)
    for m in range(n_masked - 1, -1, -1):
        block(first_masked + m, m * SB_BLOCK if aligned else 0, True)

    def live(i):
        return jnp.logical_and(i < first_masked, jnp.max(acc_scr[...]) > SB_DEAD)

    def step(i):
        block(first_masked - 1 - i, 0, False)
        return i + 1

    lax.while_loop(live, step, jnp.int32(0))
    o_ref[...] = o_scr[...].astype(BF16)


def _to_heads_body(*refs):
    n = len(refs) // 2
    for x_ref, o_ref in zip(refs[:n], refs[n:]):
        x = x_ref[...]
        for h in range(C_HEADS):
            o_ref[h] = x[:, h * C_HEAD:(h + 1) * C_HEAD].astype(BF16)


def _to_heads(x3, slots, tm):
    bn, t, _ = x3.shape
    return pl.pallas_call(
        _to_heads_body,
        grid=(bn, t // tm),
        in_specs=[pl.BlockSpec((None, tm, W), lambda b, i, s=s: (b, i, s)) for s in slots],
        out_specs=[pl.BlockSpec((None, C_HEADS, tm, C_HEAD), lambda b, i: (b, 0, i, 0))] * len(slots),
        out_shape=[jax.ShapeDtypeStruct((bn, C_HEADS, t, C_HEAD), BF16)] * len(slots),
        compiler_params=_params(("parallel", "parallel")),
        name="to_heads",
    )(*([x3] * len(slots)))


def _sb_attention(q, k, v, q_off, tq):
    bn, hh, t, d = q.shape
    sk = k.shape[2]
    sk_pad = -(-sk // SB_BLOCK) * SB_BLOCK
    if sk_pad != sk:
        pad = ((0, 0), (0, 0), (0, sk_pad - sk), (0, 0))
        k, v = jnp.pad(k, pad), jnp.pad(v, pad)
    assert tq % SB_BLOCK == 0 or t == tq
    n_masked = -(-((q_off % SB_BLOCK) + tq) // SB_BLOCK)
    kv = pl.BlockSpec((None, None, sk_pad, d), lambda b, h, i: (b, h, 0, 0))
    qo = pl.BlockSpec((None, None, tq, d), lambda b, h, i: (b, h, i, 0))
    return pl.pallas_call(
        functools.partial(_sb_body, tq=tq, q_off=q_off, n_masked=n_masked),
        grid=(bn, hh, t // tq),
        in_specs=[qo, kv, kv],
        out_specs=qo,
        out_shape=jax.ShapeDtypeStruct((bn, hh, t, d), BF16),
        scratch_shapes=[pltpu.VMEM((tq, SB_BLOCK), F32), pltpu.VMEM((tq, d), F32)],
        compiler_params=_params(("parallel", "parallel", "arbitrary")),
        name="sb_attention",
    )(q, k, v)


def _mlstm_body(qk_ref, v_ref, o_ref, g_ref, hist_ref, c0_ref, n0_ref, m0_ref,
                cw_ref, cb_ref, gb_ref, nw_ref,
                out_ref, conv_out, c_out, n_out, m_out,
                carry, q_scr, k_scr, c_scr, n_scr, m_scr, *, n_chunks):
    tm = qk_ref.shape[0]
    cl = MLSTM_CHUNK

    @pl.when(pl.program_id(1) == 0)
    def _():
        carry[...] = hist_ref[...]
        c_scr[...] = c0_ref[...]
        n_scr[...] = n0_ref[...]
        m_scr[...] = m0_ref[...]

    x = qk_ref[...]
    ext = jnp.concatenate([carry[...], x], axis=0)
    carry[...] = x[tm - 8:tm, :]
    conv = cb_ref[...] + x * cw_ref[D_CONV - 1:D_CONV, :]
    for sh in range(1, D_CONV):
        conv = conv + pltpu.roll(ext, sh, axis=0)[8:8 + tm, :] * cw_ref[D_CONV - 1 - sh:D_CONV - sh, :]
    conv = conv * _sigmoid(conv)
    q_scr[...] = conv[:, :W]
    k_scr[...] = conv[:, W:] * (D_HEAD ** -0.5)

    row = lax.broadcasted_iota(jnp.int32, (cl, cl), 0)
    col = lax.broadcasted_iota(jnp.int32, (cl, cl), 1)
    causal = row >= col
    tri_incl = causal.astype(BF16)

    def chunk(c, carry_):
        off = pl.multiple_of(c * cl, cl)
        sl = pl.ds(off, cl)
        gpre = g_ref[sl, :] + gb_ref[...]
        lfa = _log_sigmoid(gpre)
        bcol = _dot_lhs01(tri_incl, lfa)
        g_t = gpre.T
        b_t = bcol.T
        for h in range(D_HEADS):
            hs = slice(h * D_HEAD, (h + 1) * D_HEAD)
            q = q_scr[sl, hs]
            k = k_scr[sl, hs]
            v = v_ref[sl, hs]
            ig_col = gpre[:, h:h + 1]
            b_col = bcol[:, D_HEADS + h:D_HEADS + h + 1]
            ig_row = g_t[h:h + 1, :]
            b_row = b_t[D_HEADS + h:D_HEADS + h + 1, :]
            m_prev = m_scr[h:h + 1, 0:1]
            log_d = jnp.where(causal, b_col - b_row + ig_row, -jnp.inf)
            m_inter = b_col + m_prev
            m_t = jnp.maximum(m_inter, jnp.max(log_d, axis=-1, keepdims=True))
            dmat = jnp.exp(log_d - m_t)
            inter = jnp.exp(m_inter - m_t)
            qb = q.astype(BF16)
            vb = v.astype(BF16)
            w_qk = _dot_nt(qb, k.astype(BF16)) * dmat
            num = _dot(w_qk.astype(BF16), vb) + inter * _dot(qb, c_scr[h].astype(BF16))
            qn = jnp.sum(q * n_scr[h:h + 1, :], axis=-1, keepdims=True)
            den = jnp.sum(w_qk, axis=-1, keepdims=True) + inter * qn
            den = jnp.maximum(jnp.abs(den), jnp.exp(-m_t))
            hv = num / den
            m_new = m_t[cl - 1:cl, :]
            b_last = b_col[cl - 1:cl, :]
            decay = jnp.exp(b_last + m_prev - m_new)
            w_s = jnp.exp(b_last - b_col + ig_col - m_new)
            kw = k * w_s
            c_scr[h] = decay * c_scr[h] + _dot(kw.T.astype(BF16), vb)
            n_scr[h:h + 1, :] = decay * n_scr[h:h + 1, :] + jnp.sum(kw, axis=0, keepdims=True)
            m_scr[h:h + 1, :] = jnp.broadcast_to(m_new, (1, LANE))
            mu = jnp.mean(hv, axis=-1, keepdims=True)
            hc = hv - mu
            var = jnp.mean(hc * hc, axis=-1, keepdims=True)
            hn = hc * lax.rsqrt(var + D_NORM_EPS) * nw_ref[:, hs]
            out_ref[sl, hs] = (_sigmoid(o_ref[sl, hs]) * hn).astype(BF16)
        return carry_

    lax.fori_loop(0, n_chunks, chunk, 0)
    conv_out[...] = carry[...]
    c_out[...] = c_scr[...]
    n_out[...] = n_scr[...]
    m_out[...] = m_scr[...]


def _mlstm(u3, conv_hist, c0, n0, m0, p, tm):
    bn, t, _ = u3.shape
    hist8 = jnp.pad(conv_hist, ((0, 0), (8 - (D_CONV - 1), 0), (0, 0)))
    n0p = jnp.pad(n0, ((0, 0), (0, 8 - D_HEADS), (0, 0)))
    m0p = jnp.pad(jnp.broadcast_to(m0[:, :, None], (bn, D_HEADS, LANE)), ((0, 0), (0, 8 - D_HEADS), (0, 0)))
    full = lambda a: pl.BlockSpec(a.shape, lambda b, i: (0,) * a.ndim)
    per_b = lambda *s: pl.BlockSpec((None,) + s, lambda b, i: (b,) + (0,) * len(s))
    consts = (p["conv_w"], p["conv_b"], p["gate_bias"], p["norm_w"])
    outs = pl.pallas_call(
        functools.partial(_mlstm_body, n_chunks=tm // MLSTM_CHUNK),
        grid=(bn, t // tm),
        in_specs=[pl.BlockSpec((None, tm, 2 * W), lambda b, i: (b, i, 0)),
                  pl.BlockSpec((None, tm, W), lambda b, i: (b, i, SLOT["dv"])),
                  pl.BlockSpec((None, tm, W), lambda b, i: (b, i, SLOT["do"])),
                  pl.BlockSpec((None, tm, LANE), lambda b, i: (b, i, COL_DG // LANE)),
                  per_b(8, 2 * W), per_b(D_HEADS, D_HEAD, D_HEAD), per_b(8, LANE), per_b(8, LANE)]
                 + [full(c) for c in consts],
        out_specs=[pl.BlockSpec((None, tm, W), lambda b, i: (b, i, 0)),
                   per_b(8, 2 * W), per_b(D_HEADS, D_HEAD, D_HEAD), per_b(8, LANE), per_b(8, LANE)],
        out_shape=[jax.ShapeDtypeStruct((bn, t, W), BF16),
                   jax.ShapeDtypeStruct((bn, 8, 2 * W), F32),
                   jax.ShapeDtypeStruct((bn, D_HEADS, D_HEAD, D_HEAD), F32),
                   jax.ShapeDtypeStruct((bn, 8, LANE), F32),
                   jax.ShapeDtypeStruct((bn, 8, LANE), F32)],
        scratch_shapes=[pltpu.VMEM((8, 2 * W), F32), pltpu.VMEM((tm, W), F32), pltpu.VMEM((tm, W), F32),
                        pltpu.VMEM((D_HEADS, D_HEAD, D_HEAD), F32), pltpu.VMEM((8, LANE), F32),
                        pltpu.VMEM((8, LANE), F32)],
        compiler_params=_params(("parallel", "arbitrary")),
        name="mlstm",
    )(u3, u3, u3, u3, hist8, c0, n0p, m0p, *consts)
    out_d, conv8, c, n8, m8 = outs
    return out_d, conv8[:, 8 - (D_CONV - 1):], c, n8[:, :D_HEADS], m8[:, :D_HEADS, 0]


def _merge_body(x_ref, g_ref, a_ref, b_ref, c_ref, d_ref, wg_ref, wb_ref, wo_ref, o_ref):
    g = g_ref[...].astype(BF16)
    c = jnp.concatenate([c_ref[h] for h in range(C_HEADS)], axis=1)
    merged = None
    for i, br in enumerate((a_ref[...], b_ref[...], c, d_ref[...])):
        term = _sigmoid(_dot(g, wg_ref[i])) * _dot(br, wb_ref[i])
        merged = term if merged is None else merged + term
    o_ref[...] = x_ref[...] + _dot(merged.astype(BF16), wo_ref[...])


def _merge(x3, u3, out_a, out_b, out_c, out_d, wg, wb, wo, tm):
    bn, t, _ = x3.shape
    once = lambda a: pl.BlockSpec(a.shape, lambda b, i: (0,) * a.ndim, pipeline_mode=pl.Buffered(1))
    bblk = pl.BlockSpec((None, tm, W), lambda b, i: (b, i, 0))
    xblk = pl.BlockSpec((None, tm, D_MODEL), lambda b, i: (b, i, 0))
    return pl.pallas_call(
        _merge_body,
        grid=(bn, t // tm),
        in_specs=[xblk, pl.BlockSpec((None, tm, GATE_RANK), lambda b, i: (b, i, COL_G // GATE_RANK)),
                  bblk, bblk, pl.BlockSpec((None, C_HEADS, tm, C_HEAD), lambda b, i: (b, 0, i, 0)), bblk,
                  once(wg), once(wb), once(wo)],
        out_specs=xblk,
        out_shape=jax.ShapeDtypeStruct((bn, t, D_MODEL), F32),
        compiler_params=_params(("parallel", "parallel")),
        name="merge",
    )(x3, u3, out_a, out_b, out_c, out_d, wg, wb, wo)


def _ffn_epilogue(acc, gf_ref, final_norm):
    return _rmsnorm(acc, gf_ref[...]) if final_norm else acc


def _ffn_dense_body(x_ref, g_ref, wg_ref, wu_ref, wd_ref, gf_ref, o_ref, h_scr, acc, *, final_norm):
    j = pl.program_id(1)

    @pl.when(j == 0)
    def _():
        x = x_ref[...]
        h_scr[...] = _rmsnorm(x, g_ref[...]).astype(BF16)
        acc[...] = x

    h = h_scr[...]
    a = _dot(h, wg_ref[...])
    act = (a * _sigmoid(a)) * _dot(h, wu_ref[...])
    acc[...] += _dot(act.astype(BF16), wd_ref[...])

    @pl.when(j == pl.num_programs(1) - 1)
    def _():
        o_ref[...] = _ffn_epilogue(acc[...], gf_ref, final_norm)


def _ffn_dense(x2d, g, wgate, wup, wdown, g_final, final_norm, tm, tf=512):
    n = x2d.shape[0]
    xblk = pl.BlockSpec((tm, D_MODEL), lambda i, j: (i, 0))
    vec = pl.BlockSpec((1, D_MODEL), lambda i, j: (0, 0))
    return pl.pallas_call(
        functools.partial(_ffn_dense_body, final_norm=final_norm),
        grid=(n // tm, D_FF // tf),
        in_specs=[xblk, vec,
                  pl.BlockSpec((D_MODEL, tf), lambda i, j: (0, j)),
                  pl.BlockSpec((D_MODEL, tf), lambda i, j: (0, j)),
                  pl.BlockSpec((tf, D_MODEL), lambda i, j: (j, 0)), vec],
        out_specs=xblk,
        out_shape=jax.ShapeDtypeStruct((n, D_MODEL), F32),
        scratch_shapes=[pltpu.VMEM((tm, D_MODEL), BF16), pltpu.VMEM((tm, D_MODEL), F32)],
        compiler_params=_params(("parallel", "arbitrary")),
        name="ffn_dense",
    )(x2d, g.reshape(1, D_MODEL), wgate, wup, wdown, g_final.reshape(1, D_MODEL))


def _ffn_moe_body(x_ref, g_ref, rt_ref, wg_ref, wu_ref, wd_ref, gf_ref, o_ref,
                  h_scr, acc, gates, *, final_norm):
    e = pl.program_id(1)

    @pl.when(e == 0)
    def _():
        x = x_ref[...]
        h = _rmsnorm(x, g_ref[...])
        h_scr[...] = h.astype(BF16)
        acc[...] = x
        h_hi, h_lo = _split2(h)
        r_hi, r_lo = _split2(rt_ref[...])
        logits = _dot(h_hi, r_hi) + _dot(h_lo, r_hi) + _dot(h_hi, r_lo)
        lane = lax.broadcasted_iota(jnp.int32, logits.shape, 1)
        logits = jnp.where(lane < N_EXPERTS, logits, -jnp.inf)
        v1 = jnp.max(logits, axis=-1, keepdims=True)
        i1 = jnp.min(jnp.where(logits == v1, lane, LANE), axis=-1, keepdims=True)
        rest = jnp.where(lane == i1, -jnp.inf, logits)
        v2 = jnp.max(rest, axis=-1, keepdims=True)
        i2 = jnp.min(jnp.where(rest == v2, lane, LANE), axis=-1, keepdims=True)
        e2 = jnp.exp(v2 - v1)
        den = 1.0 + e2
        gates[...] = jnp.where(lane == i1, 1.0 / den, 0.0) + jnp.where(lane == i2, e2 / den, 0.0)

    h = h_scr[...]
    lane = lax.broadcasted_iota(jnp.int32, gates.shape, 1)
    gate = jnp.sum(jnp.where(lane == e, gates[...], 0.0), axis=-1, keepdims=True)
    a = _dot(h, wg_ref[...])
    act = (a * _sigmoid(a)) * _dot(h, wu_ref[...])
    acc[...] += gate * _dot(act.astype(BF16), wd_ref[...])

    @pl.when(e == pl.num_programs(1) - 1)
    def _():
        o_ref[...] = _ffn_epilogue(acc[...], gf_ref, final_norm)


def _ffn_moe(x2d, g, router, wgate, wup, wdown, g_final, final_norm, tm):
    n = x2d.shape[0]
    xblk = pl.BlockSpec((tm, D_MODEL), lambda i, j: (i, 0))
    vec = pl.BlockSpec((1, D_MODEL), lambda i, j: (0, 0))
    return pl.pallas_call(
        functools.partial(_ffn_moe_body, final_norm=final_norm),
        grid=(n // tm, N_EXPERTS),
        in_specs=[xblk, vec,
                  pl.BlockSpec((D_MODEL, LANE), lambda i, j: (0, 0)),
                  pl.BlockSpec((None, D_MODEL, EXPERT_PAD), lambda i, j: (j, 0, 0)),
                  pl.BlockSpec((None, D_MODEL, EXPERT_PAD), lambda i, j: (j, 0, 0)),
                  pl.BlockSpec((None, EXPERT_PAD, D_MODEL), lambda i, j: (j, 0, 0)), vec],
        out_specs=xblk,
        out_shape=jax.ShapeDtypeStruct((n, D_MODEL), F32),
        scratch_shapes=[pltpu.VMEM((tm, D_MODEL), BF16), pltpu.VMEM((tm, D_MODEL), F32),
                        pltpu.VMEM((tm, LANE), F32)],
        compiler_params=_params(("parallel", "arbitrary")),
        name="ffn_moe",
    )(x2d, g.reshape(1, D_MODEL), router, wgate, wup, wdown, g_final.reshape(1, D_MODEL))


def _pack_w_in(w):
    a, b, c, d = w[:, :OFF_B], w[:, OFF_B:OFF_C], w[:, OFF_C:OFF_D], w[:, OFF_D:OFF_G]
    g = w[:, OFF_G:]
    zeros = lambda k: jnp.zeros((D_MODEL, k), w.dtype)
    lora = a[:, 3 * W:]
    gates = d[:, 4 * W:]
    cols = [d[:, :4 * W], a[:, :3 * W], b, c,
            lora, zeros(LORA_PAD - lora.shape[1]), g,
            gates, zeros(LANE - gates.shape[1]), zeros(WP - COL_DG - LANE)]
    return jnp.concatenate(cols, axis=1).astype(BF16)


def _layer_params(l, wts):
    (norm_mix, norm_ffn, norm_final, w_in, rwkv_mu, rwkv_w0, rwkv_w2, rwkv_a0, rwkv_a2,
     rwkv_g2, rwkv_k_k, rwkv_k_a, rwkv_r_k, rwkv_ln_w, rwkv_ln_b, pool_w, pool_scale,
     mlstm_conv_w, mlstm_conv_b, mlstm_i_bias, mlstm_f_bias, mlstm_norm_w,
     w_branch, w_merge_gate, w_out, ffn_w_gate, ffn_w_up, ffn_w_down,
     moe_router, moe_w_gate, moe_w_up, moe_w_down) = wts
    n_l = sum(A_LORA)
    row = lambda v: v.reshape(1, -1)

    def lora_pad(wm, start):
        return jnp.pad(wm, ((start, LORA_PAD - start - wm.shape[0]), (0, 0))).astype(BF16)

    hid = jnp.arange(W) // A_HEAD
    p = {
        "norm_mix": norm_mix[l], "norm_ffn": norm_ffn[l],
        "wp": _pack_w_in(w_in[l]),
        "rwkv": {
            "mu_rkv": rwkv_mu[l, :3 * W].reshape(3, W),
            "mu_l": jnp.pad(rwkv_mu[l, 3 * W:], (0, LORA_PAD - n_l)).reshape(1, LORA_PAD),
            "w0": row(rwkv_w0[l]), "a0": row(rwkv_a0[l]),
            "k_k": row(rwkv_k_k[l]), "k_a": row(rwkv_k_a[l]),
            "w2p": lora_pad(rwkv_w2[l], 0),
            "a2p": lora_pad(rwkv_a2[l], A_LORA[0]),
            "g2p": lora_pad(rwkv_g2[l], A_LORA[0] + A_LORA[1]),
            "head_ones": (hid[:, None] == hid[None, :]).astype(BF16),
            "r_k": row(rwkv_r_k[l]), "ln_w": row(rwkv_ln_w[l]), "ln_b": row(rwkv_ln_b[l]),
        },
        "pool_w": pool_w[l], "pool_scale": pool_scale[l],
        "mlstm": {
            "conv_w": mlstm_conv_w[l], "conv_b": row(mlstm_conv_b[l]),
            "gate_bias": jnp.pad(jnp.concatenate([mlstm_i_bias[l], mlstm_f_bias[l]]),
                                 (0, LANE - 2 * D_HEADS)).reshape(1, LANE),
            "norm_w": row(mlstm_norm_w[l]),
        },
        "wg": w_merge_gate[l].astype(BF16), "wb": w_branch[l].astype(BF16), "wo": w_out[l].astype(BF16),
    }
    if l % 2 == 0:
        p["ffn"] = (ffn_w_gate[l // 2].astype(BF16), ffn_w_up[l // 2].astype(BF16),
                    ffn_w_down[l // 2].astype(BF16))
    else:
        pe = EXPERT_PAD - D_FF_EXPERT
        p["moe"] = (jnp.pad(moe_router[l // 2], ((0, 0), (0, LANE - N_EXPERTS))),
                    jnp.pad(moe_w_gate[l // 2], ((0, 0), (0, 0), (0, pe))).astype(BF16),
                    jnp.pad(moe_w_up[l // 2], ((0, 0), (0, 0), (0, pe))).astype(BF16),
                    jnp.pad(moe_w_down[l // 2], ((0, 0), (0, pe), (0, 0))).astype(BF16))
    return p


def _slot(u3, name):
    s = SLOT[name] * W
    return u3[:, :, s:s + W]


def _run_trunk(x, start_pos, cache_k, cache_v, wkv0, shift0, pool0, conv0, c0, n0, m0,
               layers, norm_final):
    bn, t, _ = x.shape
    n = bn * t
    tm_tok = min(256, t)
    tm_row = min(512, n)
    x2 = x.reshape(n, D_MODEL)
    ks, vs, wkvs, shifts, pools, convs, cs, ns, ms = ([] for _ in range(9))
    for l, p in enumerate(layers):
        u2 = _inproj(x2, p["norm_mix"], p["wp"])
        u3 = u2.reshape(bn, t, WP)

        out_a, wkv = _rwkv(u3, shift0[l], wkv0[l], p["rwkv"], tm_tok)

        out_b = _pool(u3, pool0[l], p["pool_w"], p["pool_scale"], start_pos, tm_tok)

        k_c, v_c = _slot(u3, "ck"), _slot(u3, "cv")
        q_h, k_h, v_h = _to_heads(u3, (SLOT["cq"], SLOT["ck"], SLOT["cv"]), min(512, t))
        if cache_k is None:
            out_c = _sb_attention(q_h, k_h, v_h, 0, min(SB_TQ, t))
        else:
            past = cache_k.shape[2]
            (kc_h,) = _to_heads(cache_k[l].reshape(bn, past, W), (0,), min(512, past))
            (vc_h,) = _to_heads(cache_v[l].reshape(bn, past, W), (0,), min(512, past))
            out_c = _sb_attention(q_h, jnp.concatenate([kc_h, k_h], axis=2),
                                  jnp.concatenate([vc_h, v_h], axis=2), past, t)

        out_d, conv_new, c_new, n_new, m_new = _mlstm(u3, conv0[l], c0[l], n0[l], m0[l], p["mlstm"], tm_tok)

        x2 = _merge(x2.reshape(bn, t, D_MODEL), u3, out_a, out_b, out_c, out_d,
                    p["wg"], p["wb"], p["wo"], tm_tok).reshape(n, D_MODEL)
        last = l == len(layers) - 1
        if "ffn" in p:
            x2 = _ffn_dense(x2, p["norm_ffn"], *p["ffn"], norm_final, last, tm_row)
        else:
            x2 = _ffn_moe(x2, p["norm_ffn"], *p["moe"], norm_final, last, tm_row)

        ks.append(k_c.reshape(bn, t, C_HEADS, C_HEAD))
        vs.append(v_c.reshape(bn, t, C_HEADS, C_HEAD))
        wkvs.append(wkv)
        u_last = u3[:, -1]
        shifts.append(jnp.concatenate(
            [u_last[:, SLOT["ar"] * W:SLOT["ar"] * W + 3 * W], u_last[:, COL_LORA:COL_LORA + sum(A_LORA)]], axis=1))
        pools.append(jnp.concatenate([pool0[l], _slot(u3, "b")[:, -B_HIST:]], axis=1)[:, -B_HIST:])
        convs.append(conv_new)
        cs.append(c_new)
        ns.append(n_new)
        ms.append(m_new)
    st = lambda lst: jnp.stack(lst, axis=0)
    return (x2.reshape(bn, t, D_MODEL),
            (st(ks), st(vs), st(wkvs), st(shifts), st(pools), st(convs), st(cs), st(ns), st(ms)))


def kernel(x_prompt, x_sample, cache_sb_k, cache_sb_v, state_rwkv_wkv, state_rwkv_shift, state_pool,
           state_mlstm_conv, state_mlstm_c, state_mlstm_n, state_mlstm_m, norm_mix, norm_ffn, norm_final,
           w_in, rwkv_mu, rwkv_w0, rwkv_w2, rwkv_a0, rwkv_a2, rwkv_g2, rwkv_k_k, rwkv_k_a, rwkv_r_k,
           rwkv_ln_w, rwkv_ln_b, pool_w, pool_scale, mlstm_conv_w, mlstm_conv_b, mlstm_i_bias,
           mlstm_f_bias, mlstm_norm_w, w_branch, w_merge_gate, w_out, ffn_w_gate, ffn_w_up, ffn_w_down,
           moe_router, moe_w_gate, moe_w_up, moe_w_down):
    wts = (norm_mix, norm_ffn, norm_final, w_in, rwkv_mu, rwkv_w0, rwkv_w2, rwkv_a0, rwkv_a2,
           rwkv_g2, rwkv_k_k, rwkv_k_a, rwkv_r_k, rwkv_ln_w, rwkv_ln_b, pool_w, pool_scale,
           mlstm_conv_w, mlstm_conv_b, mlstm_i_bias, mlstm_f_bias, mlstm_norm_w,
           w_branch, w_merge_gate, w_out, ffn_w_gate, ffn_w_up, ffn_w_down,
           moe_router, moe_w_gate, moe_w_up, moe_w_down)
    depth = w_in.shape[0]
    layers = [_layer_params(l, wts) for l in range(depth)]
    bp = x_prompt.shape[0]
    z = lambda *s: jnp.zeros((depth, bp) + s, F32)
    y_p, st_p = _run_trunk(
        x_prompt, 0, None, None, z(A_HEADS, A_HEAD, A_HEAD), z(A_IN), z(B_HIST, W),
        z(D_CONV - 1, 2 * W), z(D_HEADS, D_HEAD, D_HEAD), z(D_HEADS, D_HEAD), z(D_HEADS),
        layers, norm_final)
    y_s, st_s = _run_trunk(
        x_sample, cache_sb_k.shape[2], cache_sb_k, cache_sb_v, state_rwkv_wkv, state_rwkv_shift,
        state_pool, state_mlstm_conv, state_mlstm_c, state_mlstm_n, state_mlstm_m,
        layers, norm_final)
    return (y_p, y_s) + tuple(st_p) + tuple(st_s)
```

```python
import functools

import jax
import jax.numpy as jnp
from jax import lax
from jax.experimental import pallas as pl
from jax.experimental.pallas import tpu as pltpu

F32 = jnp.float32
BF16 = jnp.bfloat16

D_MODEL = 2048
DEPTH = 2
W = 512
A_HEAD, A_HEADS = 64, 8
A_LORA = (32, 32, 96)
A_IN = 3 * W + sum(A_LORA)
A_GN_EPS = 64e-5
B_WINDOWS = (2, 4, 8, 16)
B_HIST = 15
C_HEAD, C_HEADS = 64, 8
D_HEADS, D_HEAD = 4, 128
D_CONV = 4
D_NORM_EPS = 1e-6
GATE_RANK = 256
OFF_B = A_IN
OFF_C = OFF_B + W
OFF_D = OFF_C + 3 * W
OFF_G = OFF_D + 4 * W + 2 * D_HEADS
IN_WIDTH = OFF_G + GATE_RANK
D_FF = 5632
N_EXPERTS = 8
D_FF_EXPERT = D_FF // N_EXPERTS
NORM_EPS = 1e-6

LANE = 128
V7X_VMEM_BYTES = 64 * 1024 * 1024
VMEM_LIMIT = V7X_VMEM_BYTES - 8 * 1024 * 1024

SLOT = {name: i for i, name in enumerate(
    ("dq", "dk", "dv", "do", "ar", "ak", "av", "b", "cq", "ck", "cv"))}
COL_LORA = 11 * W
COL_G = COL_LORA + 256
COL_DG = COL_G + 256
IN_TN = 1280
IN_TM = 1024
WP = 6400
LORA_PAD = 256
EXPERT_PAD = 768

RWKV_CHUNK = 64
MLSTM_CHUNK = 64
SB_BLOCK = 128
SB_TQ = 1024
SB_DEAD = -120.0


def _params(sem, vmem=VMEM_LIMIT):
    return pltpu.CompilerParams(dimension_semantics=sem, vmem_limit_bytes=vmem)


def _split2(x):
    hi = x.astype(BF16)
    lo = (x - hi.astype(F32)).astype(BF16)
    return hi, lo


def _split3(x):
    hi = x.astype(BF16)
    r1 = x - hi.astype(F32)
    mid = r1.astype(BF16)
    lo = (r1 - mid.astype(F32)).astype(BF16)
    return hi, mid, lo


def _dot(a, b):
    return jnp.dot(a, b, preferred_element_type=F32)


def _dot_nt(a, b):
    return lax.dot_general(a, b, (((1,), (1,)), ((), ())), preferred_element_type=F32)


def _bdot(a, b):
    return lax.dot_general(a, b, (((2,), (1,)), ((0,), (0,))), preferred_element_type=F32)


def _bdot_nt(a, b):
    return lax.dot_general(a, b, (((2,), (2,)), ((0,), (0,))), preferred_element_type=F32)


def _bdot_tn(a, b):
    return lax.dot_general(a, b, (((1,), (1,)), ((0,), (0,))), preferred_element_type=F32)


def _dot_lhs01(m01, x):
    hi, mid, lo = _split3(x)
    return _dot(m01, hi) + _dot(m01, mid) + _dot(m01, lo)


def _sigmoid(x):
    return 1.0 / (1.0 + jnp.exp(-x))


def _log_sigmoid(x):
    return jnp.minimum(x, 0.0) - jnp.log1p(jnp.exp(-jnp.abs(x)))


def _rmsnorm(x, g):
    ms = jnp.mean(x * x, axis=-1, keepdims=True)
    return x * lax.rsqrt(ms + NORM_EPS) * g


def _inproj_body(x_ref, g_ref, w_ref, o_ref, h_scr):
    @pl.when(pl.program_id(1) == 0)
    def _():
        h_scr[...] = _rmsnorm(x_ref[...], g_ref[...]).astype(BF16)

    o_ref[...] = _dot(h_scr[...], w_ref[...])


def _inproj(x2d, g, wp):
    n = x2d.shape[0]
    tm = min(IN_TM, n)
    return pl.pallas_call(
        _inproj_body,
        grid=(n // tm, WP // IN_TN),
        in_specs=[pl.BlockSpec((tm, D_MODEL), lambda i, j: (i, 0)),
                  pl.BlockSpec((1, D_MODEL), lambda i, j: (0, 0)),
                  pl.BlockSpec((D_MODEL, IN_TN), lambda i, j: (0, j))],
        out_specs=pl.BlockSpec((tm, IN_TN), lambda i, j: (i, j)),
        out_shape=jax.ShapeDtypeStruct((n, WP), F32),
        scratch_shapes=[pltpu.VMEM((tm, D_MODEL), BF16)],
        compiler_params=_params(("parallel", "arbitrary")),
        name="inproj",
    )(x2d, g.reshape(1, D_MODEL), wp)


def _shift_rows(u, carry_row):
    rolled = pltpu.roll(u, 1, axis=0)
    row = lax.broadcasted_iota(jnp.int32, u.shape, 0)
    return jnp.where(row == 0, carry_row, rolled)


def _rwkv_prep(ur_ref, uk_ref, uv_ref, ul_ref, sr_ref, sk_ref, sv_ref, sl_ref,
               mu_ref, mul_ref, w0_ref, a0_ref, kk_ref, ka_ref, w2_ref, a2_ref, g2_ref, ones_ref,
               cr, ck, cv, cl):
    tm = ur_ref.shape[0]

    @pl.when(pl.program_id(1) == 0)
    def _():
        cr[0:1, :] = sr_ref[...]
        ck[0:1, :] = sk_ref[...]
        cv[0:1, :] = sv_ref[...]
        cl[0:1, :] = sl_ref[...]

    def mix(u_ref, c_ref, mu):
        u = u_ref[...]
        prev = _shift_rows(u, c_ref[0:1, :])
        c_ref[0:1, :] = u[tm - 1:tm, :]
        return u + (prev - u) * mu

    r = mix(ur_ref, cr, mu_ref[0:1, :])
    k = mix(uk_ref, ck, mu_ref[1:2, :])
    v = mix(uv_ref, cv, mu_ref[2:3, :])
    xl = mix(ul_ref, cl, mul_ref[...])

    w_lora = _dot(jnp.tanh(xl).astype(BF16), w2_ref[...])
    a_lora = _dot(xl.astype(BF16), a2_ref[...])
    g = _dot(_sigmoid(xl).astype(BF16), g2_ref[...])

    y = -(w0_ref[...] + w_lora)
    w_log = -(jnp.maximum(y, 0.0) + jnp.log1p(jnp.exp(-jnp.abs(y)))) - 0.5
    log_decay = -jnp.exp(w_log)
    a = _sigmoid(a0_ref[...] + a_lora)

    kk = k * kk_ref[...]
    hi, lo = _split2(kk * kk)
    ss = _dot(hi, ones_ref[...]) + _dot(lo, ones_ref[...])
    kk = kk / jnp.maximum(jnp.sqrt(ss), 1e-12)
    k2 = k * (1.0 + (a - 1.0) * ka_ref[...])

    return r, k2, v, kk, kk * a, log_decay, g


def _rwkv_chunks(r_ref, k_ref, v_ref, kk_ref, b_ref, lw_ref, y_ref, s_scr, n_chunks):
    c_len = RWKV_CHUNK
    hh = A_HEADS
    nb = hh * n_chunks
    row = lax.broadcasted_iota(jnp.int32, (nb, c_len, c_len), 1)
    col = lax.broadcasted_iota(jnp.int32, (nb, c_len, c_len), 2)
    tri_incl = (row >= col).astype(BF16)
    lower = row >= col
    strict = row > col
    eye = (row == col).astype(F32)

    load = lambda ref: ref[...].reshape(nb, c_len, A_HEAD)
    r, k, v, kk, b, logw = (load(ref) for ref in (r_ref, k_ref, v_ref, kk_ref, b_ref, lw_ref))
    w_hi, w_mid, w_lo = _split3(logw)
    lw = _bdot(tri_incl, w_hi) + _bdot(tri_incl, w_mid) + _bdot(tri_incl, w_lo)
    lw_prev = lw - logw
    lw_last = lw[:, c_len - 1:c_len, :]
    e_neg = jnp.exp(-lw)
    e_end = jnp.exp(lw_last - lw)
    kkm = (kk * jnp.exp(lw_prev)).astype(BF16)
    rm = r * jnp.exp(lw)
    kp = (k * e_neg).astype(BF16)
    bp = (b * e_neg).astype(BF16)
    kpp = (k * e_end).astype(BF16)
    bpp = (b * e_end).astype(BF16)
    vb = v.astype(BF16)
    rmb = rm.astype(BF16)

    a_vk = jnp.where(strict, _bdot_nt(kkm, kp), 0.0)
    a_pb = jnp.where(strict, _bdot_nt(kkm, bp), 0.0)
    rk = jnp.where(lower, _bdot_nt(rmb, kp), 0.0)
    rb = jnp.where(lower, _bdot_nt(rmb, bp), 0.0)

    n_pow = -a_pb
    t_inv = eye + n_pow
    for _ in range(5):
        npb = n_pow.astype(BF16)
        n_pow = _bdot(npb, npb)
        t_inv = t_inv + _bdot(t_inv.astype(BF16), n_pow.astype(BF16))
    tb = t_inv.astype(BF16)

    kktb = _bdot(tb, kkm).astype(BF16)
    pvb = _bdot(tb, _bdot(a_vk.astype(BF16), vb).astype(BF16)).astype(BF16)
    rbb = rb.astype(BF16)
    per_chunk = lambda x: x.reshape((hh, n_chunks) + x.shape[1:])
    m_mat = per_chunk(eye * jnp.exp(lw_last) - _bdot_tn(kktb, bpp))
    n_mat = per_chunk(_bdot_tn(vb, kpp) - _bdot_tn(pvb, bpp))
    q_mat = per_chunk((rm - _bdot(rbb, kktb)).astype(BF16))
    y0 = per_chunk(_bdot(rk.astype(BF16), vb) - _bdot(rbb, pvb))

    s = s_scr[...]
    for g in range(n_chunks):
        s_hi, s_lo = _split2(s)
        m_hi, m_lo = _split2(m_mat[:, g])
        y_ref[:, g * c_len:(g + 1) * c_len, :] = _bdot_nt(q_mat[:, g], s_hi) + y0[:, g]
        s = _bdot(s_hi, m_hi) + _bdot(s_lo, m_hi) + _bdot(s_hi, m_lo) + n_mat[:, g]
    s_scr[...] = s


def _rwkv_post(y, r, k, v, g, r_k, ln_w, ln_b, ones):
    def head_sum(x):
        hi, lo = _split2(x)
        return _dot(hi, ones) + _dot(lo, ones)

    mu = head_sum(y) * (1.0 / A_HEAD)
    yc = y - mu
    var = head_sum(yc * yc) * (1.0 / A_HEAD)
    yn = yc * lax.rsqrt(var + A_GN_EPS) * ln_w + ln_b
    bonus = head_sum(r * k * r_k) * v
    return ((yn + bonus) * g).astype(BF16)


def _rwkv_body(ur_ref, uk_ref, uv_ref, ul_ref, sr_ref, sk_ref, sv_ref, sl_ref,
               mu_ref, mul_ref, w0_ref, a0_ref, kk_ref, ka_ref, w2_ref, a2_ref, g2_ref, ones_ref,
               rk_ref, lnw_ref, lnb_ref, s0_ref,
               o_ref, sout_ref,
               cr, ck, cv, cl, s_scr, r_hm, k_hm, v_hm, kk_hm, b_hm, lw_hm, y_hm, *, n_chunks):
    @pl.when(pl.program_id(1) == 0)
    def _():
        s_scr[...] = s0_ref[...]

    r, k, v, kk, b, lw, g = _rwkv_prep(
        ur_ref, uk_ref, uv_ref, ul_ref, sr_ref, sk_ref, sv_ref, sl_ref, mu_ref, mul_ref, w0_ref, a0_ref,
        kk_ref, ka_ref, w2_ref, a2_ref, g2_ref, ones_ref, cr, ck, cv, cl)
    for x, ref in ((r, r_hm), (k, k_hm), (v, v_hm), (kk, kk_hm), (b, b_hm), (lw, lw_hm)):
        for h in range(A_HEADS):
            ref[h] = x[:, h * A_HEAD:(h + 1) * A_HEAD]
    _rwkv_chunks(r_hm, k_hm, v_hm, kk_hm, b_hm, lw_hm, y_hm, s_scr, n_chunks)
    sout_ref[...] = s_scr[...]
    y = jnp.concatenate([y_hm[h] for h in range(A_HEADS)], axis=1)
    o_ref[...] = _rwkv_post(y, r, k, v, g, rk_ref[...], lnw_ref[...], lnb_ref[...], ones_ref[...])


def _rwkv(u3, shift_prev, s0, p, tm):
    bn, t, _ = u3.shape
    sr = shift_prev[:, None, 0:W]
    sk = shift_prev[:, None, W:2 * W]
    sv = shift_prev[:, None, 2 * W:3 * W]
    sl = jnp.pad(shift_prev[:, None, 3 * W:], ((0, 0), (0, 0), (0, LORA_PAD - sum(A_LORA))))
    row = lambda c: pl.BlockSpec((None, 1, c), lambda b, i: (b, 0, 0))
    full = lambda a: pl.BlockSpec(a.shape, lambda b, i: (0,) * a.ndim)
    ublk = lambda slot: pl.BlockSpec((None, tm, W), lambda b, i, s=slot: (b, i, s))
    consts = (p["mu_rkv"], p["mu_l"], p["w0"], p["a0"], p["k_k"], p["k_a"],
              p["w2p"], p["a2p"], p["g2p"], p["head_ones"], p["r_k"], p["ln_w"], p["ln_b"])
    sblk = pl.BlockSpec((None, A_HEADS, A_HEAD, A_HEAD), lambda b, i: (b, 0, 0, 0))
    head_major = pltpu.VMEM((A_HEADS, tm, A_HEAD), F32)
    return pl.pallas_call(
        functools.partial(_rwkv_body, n_chunks=tm // RWKV_CHUNK),
        grid=(bn, t // tm),
        in_specs=[ublk(SLOT["ar"]), ublk(SLOT["ak"]), ublk(SLOT["av"]),
                  pl.BlockSpec((None, tm, LORA_PAD), lambda b, i: (b, i, COL_LORA // LORA_PAD)),
                  row(W), row(W), row(W), row(LORA_PAD)] + [full(c) for c in consts] + [sblk],
        out_specs=[pl.BlockSpec((None, tm, W), lambda b, i: (b, i, 0)), sblk],
        out_shape=[jax.ShapeDtypeStruct((bn, t, W), BF16),
                   jax.ShapeDtypeStruct((bn, A_HEADS, A_HEAD, A_HEAD), F32)],
        scratch_shapes=[pltpu.VMEM((8, W), F32)] * 3
                       + [pltpu.VMEM((8, LORA_PAD), F32), pltpu.VMEM((A_HEADS, A_HEAD, A_HEAD), F32)]
                       + [head_major] * 7,
        compiler_params=_params(("parallel", "arbitrary")),
        name="rwkv",
    )(u3, u3, u3, u3, sr, sk, sv, sl, *consts, s0)


def _pool_body(u_ref, hist_ref, w_ref, scale_ref, o_ref, carry, *, start_pos):
    tm = u_ref.shape[0]
    i = pl.program_id(1)

    @pl.when(i == 0)
    def _():
        carry[...] = hist_ref[...]

    x = u_ref[...]
    ext = jnp.concatenate([carry[...], x], axis=0)
    carry[...] = x[tm - 16:tm, :]
    s = ext
    sums = []
    for sh in (1, 2, 4, 8):
        s = s + pltpu.roll(s, sh, axis=0)
        sums.append(s[16:16 + tm, :])
    pos = start_pos + i * tm + lax.broadcasted_iota(jnp.int32, (tm, 1), 0)
    for gi, wlen in enumerate(B_WINDOWS):
        cs = slice(gi * LANE, (gi + 1) * LANE)
        cnt = jnp.minimum(wlen, pos + 1).astype(F32)
        d = sums[gi][:, cs] / cnt - x[:, cs]
        o_ref[:, cs] = (_dot(d.astype(BF16), w_ref[gi]) * scale_ref[:, cs]).astype(BF16)


def _pool(u3, hist, pool_w, pool_scale, start_pos, tm):
    bn, t, _ = u3.shape
    hist16 = jnp.pad(hist, ((0, 0), (1, 0), (0, 0)))
    return pl.pallas_call(
        functools.partial(_pool_body, start_pos=start_pos),
        grid=(bn, t // tm),
        in_specs=[pl.BlockSpec((None, tm, W), lambda b, i: (b, i, SLOT["b"])),
                  pl.BlockSpec((None, 16, W), lambda b, i: (b, 0, 0)),
                  pl.BlockSpec((4, LANE, LANE), lambda b, i: (0, 0, 0)),
                  pl.BlockSpec((1, W), lambda b, i: (0, 0))],
        out_specs=pl.BlockSpec((None, tm, W), lambda b, i: (b, i, 0)),
        out_shape=jax.ShapeDtypeStruct((bn, t, W), BF16),
        scratch_shapes=[pltpu.VMEM((16, W), F32)],
        compiler_params=_params(("parallel", "arbitrary")),
        name="pool",
    )(u3, hist16, pool_w.astype(BF16), pool_scale.reshape(1, W))


def _sb_body(q_ref, k_ref, v_ref, o_ref, acc_scr, o_scr, *, tq, q_off, n_masked):
    qi = pl.program_id(2)
    q = q_ref[...] * (C_HEAD ** -0.5)
    q_start = q_off + qi * tq
    first_masked = q_start // SB_BLOCK

    jj = lax.broadcasted_iota(jnp.int32, (2 * SB_BLOCK, 2 * SB_BLOCK), 0) % SB_BLOCK
    ss = lax.broadcasted_iota(jnp.int32, (2 * SB_BLOCK, 2 * SB_BLOCK), 1)
    cs = jnp.where(jnp.logical_or(ss >= SB_BLOCK, jj > ss), -1.0, 0.0).astype(BF16)
    aligned = q_off % SB_BLOCK == 0 and tq % SB_BLOCK == 0

    def block(kb, r0, masked):
        rows = slice(r0, tq)
        ks = pl.multiple_of(kb * SB_BLOCK, SB_BLOCK)
        kblk = k_ref[pl.ds(ks, SB_BLOCK), :]
        vblk = v_ref[pl.ds(ks, SB_BLOCK), :]
        z = _dot_nt(q[rows], kblk)
        zp = jnp.maximum(z, 0.0)
        zn = jnp.minimum(z, 0.0)
        l1p = jnp.log(1.0 + jnp.exp(zn - zp))
        nlf = zp + l1p
        s = zn - l1p
        if masked:
            q_pos = q_start + r0 + lax.broadcasted_iota(jnp.int32, z.shape, 0)
            mask = (ks + lax.broadcasted_iota(jnp.int32, z.shape, 1)) < q_pos
            nlf = jnp.where(mask, nlf, 0.0)
        hi, lo = _split2(nlf)
        lt = _dot(jnp.concatenate([hi, lo], axis=1), cs)
        att = jnp.exp(s + lt[:, :SB_BLOCK] + acc_scr[rows, :])
        if masked:
            att = jnp.where(mask, att, 0.0)
        o_scr[rows, :] += _dot(att.astype(BF16), vblk)
        acc_scr[rows, :] += lt[:, SB_BLOCK:]

    acc_scr[...] = jnp.zeros_like(acc_scr)
    o_scr[...] = jnp.zeros_like(o_scr)
    for m in range(n_masked - 1, -1, -1):
        block(first_masked + m, m * SB_BLOCK if aligned else 0, True)

    def live(i):
        return jnp.logical_and(i < first_masked, jnp.max(acc_scr[...]) > SB_DEAD)

    def step(i):
        block(first_masked - 1 - i, 0, False)
        return i + 1

    lax.while_loop(live, step, jnp.int32(0))
    o_ref[...] = o_scr[...].astype(BF16)


def _to_heads_body(*refs):
    n = len(refs) // 2
    for x_ref, o_ref in zip(refs[:n], refs[n:]):
        x = x_ref[...]
        for h in range(C_HEADS):
            o_ref[h] = x[:, h * C_HEAD:(h + 1) * C_HEAD].astype(BF16)


def _to_heads(x3, slots, tm):
    bn, t, _ = x3.shape
    return pl.pallas_call(
        _to_heads_body,
        grid=(bn, t // tm),
        in_specs=[pl.BlockSpec((None, tm, W), lambda b, i, s=s: (b, i, s)) for s in slots],
        out_specs=[pl.BlockSpec((None, C_HEADS, tm, C_HEAD), lambda b, i: (b, 0, i, 0))] * len(slots),
        out_shape=[jax.ShapeDtypeStruct((bn, C_HEADS, t, C_HEAD), BF16)] * len(slots),
        compiler_params=_params(("parallel", "parallel")),
        name="to_heads",
    )(*([x3] * len(slots)))


def _sb_attention(q, k, v, q_off, tq):
    bn, hh, t, d = q.shape
    sk = k.shape[2]
    sk_pad = -(-sk // SB_BLOCK) * SB_BLOCK
    if sk_pad != sk:
        pad = ((0, 0), (0, 0), (0, sk_pad - sk), (0, 0))
        k, v = jnp.pad(k, pad), jnp.pad(v, pad)
    assert tq % SB_BLOCK == 0 or t == tq
    n_masked = -(-((q_off % SB_BLOCK) + tq) // SB_BLOCK)
    kv = pl.BlockSpec((None, None, sk_pad, d), lambda b, h, i: (b, h, 0, 0))
    qo = pl.BlockSpec((None, None, tq, d), lambda b, h, i: (b, h, i, 0))
    return pl.pallas_call(
        functools.partial(_sb_body, tq=tq, q_off=q_off, n_masked=n_masked),
        grid=(bn, hh, t // tq),
        in_specs=[qo, kv, kv],
        out_specs=qo,
        out_shape=jax.ShapeDtypeStruct((bn, hh, t, d), BF16),
        scratch_shapes=[pltpu.VMEM((tq, SB_BLOCK), F32), pltpu.VMEM((tq, d), F32)],
        compiler_params=_params(("parallel", "parallel", "arbitrary")),
        name="sb_attention",
    )(q, k, v)


def _mlstm_body(qk_ref, v_ref, o_ref, g_ref, hist_ref, c0_ref, n0_ref, m0_ref,
                cw_ref, cb_ref, gb_ref, nw_ref,
                out_ref, conv_out, c_out, n_out, m_out,
                carry, q_scr, k_scr, c_scr, n_scr, m_scr, *, n_chunks):
    tm = qk_ref.shape[0]
    cl = MLSTM_CHUNK

    @pl.when(pl.program_id(1) == 0)
    def _():
        carry[...] = hist_ref[...]
        c_scr[...] = c0_ref[...]
        n_scr[...] = n0_ref[...]
        m_scr[...] = m0_ref[...]

    x = qk_ref[...]
    ext = jnp.concatenate([carry[...], x], axis=0)
    carry[...] = x[tm - 8:tm, :]
    conv = cb_ref[...] + x * cw_ref[D_CONV - 1:D_CONV, :]
    for sh in range(1, D_CONV):
        conv = conv + pltpu.roll(ext, sh, axis=0)[8:8 + tm, :] * cw_ref[D_CONV - 1 - sh:D_CONV - sh, :]
    conv = conv * _sigmoid(conv)
    q_scr[...] = conv[:, :W]
    k_scr[...] = conv[:, W:] * (D_HEAD ** -0.5)

    row = lax.broadcasted_iota(jnp.int32, (cl, cl), 0)
    col = lax.broadcasted_iota(jnp.int32, (cl, cl), 1)
    causal = row >= col
    tri_incl = causal.astype(BF16)

    def chunk(c, carry_):
        off = pl.multiple_of(c * cl, cl)
        sl = pl.ds(off, cl)
        gpre = g_ref[sl, :] + gb_ref[...]
        lfa = _log_sigmoid(gpre)
        bcol = _dot_lhs01(tri_incl, lfa)
        g_t = gpre.T
        b_t = bcol.T
        for h in range(D_HEADS):
            hs = slice(h * D_HEAD, (h + 1) * D_HEAD)
            q = q_scr[sl, hs]
            k = k_scr[sl, hs]
            v = v_ref[sl, hs]
            ig_col = gpre[:, h:h + 1]
            b_col = bcol[:, D_HEADS + h:D_HEADS + h + 1]
            ig_row = g_t[h:h + 1, :]
            b_row = b_t[D_HEADS + h:D_HEADS + h + 1, :]
            m_prev = m_scr[h:h + 1, 0:1]
            log_d = jnp.where(causal, b_col - b_row + ig_row, -jnp.inf)
            m_inter = b_col + m_prev
            m_t = jnp.maximum(m_inter, jnp.max(log_d, axis=-1, keepdims=True))
            dmat = jnp.exp(log_d - m_t)
            inter = jnp.exp(m_inter - m_t)
            qb = q.astype(BF16)
            vb = v.astype(BF16)
            w_qk = _dot_nt(qb, k.astype(BF16)) * dmat
            num = _dot(w_qk.astype(BF16), vb) + inter * _dot(qb, c_scr[h].astype(BF16))
            qn = jnp.sum(q * n_scr[h:h + 1, :], axis=-1, keepdims=True)
            den = jnp.sum(w_qk, axis=-1, keepdims=True) + inter * qn
            den = jnp.maximum(jnp.abs(den), jnp.exp(-m_t))
            hv = num / den
            m_new = m_t[cl - 1:cl, :]
            b_last = b_col[cl - 1:cl, :]
            decay = jnp.exp(b_last + m_prev - m_new)
            w_s = jnp.exp(b_last - b_col + ig_col - m_new)
            kw = k * w_s
            c_scr[h] = decay * c_scr[h] + _dot(kw.T.astype(BF16), vb)
            n_scr[h:h + 1, :] = decay * n_scr[h:h + 1, :] + jnp.sum(kw, axis=0, keepdims=True)
            m_scr[h:h + 1, :] = jnp.broadcast_to(m_new, (1, LANE))
            mu = jnp.mean(hv, axis=-1, keepdims=True)
            hc = hv - mu
            var = jnp.mean(hc * hc, axis=-1, keepdims=True)
            hn = hc * lax.rsqrt(var + D_NORM_EPS) * nw_ref[:, hs]
            out_ref[sl, hs] = (_sigmoid(o_ref[sl, hs]) * hn).astype(BF16)
        return carry_

    lax.fori_loop(0, n_chunks, chunk, 0)
    conv_out[...] = carry[...]
    c_out[...] = c_scr[...]
    n_out[...] = n_scr[...]
    m_out[...] = m_scr[...]


def _mlstm(u3, conv_hist, c0, n0, m0, p, tm):
    bn, t, _ = u3.shape
    hist8 = jnp.pad(conv_hist, ((0, 0), (8 - (D_CONV - 1), 0), (0, 0)))
    n0p = jnp.pad(n0, ((0, 0), (0, 8 - D_HEADS), (0, 0)))
    m0p = jnp.pad(jnp.broadcast_to(m0[:, :, None], (bn, D_HEADS, LANE)), ((0, 0), (0, 8 - D_HEADS), (0, 0)))
    full = lambda a: pl.BlockSpec(a.shape, lambda b, i: (0,) * a.ndim)
    per_b = lambda *s: pl.BlockSpec((None,) + s, lambda b, i: (b,) + (0,) * len(s))
    consts = (p["conv_w"], p["conv_b"], p["gate_bias"], p["norm_w"])
    outs = pl.pallas_call(
        functools.partial(_mlstm_body, n_chunks=tm // MLSTM_CHUNK),
        grid=(bn, t // tm),
        in_specs=[pl.BlockSpec((None, tm, 2 * W), lambda b, i: (b, i, 0)),
                  pl.BlockSpec((None, tm, W), lambda b, i: (b, i, SLOT["dv"])),
                  pl.BlockSpec((None, tm, W), lambda b, i: (b, i, SLOT["do"])),
                  pl.BlockSpec((None, tm, LANE), lambda b, i: (b, i, COL_DG // LANE)),
                  per_b(8, 2 * W), per_b(D_HEADS, D_HEAD, D_HEAD), per_b(8, LANE), per_b(8, LANE)]
                 + [full(c) for c in consts],
        out_specs=[pl.BlockSpec((None, tm, W), lambda b, i: (b, i, 0)),
                   per_b(8, 2 * W), per_b(D_HEADS, D_HEAD, D_HEAD), per_b(8, LANE), per_b(8, LANE)],
        out_shape=[jax.ShapeDtypeStruct((bn, t, W), BF16),
                   jax.ShapeDtypeStruct((bn, 8, 2 * W), F32),
                   jax.ShapeDtypeStruct((bn, D_HEADS, D_HEAD, D_HEAD), F32),
                   jax.ShapeDtypeStruct((bn, 8, LANE), F32),
                   jax.ShapeDtypeStruct((bn, 8, LANE), F32)],
        scratch_shapes=[pltpu.VMEM((8, 2 * W), F32), pltpu.VMEM((tm, W), F32), pltpu.VMEM((tm, W), F32),
                        pltpu.VMEM((D_HEADS, D_HEAD, D_HEAD), F32), pltpu.VMEM((8, LANE), F32),
                        pltpu.VMEM((8, LANE), F32)],
        compiler_params=_params(("parallel", "arbitrary")),
        name="mlstm",
    )(u3, u3, u3, u3, hist8, c0, n0p, m0p, *consts)
    out_d, conv8, c, n8, m8 = outs
    return out_d, conv8[:, 8 - (D_CONV - 1):], c, n8[:, :D_HEADS], m8[:, :D_HEADS, 0]


def _merge_body(x_ref, g_ref, a_ref, b_ref, c_ref, d_ref, wg_ref, wb_ref, wo_ref, o_ref):
    g = g_ref[...].astype(BF16)
    c = jnp.concatenate([c_ref[h] for h in range(C_HEADS)], axis=1)
    merged = None
    for i, br in enumerate((a_ref[...], b_ref[...], c, d_ref[...])):
        term = _sigmoid(_dot(g, wg_ref[i])) * _dot(br, wb_ref[i])
        merged = term if merged is None else merged + term
    o_ref[...] = x_ref[...] + _dot(merged.astype(BF16), wo_ref[...])


def _merge(x3, u3, out_a, out_b, out_c, out_d, wg, wb, wo, tm):
    bn, t, _ = x3.shape
    once = lambda a: pl.BlockSpec(a.shape, lambda b, i: (0,) * a.ndim, pipeline_mode=pl.Buffered(1))
    bblk = pl.BlockSpec((None, tm, W), lambda b, i: (b, i, 0))
    xblk = pl.BlockSpec((None, tm, D_MODEL), lambda b, i: (b, i, 0))
    return pl.pallas_call(
        _merge_body,
        grid=(bn, t // tm),
        in_specs=[xblk, pl.BlockSpec((None, tm, GATE_RANK), lambda b, i: (b, i, COL_G // GATE_RANK)),
                  bblk, bblk, pl.BlockSpec((None, C_HEADS, tm, C_HEAD), lambda b, i: (b, 0, i, 0)), bblk,
                  once(wg), once(wb), once(wo)],
        out_specs=xblk,
        out_shape=jax.ShapeDtypeStruct((bn, t, D_MODEL), F32),
        compiler_params=_params(("parallel", "parallel")),
        name="merge",
    )(x3, u3, out_a, out_b, out_c, out_d, wg, wb, wo)


def _ffn_epilogue(acc, gf_ref, final_norm):
    return _rmsnorm(acc, gf_ref[...]) if final_norm else acc


def _ffn_dense_body(x_ref, g_ref, wg_ref, wu_ref, wd_ref, gf_ref, o_ref, h_scr, acc, *, final_norm):
    j = pl.program_id(1)

    @pl.when(j == 0)
    def _():
        x = x_ref[...]
        h_scr[...] = _rmsnorm(x, g_ref[...]).astype(BF16)
        acc[...] = x

    h = h_scr[...]
    a = _dot(h, wg_ref[...])
    act = (a * _sigmoid(a)) * _dot(h, wu_ref[...])
    acc[...] += _dot(act.astype(BF16), wd_ref[...])

    @pl.when(j == pl.num_programs(1) - 1)
    def _():
        o_ref[...] = _ffn_epilogue(acc[...], gf_ref, final_norm)


def _ffn_dense(x2d, g, wgate, wup, wdown, g_final, final_norm, tm, tf=512):
    n = x2d.shape[0]
    xblk = pl.BlockSpec((tm, D_MODEL), lambda i, j: (i, 0))
    vec = pl.BlockSpec((1, D_MODEL), lambda i, j: (0, 0))
    return pl.pallas_call(
        functools.partial(_ffn_dense_body, final_norm=final_norm),
        grid=(n // tm, D_FF // tf),
        in_specs=[xblk, vec,
                  pl.BlockSpec((D_MODEL, tf), lambda i, j: (0, j)),
                  pl.BlockSpec((D_MODEL, tf), lambda i, j: (0, j)),
                  pl.BlockSpec((tf, D_MODEL), lambda i, j: (j, 0)), vec],
        out_specs=xblk,
        out_shape=jax.ShapeDtypeStruct((n, D_MODEL), F32),
        scratch_shapes=[pltpu.VMEM((tm, D_MODEL), BF16), pltpu.VMEM((tm, D_MODEL), F32)],
        compiler_params=_params(("parallel", "arbitrary")),
        name="ffn_dense",
    )(x2d, g.reshape(1, D_MODEL), wgate, wup, wdown, g_final.reshape(1, D_MODEL))


def _ffn_moe_body(x_ref, g_ref, rt_ref, wg_ref, wu_ref, wd_ref, gf_ref, o_ref,
                  h_scr, acc, gates, *, final_norm):
    e = pl.program_id(1)

    @pl.when(e == 0)
    def _():
        x = x_ref[...]
        h = _rmsnorm(x, g_ref[...])
        h_scr[...] = h.astype(BF16)
        acc[...] = x
        h_hi, h_lo = _split2(h)
        r_hi, r_lo = _split2(rt_ref[...])
        logits = _dot(h_hi, r_hi) + _dot(h_lo, r_hi) + _dot(h_hi, r_lo)
        lane = lax.broadcasted_iota(jnp.int32, logits.shape, 1)
        logits = jnp.where(lane < N_EXPERTS, logits, -jnp.inf)
        v1 = jnp.max(logits, axis=-1, keepdims=True)
        i1 = jnp.min(jnp.where(logits == v1, lane, LANE), axis=-1, keepdims=True)
        rest = jnp.where(lane == i1, -jnp.inf, logits)
        v2 = jnp.max(rest, axis=-1, keepdims=True)
        i2 = jnp.min(jnp.where(rest == v2, lane, LANE), axis=-1, keepdims=True)
        e2 = jnp.exp(v2 - v1)
        den = 1.0 + e2
        gates[...] = jnp.where(lane == i1, 1.0 / den, 0.0) + jnp.where(lane == i2, e2 / den, 0.0)

    h = h_scr[...]
    lane = lax.broadcasted_iota(jnp.int32, gates.shape, 1)
    gate = jnp.sum(jnp.where(lane == e, gates[...], 0.0), axis=-1, keepdims=True)
    a = _dot(h, wg_ref[...])
    act = (a * _sigmoid(a)) * _dot(h, wu_ref[...])
    acc[...] += gate * _dot(act.astype(BF16), wd_ref[...])

    @pl.when(e == pl.num_programs(1) - 1)
    def _():
        o_ref[...] = _ffn_epilogue(acc[...], gf_ref, final_norm)


def _ffn_moe(x2d, g, router, wgate, wup, wdown, g_final, final_norm, tm):
    n = x2d.shape[0]
    xblk = pl.BlockSpec((tm, D_MODEL), lambda i, j: (i, 0))
    vec = pl.BlockSpec((1, D_MODEL), lambda i, j: (0, 0))
    return pl.pallas_call(
        functools.partial(_ffn_moe_body, final_norm=final_norm),
        grid=(n // tm, N_EXPERTS),
        in_specs=[xblk, vec,
                  pl.BlockSpec((D_MODEL, LANE), lambda i, j: (0, 0)),
                  pl.BlockSpec((None, D_MODEL, EXPERT_PAD), lambda i, j: (j, 0, 0)),
                  pl.BlockSpec((None, D_MODEL, EXPERT_PAD), lambda i, j: (j, 0, 0)),
                  pl.BlockSpec((None, EXPERT_PAD, D_MODEL), lambda i, j: (j, 0, 0)), vec],
        out_specs=xblk,
        out_shape=jax.ShapeDtypeStruct((n, D_MODEL), F32),
        scratch_shapes=[pltpu.VMEM((tm, D_MODEL), BF16), pltpu.VMEM((tm, D_MODEL), F32),
                        pltpu.VMEM((tm, LANE), F32)],
        compiler_params=_params(("parallel", "arbitrary")),
        name="ffn_moe",
    )(x2d, g.reshape(1, D_MODEL), router, wgate, wup, wdown, g_final.reshape(1, D_MODEL))


def _pack_w_in(w):
    a, b, c, d = w[:, :OFF_B], w[:, OFF_B:OFF_C], w[:, OFF_C:OFF_D], w[:, OFF_D:OFF_G]
    g = w[:, OFF_G:]
    zeros = lambda k: jnp.zeros((D_MODEL, k), w.dtype)
    lora = a[:, 3 * W:]
    gates = d[:, 4 * W:]
    cols = [d[:, :4 * W], a[:, :3 * W], b, c,
            lora, zeros(LORA_PAD - lora.shape[1]), g,
            gates, zeros(LANE - gates.shape[1]), zeros(WP - COL_DG - LANE)]
    return jnp.concatenate(cols, axis=1).astype(BF16)


def _layer_params(l, wts):
    (norm_mix, norm_ffn, norm_final, w_in, rwkv_mu, rwkv_w0, rwkv_w2, rwkv_a0, rwkv_a2,
     rwkv_g2, rwkv_k_k, rwkv_k_a, rwkv_r_k, rwkv_ln_w, rwkv_ln_b, pool_w, pool_scale,
     mlstm_conv_w, mlstm_conv_b, mlstm_i_bias, mlstm_f_bias, mlstm_norm_w,
     w_branch, w_merge_gate, w_out, ffn_w_gate, ffn_w_up, ffn_w_down,
     moe_router, moe_w_gate, moe_w_up, moe_w_down) = wts
    n_l = sum(A_LORA)
    row = lambda v: v.reshape(1, -1)

    def lora_pad(wm, start):
        return jnp.pad(wm, ((start, LORA_PAD - start - wm.shape[0]), (0, 0))).astype(BF16)

    hid = jnp.arange(W) // A_HEAD
    p = {
        "norm_mix": norm_mix[l], "norm_ffn": norm_ffn[l],
        "wp": _pack_w_in(w_in[l]),
        "rwkv": {
            "mu_rkv": rwkv_mu[l, :3 * W].reshape(3, W),
            "mu_l": jnp.pad(rwkv_mu[l, 3 * W:], (0, LORA_PAD - n_l)).reshape(1, LORA_PAD),
            "w0": row(rwkv_w0[l]), "a0": row(rwkv_a0[l]),
            "k_k": row(rwkv_k_k[l]), "k_a": row(rwkv_k_a[l]),
            "w2p": lora_pad(rwkv_w2[l], 0),
            "a2p": lora_pad(rwkv_a2[l], A_LORA[0]),
            "g2p": lora_pad(rwkv_g2[l], A_LORA[0] + A_LORA[1]),
            "head_ones": (hid[:, None] == hid[None, :]).astype(BF16),
            "r_k": row(rwkv_r_k[l]), "ln_w": row(rwkv_ln_w[l]), "ln_b": row(rwkv_ln_b[l]),
        },
        "pool_w": pool_w[l], "pool_scale": pool_scale[l],
        "mlstm": {
            "conv_w": mlstm_conv_w[l], "conv_b": row(mlstm_conv_b[l]),
            "gate_bias": jnp.pad(jnp.concatenate([mlstm_i_bias[l], mlstm_f_bias[l]]),
                                 (0, LANE - 2 * D_HEADS)).reshape(1, LANE),
            "norm_w": row(mlstm_norm_w[l]),
        },
        "wg": w_merge_gate[l].astype(BF16), "wb": w_branch[l].astype(BF16), "wo": w_out[l].astype(BF16),
    }
    if l % 2 == 0:
        p["ffn"] = (ffn_w_gate[l // 2].astype(BF16), ffn_w_up[l // 2].astype(BF16),
                    ffn_w_down[l // 2].astype(BF16))
    else:
        pe = EXPERT_PAD - D_FF_EXPERT
        p["moe"] = (jnp.pad(moe_router[l // 2], ((0, 0), (0, LANE - N_EXPERTS))),
                    jnp.pad(moe_w_gate[l // 2], ((0, 0), (0, 0), (0, pe))).astype(BF16),
                    jnp.pad(moe_w_up[l // 2], ((0, 0), (0, 0), (0, pe))).astype(BF16),
                    jnp.pad(moe_w_down[l // 2], ((0, 0), (0, pe), (0, 0))).astype(BF16))
    return p


def _slot(u3, name):
    s = SLOT[name] * W
    return u3[:, :, s:s + W]


def _run_trunk(x, start_pos, cache_k, cache_v, wkv0, shift0, pool0, conv0, c0, n0, m0,
               layers, norm_final):
    bn, t, _ = x.shape
    n = bn * t
    tm_tok = min(256, t)
    tm_row = min(512, n)
    x2 = x.reshape(n, D_MODEL)
    ks, vs, wkvs, shifts, pools, convs, cs, ns, ms = ([] for _ in range(9))
    for l, p in enumerate(layers):
        u2 = _inproj(x2, p["norm_mix"], p["wp"])
        u3 = u2.reshape(bn, t, WP)

        out_a, wkv = _rwkv(u3, shift0[l], wkv0[l], p["rwkv"], tm_tok)

        out_b = _pool(u3, pool0[l], p["pool_w"], p["pool_scale"], start_pos, tm_tok)

        k_c, v_c = _slot(u3, "ck"), _slot(u3, "cv")
        q_h, k_h, v_h = _to_heads(u3, (SLOT["cq"], SLOT["ck"], SLOT["cv"]), min(512, t))
        if cache_k is None:
            out_c = _sb_attention(q_h, k_h, v_h, 0, min(SB_TQ, t))
        else:
            past = cache_k.shape[2]
            (kc_h,) = _to_heads(cache_k[l].reshape(bn, past, W), (0,), min(512, past))
            (vc_h,) = _to_heads(cache_v[l].reshape(bn, past, W), (0,), min(512, past))
            out_c = _sb_attention(q_h, jnp.concatenate([kc_h, k_h], axis=2),
                                  jnp.concatenate([vc_h, v_h], axis=2), past, t)

        out_d, conv_new, c_new, n_new, m_new = _mlstm(u3, conv0[l], c0[l], n0[l], m0[l], p["mlstm"], tm_tok)

        x2 = _merge(x2.reshape(bn, t, D_MODEL), u3, out_a, out_b, out_c, out_d,
                    p["wg"], p["wb"], p["wo"], tm_tok).reshape(n, D_MODEL)
        last = l == len(layers) - 1
        if "ffn" in p:
            x2 = _ffn_dense(x2, p["norm_ffn"], *p["ffn"], norm_final, last, tm_row)
        else:
            x2 = _ffn_moe(x2, p["norm_ffn"], *p["moe"], norm_final, last, tm_row)

        ks.append(k_c.reshape(bn, t, C_HEADS, C_HEAD))
        vs.append(v_c.reshape(bn, t, C_HEADS, C_HEAD))
        wkvs.append(wkv)
        u_last = u3[:, -1]
        shifts.append(jnp.concatenate(
            [u_last[:, SLOT["ar"] * W:SLOT["ar"] * W + 3 * W], u_last[:, COL_LORA:COL_LORA + sum(A_LORA)]], axis=1))
        pools.append(jnp.concatenate([pool0[l], _slot(u3, "b")[:, -B_HIST:]], axis=1)[:, -B_HIST:])
        convs.append(conv_new)
        cs.append(c_new)
        ns.append(n_new)
        ms.append(m_new)
    st = lambda lst: jnp.stack(lst, axis=0)
    return (x2.reshape(bn, t, D_MODEL),
            (st(ks), st(vs), st(wkvs), st(shifts), st(pools), st(convs), st(cs), st(ns), st(ms)))


def kernel(x_prompt, x_sample, cache_sb_k, cache_sb_v, state_rwkv_wkv, state_rwkv_shift, state_pool,
           state_mlstm_conv, state_mlstm_c, state_mlstm_n, state_mlstm_m, norm_mix, norm_ffn, norm_final,
           w_in, rwkv_mu, rwkv_w0, rwkv_w2, rwkv_a0, rwkv_a2, rwkv_g2, rwkv_k_k, rwkv_k_a, rwkv_r_k,
           rwkv_ln_w, rwkv_ln_b, pool_w, pool_scale, mlstm_conv_w, mlstm_conv_b, mlstm_i_bias,
           mlstm_f_bias, mlstm_norm_w, w_branch, w_merge_gate, w_out, ffn_w_gate, ffn_w_up, ffn_w_down,
           moe_router, moe_w_gate, moe_w_up, moe_w_down):
    wts = (norm_mix, norm_ffn, norm_final, w_in, rwkv_mu, rwkv_w0, rwkv_w2, rwkv_a0, rwkv_a2,
           rwkv_g2, rwkv_k_k, rwkv_k_a, rwkv_r_k, rwkv_ln_w, rwkv_ln_b, pool_w, pool_scale,
           mlstm_conv_w, mlstm_conv_b, mlstm_i_bias, mlstm_f_bias, mlstm_norm_w,
           w_branch, w_merge_gate, w_out, ffn_w_gate, ffn_w_up, ffn_w_down,
           moe_router, moe_w_gate, moe_w_up, moe_w_down)
    depth = w_in.shape[0]
    layers = [_layer_params(l, wts) for l in range(depth)]
    bp = x_prompt.shape[0]
    z = lambda *s: jnp.zeros((depth, bp) + s, F32)
    y_p, st_p = _run_trunk(
        x_prompt, 0, None, None, z(A_HEADS, A_HEAD, A_HEAD), z(A_IN), z(B_HIST, W),
        z(D_CONV - 1, 2 * W), z(D_HEADS, D_HEAD, D_HEAD), z(D_HEADS, D_HEAD), z(D_HEADS),
        layers, norm_final)
    y_s, st_s = _run_trunk(
        x_sample, cache_sb_k.shape[2], cache_sb_k, cache_sb_v, state_rwkv_wkv, state_rwkv_shift,
        state_pool, state_mlstm_conv, state_mlstm_c, state_mlstm_n, state_mlstm_m,
        layers, norm_final)
    return (y_p, y_s) + tuple(st_p) + tuple(st_s)
```

```python
import functools

import jax
import jax.numpy as jnp
from jax import lax
from jax.experimental import pallas as pl
from jax.experimental.pallas import tpu as pltpu

F32 = jnp.float32
BF16 = jnp.bfloat16

D_MODEL = 2048
DEPTH = 2
W = 512
A_HEAD, A_HEADS = 64, 8
A_LORA = (32, 32, 96)
A_IN = 3 * W + sum(A_LORA)
A_GN_EPS = 64e-5
B_WINDOWS = (2, 4, 8, 16)
B_HIST = 15
C_HEAD, C_HEADS = 64, 8
D_HEADS, D_HEAD = 4, 128
D_CONV = 4
D_NORM_EPS = 1e-6
GATE_RANK = 256
OFF_B = A_IN
OFF_C = OFF_B + W
OFF_D = OFF_C + 3 * W
OFF_G = OFF_D + 4 * W + 2 * D_HEADS
IN_WIDTH = OFF_G + GATE_RANK
D_FF = 5632
N_EXPERTS = 8
D_FF_EXPERT = D_FF // N_EXPERTS
NORM_EPS = 1e-6

LANE = 128
V7X_VMEM_BYTES = 64 * 1024 * 1024
VMEM_LIMIT = V7X_VMEM_BYTES - 8 * 1024 * 1024

SLOT = {name: i for i, name in enumerate(
    ("dq", "dk", "dv", "do", "ar", "ak", "av", "b", "cq", "ck", "cv"))}
COL_LORA = 11 * W
COL_G = COL_LORA + 256
COL_DG = COL_G + 256
IN_TN = 1280
IN_TM = 1024
WP = 6400
LORA_PAD = 256
EXPERT_PAD = 768

RWKV_CHUNK = 64
MLSTM_CHUNK = 128
SB_BLOCK = 128
SB_TQ = 256
SB_DEAD = -120.0


def _params(sem, vmem=VMEM_LIMIT):
    return pltpu.CompilerParams(dimension_semantics=sem, vmem_limit_bytes=vmem)


def _split2(x):
    hi = x.astype(BF16)
    lo = (x - hi.astype(F32)).astype(BF16)
    return hi, lo


def _split3(x):
    hi = x.astype(BF16)
    r1 = x - hi.astype(F32)
    mid = r1.astype(BF16)
    lo = (r1 - mid.astype(F32)).astype(BF16)
    return hi, mid, lo


def _dot(a, b):
    return jnp.dot(a, b, preferred_element_type=F32)


def _dot_nt(a, b):
    return lax.dot_general(a, b, (((1,), (1,)), ((), ())), preferred_element_type=F32)


def _bdot(a, b):
    return lax.dot_general(a, b, (((2,), (1,)), ((0,), (0,))), preferred_element_type=F32)


def _bdot_nt(a, b):
    return lax.dot_general(a, b, (((2,), (2,)), ((0,), (0,))), preferred_element_type=F32)


def _bdot_tn(a, b):
    return lax.dot_general(a, b, (((1,), (1,)), ((0,), (0,))), preferred_element_type=F32)


def _dot_lhs01(m01, x):
    hi, mid, lo = _split3(x)
    return _dot(m01, hi) + _dot(m01, mid) + _dot(m01, lo)


def _sigmoid(x):
    return 1.0 / (1.0 + jnp.exp(-x))


def _log_sigmoid(x):
    return jnp.minimum(x, 0.0) - jnp.log1p(jnp.exp(-jnp.abs(x)))


def _rmsnorm(x, g):
    ms = jnp.mean(x * x, axis=-1, keepdims=True)
    return x * lax.rsqrt(ms + NORM_EPS) * g


def _inproj_body(x_ref, g_ref, w_ref, o_ref, h_scr):
    @pl.when(pl.program_id(1) == 0)
    def _():
        h_scr[...] = _rmsnorm(x_ref[...], g_ref[...]).astype(BF16)

    o_ref[...] = _dot(h_scr[...], w_ref[...])


def _inproj(x2d, g, wp):
    n = x2d.shape[0]
    tm = min(IN_TM, n)
    return pl.pallas_call(
        _inproj_body,
        grid=(n // tm, WP // IN_TN),
        in_specs=[pl.BlockSpec((tm, D_MODEL), lambda i, j: (i, 0)),
                  pl.BlockSpec((1, D_MODEL), lambda i, j: (0, 0)),
                  pl.BlockSpec((D_MODEL, IN_TN), lambda i, j: (0, j))],
        out_specs=pl.BlockSpec((tm, IN_TN), lambda i, j: (i, j)),
        out_shape=jax.ShapeDtypeStruct((n, WP), F32),
        scratch_shapes=[pltpu.VMEM((tm, D_MODEL), BF16)],
        compiler_params=_params(("parallel", "arbitrary")),
        name="inproj",
    )(x2d, g.reshape(1, D_MODEL), wp)


def _shift_rows(u, carry_row):
    rolled = pltpu.roll(u, 1, axis=0)
    row = lax.broadcasted_iota(jnp.int32, u.shape, 0)
    return jnp.where(row == 0, carry_row, rolled)


def _rwkv_prep(ur_ref, uk_ref, uv_ref, ul_ref, sr_ref, sk_ref, sv_ref, sl_ref,
               mu_ref, mul_ref, w0_ref, a0_ref, kk_ref, ka_ref, w2_ref, a2_ref, g2_ref, ones_ref,
               cr, ck, cv, cl):
    tm = ur_ref.shape[0]

    @pl.when(pl.program_id(1) == 0)
    def _():
        cr[0:1, :] = sr_ref[...]
        ck[0:1, :] = sk_ref[...]
        cv[0:1, :] = sv_ref[...]
        cl[0:1, :] = sl_ref[...]

    def mix(u_ref, c_ref, mu):
        u = u_ref[...]
        prev = _shift_rows(u, c_ref[0:1, :])
        c_ref[0:1, :] = u[tm - 1:tm, :]
        return u + (prev - u) * mu

    r = mix(ur_ref, cr, mu_ref[0:1, :])
    k = mix(uk_ref, ck, mu_ref[1:2, :])
    v = mix(uv_ref, cv, mu_ref[2:3, :])
    xl = mix(ul_ref, cl, mul_ref[...])

    w_lora = _dot(jnp.tanh(xl).astype(BF16), w2_ref[...])
    a_lora = _dot(xl.astype(BF16), a2_ref[...])
    g = _dot(_sigmoid(xl).astype(BF16), g2_ref[...])

    y = -(w0_ref[...] + w_lora)
    w_log = -(jnp.maximum(y, 0.0) + jnp.log1p(jnp.exp(-jnp.abs(y)))) - 0.5
    log_decay = -jnp.exp(w_log)
    a = _sigmoid(a0_ref[...] + a_lora)

    kk = k * kk_ref[...]
    hi, lo = _split2(kk * kk)
    ss = _dot(hi, ones_ref[...]) + _dot(lo, ones_ref[...])
    kk = kk / jnp.maximum(jnp.sqrt(ss), 1e-12)
    k2 = k * (1.0 + (a - 1.0) * ka_ref[...])

    return r, k2, v, kk, kk * a, log_decay, g


def _rwkv_chunks(r_ref, k_ref, v_ref, kk_ref, b_ref, lw_ref, y_ref, s_scr, n_chunks):
    c_len = RWKV_CHUNK
    hh = A_HEADS
    nb = hh * n_chunks
    row = lax.broadcasted_iota(jnp.int32, (nb, c_len, c_len), 1)
    col = lax.broadcasted_iota(jnp.int32, (nb, c_len, c_len), 2)
    tri_incl = (row >= col).astype(BF16)
    lower = row >= col
    strict = row > col
    eye = (row == col).astype(F32)

    load = lambda ref: ref[...].reshape(nb, c_len, A_HEAD)
    r, k, v, kk, b, logw = (load(ref) for ref in (r_ref, k_ref, v_ref, kk_ref, b_ref, lw_ref))
    w_hi, w_mid, w_lo = _split3(logw)
    lw = _bdot(tri_incl, w_hi) + _bdot(tri_incl, w_mid) + _bdot(tri_incl, w_lo)
    lw_prev = lw - logw
    lw_last = lw[:, c_len - 1:c_len, :]
    e_neg = jnp.exp(-lw)
    e_end = jnp.exp(lw_last - lw)
    kkm = (kk * jnp.exp(lw_prev)).astype(BF16)
    rm = r * jnp.exp(lw)
    kp = (k * e_neg).astype(BF16)
    bp = (b * e_neg).astype(BF16)
    kpp = (k * e_end).astype(BF16)
    bpp = (b * e_end).astype(BF16)
    vb = v.astype(BF16)
    rmb = rm.astype(BF16)

    a_vk = jnp.where(strict, _bdot_nt(kkm, kp), 0.0)
    a_pb = jnp.where(strict, _bdot_nt(kkm, bp), 0.0)
    rk = jnp.where(lower, _bdot_nt(rmb, kp), 0.0)
    rb = jnp.where(lower, _bdot_nt(rmb, bp), 0.0)

    n_pow = -a_pb
    t_inv = eye + n_pow
    for _ in range(5):
        npb = n_pow.astype(BF16)
        n_pow = _bdot(npb, npb)
        t_inv = t_inv + _bdot(t_inv.astype(BF16), n_pow.astype(BF16))
    tb = t_inv.astype(BF16)

    kktb = _bdot(tb, kkm).astype(BF16)
    pvb = _bdot(tb, _bdot(a_vk.astype(BF16), vb).astype(BF16)).astype(BF16)
    rbb = rb.astype(BF16)
    per_chunk = lambda x: x.reshape((hh, n_chunks) + x.shape[1:])
    m_mat = per_chunk(eye * jnp.exp(lw_last) - _bdot_tn(kktb, bpp))
    n_mat = per_chunk(_bdot_tn(vb, kpp) - _bdot_tn(pvb, bpp))
    q_mat = per_chunk((rm - _bdot(rbb, kktb)).astype(BF16))
    y0 = per_chunk(_bdot(rk.astype(BF16), vb) - _bdot(rbb, pvb))

    s = s_scr[...]
    for g in range(n_chunks):
        s_hi, s_lo = _split2(s)
        m_hi, m_lo = _split2(m_mat[:, g])
        y_ref[:, g * c_len:(g + 1) * c_len, :] = _bdot_nt(q_mat[:, g], s_hi) + y0[:, g]
        s = _bdot(s_hi, m_hi) + _bdot(s_lo, m_hi) + _bdot(s_hi, m_lo) + n_mat[:, g]
    s_scr[...] = s


def _rwkv_post(y, r, k, v, g, r_k, ln_w, ln_b, ones):
    def head_sum(x):
        hi, lo = _split2(x)
        return _dot(hi, ones) + _dot(lo, ones)

    mu = head_sum(y) * (1.0 / A_HEAD)
    yc = y - mu
    var = head_sum(yc * yc) * (1.0 / A_HEAD)
    yn = yc * lax.rsqrt(var + A_GN_EPS) * ln_w + ln_b
    bonus = head_sum(r * k * r_k) * v
    return ((yn + bonus) * g).astype(BF16)


def _rwkv_body(ur_ref, uk_ref, uv_ref, ul_ref, sr_ref, sk_ref, sv_ref, sl_ref,
               mu_ref, mul_ref, w0_ref, a0_ref, kk_ref, ka_ref, w2_ref, a2_ref, g2_ref, ones_ref,
               rk_ref, lnw_ref, lnb_ref, s0_ref,
               o_ref, sout_ref,
               cr, ck, cv, cl, s_scr, r_hm, k_hm, v_hm, kk_hm, b_hm, lw_hm, y_hm, *, n_chunks):
    @pl.when(pl.program_id(1) == 0)
    def _():
        s_scr[...] = s0_ref[...]

    r, k, v, kk, b, lw, g = _rwkv_prep(
        ur_ref, uk_ref, uv_ref, ul_ref, sr_ref, sk_ref, sv_ref, sl_ref, mu_ref, mul_ref, w0_ref, a0_ref,
        kk_ref, ka_ref, w2_ref, a2_ref, g2_ref, ones_ref, cr, ck, cv, cl)
    for x, ref in ((r, r_hm), (k, k_hm), (v, v_hm), (kk, kk_hm), (b, b_hm), (lw, lw_hm)):
        for h in range(A_HEADS):
            ref[h] = x[:, h * A_HEAD:(h + 1) * A_HEAD]
    _rwkv_chunks(r_hm, k_hm, v_hm, kk_hm, b_hm, lw_hm, y_hm, s_scr, n_chunks)
    sout_ref[...] = s_scr[...]
    y = jnp.concatenate([y_hm[h] for h in range(A_HEADS)], axis=1)
    o_ref[...] = _rwkv_post(y, r, k, v, g, rk_ref[...], lnw_ref[...], lnb_ref[...], ones_ref[...])


def _rwkv(u3, shift_prev, s0, p, tm):
    bn, t, _ = u3.shape
    sr = shift_prev[:, None, 0:W]
    sk = shift_prev[:, None, W:2 * W]
    sv = shift_prev[:, None, 2 * W:3 * W]
    sl = jnp.pad(shift_prev[:, None, 3 * W:], ((0, 0), (0, 0), (0, LORA_PAD - sum(A_LORA))))
    row = lambda c: pl.BlockSpec((None, 1, c), lambda b, i: (b, 0, 0))
    full = lambda a: pl.BlockSpec(a.shape, lambda b, i: (0,) * a.ndim)
    ublk = lambda slot: pl.BlockSpec((None, tm, W), lambda b, i, s=slot: (b, i, s))
    consts = (p["mu_rkv"], p["mu_l"], p["w0"], p["a0"], p["k_k"], p["k_a"],
              p["w2p"], p["a2p"], p["g2p"], p["head_ones"], p["r_k"], p["ln_w"], p["ln_b"])
    sblk = pl.BlockSpec((None, A_HEADS, A_HEAD, A_HEAD), lambda b, i: (b, 0, 0, 0))
    head_major = pltpu.VMEM((A_HEADS, tm, A_HEAD), F32)
    return pl.pallas_call(
        functools.partial(_rwkv_body, n_chunks=tm // RWKV_CHUNK),
        grid=(bn, t // tm),
        in_specs=[ublk(SLOT["ar"]), ublk(SLOT["ak"]), ublk(SLOT["av"]),
                  pl.BlockSpec((None, tm, LORA_PAD), lambda b, i: (b, i, COL_LORA // LORA_PAD)),
                  row(W), row(W), row(W), row(LORA_PAD)] + [full(c) for c in consts] + [sblk],
        out_specs=[pl.BlockSpec((None, tm, W), lambda b, i: (b, i, 0)), sblk],
        out_shape=[jax.ShapeDtypeStruct((bn, t, W), BF16),
                   jax.ShapeDtypeStruct((bn, A_HEADS, A_HEAD, A_HEAD), F32)],
        scratch_shapes=[pltpu.VMEM((8, W), F32)] * 3
                       + [pltpu.VMEM((8, LORA_PAD), F32), pltpu.VMEM((A_HEADS, A_HEAD, A_HEAD), F32)]
                       + [head_major] * 7,
        compiler_params=_params(("parallel", "arbitrary")),
        name="rwkv",
    )(u3, u3, u3, u3, sr, sk, sv, sl, *consts, s0)


def _pool_body(u_ref, hist_ref, w_ref, scale_ref, o_ref, carry, *, start_pos):
    tm = u_ref.shape[0]
    i = pl.program_id(1)

    @pl.when(i == 0)
    def _():
        carry[...] = hist_ref[...]

    x = u_ref[...]
    ext = jnp.concatenate([carry[...], x], axis=0)
    carry[...] = x[tm - 16:tm, :]
    s = ext
    sums = []
    for sh in (1, 2, 4, 8):
        s = s + pltpu.roll(s, sh, axis=0)
        sums.append(s[16:16 + tm, :])
    pos = start_pos + i * tm + lax.broadcasted_iota(jnp.int32, (tm, 1), 0)
    for gi, wlen in enumerate(B_WINDOWS):
        cs = slice(gi * LANE, (gi + 1) * LANE)
        cnt = jnp.minimum(wlen, pos + 1).astype(F32)
        d = sums[gi][:, cs] / cnt - x[:, cs]
        o_ref[:, cs] = (_dot(d.astype(BF16), w_ref[gi]) * scale_ref[:, cs]).astype(BF16)


def _pool(u3, hist, pool_w, pool_scale, start_pos, tm):
    bn, t, _ = u3.shape
    hist16 = jnp.pad(hist, ((0, 0), (1, 0), (0, 0)))
    return pl.pallas_call(
        functools.partial(_pool_body, start_pos=start_pos),
        grid=(bn, t // tm),
        in_specs=[pl.BlockSpec((None, tm, W), lambda b, i: (b, i, SLOT["b"])),
                  pl.BlockSpec((None, 16, W), lambda b, i: (b, 0, 0)),
                  pl.BlockSpec((4, LANE, LANE), lambda b, i: (0, 0, 0)),
                  pl.BlockSpec((1, W), lambda b, i: (0, 0))],
        out_specs=pl.BlockSpec((None, tm, W), lambda b, i: (b, i, 0)),
        out_shape=jax.ShapeDtypeStruct((bn, t, W), BF16),
        scratch_shapes=[pltpu.VMEM((16, W), F32)],
        compiler_params=_params(("parallel", "arbitrary")),
        name="pool",
    )(u3, hist16, pool_w.astype(BF16), pool_scale.reshape(1, W))


def _sb_body(q_ref, k_ref, v_ref, o_ref, acc_scr, o_scr, *, tq, q_off, n_masked):
    qi = pl.program_id(1)
    n_pairs = W // LANE
    q = q_ref[...] * (C_HEAD ** -0.5)
    q_start = q_off + qi * tq
    first_masked = q_start // SB_BLOCK
    even = lax.broadcasted_iota(jnp.int32, (tq, LANE), 1) < C_HEAD
    qz = jnp.zeros((tq, LANE), BF16)
    qs = jnp.stack([jnp.stack([jnp.where(even, q[:, p * LANE:(p + 1) * LANE], qz),
                               jnp.where(even, qz, q[:, p * LANE:(p + 1) * LANE])])
                    for p in range(n_pairs)])

    jj = lax.broadcasted_iota(jnp.int32, (2 * SB_BLOCK, 2 * SB_BLOCK), 0) % SB_BLOCK
    ss = lax.broadcasted_iota(jnp.int32, (2 * SB_BLOCK, 2 * SB_BLOCK), 1)
    cs = jnp.where(jnp.logical_or(ss >= SB_BLOCK, jj > ss), -1.0, 0.0).astype(BF16)
    aligned = q_off % SB_BLOCK == 0 and tq % SB_BLOCK == 0

    def block(kb, r0, masked):
        n = tq - r0
        ks = pl.multiple_of(kb * SB_BLOCK, SB_BLOCK)
        kblk = k_ref[pl.ds(ks, SB_BLOCK), :]
        vblk = v_ref[pl.ds(ks, SB_BLOCK), :]
        kst = jnp.stack([kblk[:, p * LANE:(p + 1) * LANE] for p in range(n_pairs)])
        vst = jnp.stack([vblk[:, p * LANE:(p + 1) * LANE] for p in range(n_pairs)])
        z = _bdot_nt(qs[:, :, r0:, :].reshape(n_pairs, 2 * n, LANE), kst).reshape(2 * n_pairs, n, SB_BLOCK)
        zp = jnp.maximum(z, 0.0)
        zn = jnp.minimum(z, 0.0)
        l1p = jnp.log(1.0 + jnp.exp(zn - zp))
        nlf = zp + l1p
        s = zn - l1p
        if masked:
            q_pos = q_start + r0 + lax.broadcasted_iota(jnp.int32, (n, SB_BLOCK), 0)
            mask = ((ks + lax.broadcasted_iota(jnp.int32, (n, SB_BLOCK), 1)) < q_pos)[None]
            nlf = jnp.where(mask, nlf, 0.0)
        hi, lo = _split2(nlf)
        lt = _dot(jnp.concatenate([hi, lo], axis=2).reshape(2 * n_pairs * n, 2 * SB_BLOCK), cs)
        lt = lt.reshape(2 * n_pairs, n, 2 * SB_BLOCK)
        att = jnp.exp(s + lt[:, :, :SB_BLOCK] + acc_scr[:, r0:, :])
        if masked:
            att = jnp.where(mask, att, 0.0)
        pv = _bdot(att.astype(BF16).reshape(n_pairs, 2 * n, SB_BLOCK), vst)
        o_scr[:, r0:, :] += pv.reshape(2 * n_pairs, n, LANE)
        acc_scr[:, r0:, :] += lt[:, :, SB_BLOCK:]

    acc_scr[...] = jnp.zeros_like(acc_scr)
    o_scr[...] = jnp.zeros_like(o_scr)
    for m in range(n_masked - 1, -1, -1):
        block(first_masked + m, m * SB_BLOCK if aligned else 0, True)

    def live(i):
        return jnp.logical_and(i < first_masked, jnp.max(acc_scr[...]) > SB_DEAD)

    def step(i):
        block(first_masked - 1 - i, 0, False)
        return i + 1

    lax.while_loop(live, step, jnp.int32(0))
    o = o_scr[...]
    o_ref[...] = jnp.concatenate([jnp.where(even, o[2 * p], o[2 * p + 1]) for p in range(n_pairs)],
                                 axis=1).astype(BF16)


def _to_bf16_body(*refs):
    n = len(refs) // 2
    for x_ref, o_ref in zip(refs[:n], refs[n:]):
        o_ref[...] = x_ref[...].astype(BF16)


def _to_bf16(x3, slots, tm):
    bn, t, _ = x3.shape
    return pl.pallas_call(
        _to_bf16_body,
        grid=(bn, t // tm),
        in_specs=[pl.BlockSpec((None, tm, W), lambda b, i, s=s: (b, i, s)) for s in slots],
        out_specs=[pl.BlockSpec((None, tm, W), lambda b, i: (b, i, 0))] * len(slots),
        out_shape=[jax.ShapeDtypeStruct((bn, t, W), BF16)] * len(slots),
        compiler_params=_params(("parallel", "parallel")),
        name="to_bf16",
    )(*([x3] * len(slots)))


def _sb_attention(q, k, v, q_off, tq):
    bn, t, _ = q.shape
    sk = k.shape[1]
    sk_pad = -(-sk // SB_BLOCK) * SB_BLOCK
    if sk_pad != sk:
        pad = ((0, 0), (0, sk_pad - sk), (0, 0))
        k, v = jnp.pad(k, pad), jnp.pad(v, pad)
    assert tq % SB_BLOCK == 0 or t == tq
    n_masked = -(-((q_off % SB_BLOCK) + tq) // SB_BLOCK)
    kv = pl.BlockSpec((None, sk_pad, W), lambda b, i: (b, 0, 0), pipeline_mode=pl.Buffered(1))
    qo = pl.BlockSpec((None, tq, W), lambda b, i: (b, i, 0))
    return pl.pallas_call(
        functools.partial(_sb_body, tq=tq, q_off=q_off, n_masked=n_masked),
        grid=(bn, t // tq),
        in_specs=[qo, kv, kv],
        out_specs=qo,
        out_shape=jax.ShapeDtypeStruct((bn, t, W), BF16),
        scratch_shapes=[pltpu.VMEM((C_HEADS, tq, SB_BLOCK), F32), pltpu.VMEM((C_HEADS, tq, LANE), F32)],
        compiler_params=_params(("parallel", "arbitrary")),
        name="sb_attention",
    )(q, k, v)


def _mlstm_body(qk_ref, v_ref, o_ref, g_ref, hist_ref, c0_ref, n0_ref, m0_ref,
                cw_ref, cb_ref, gb_ref, nw_ref,
                out_ref, conv_out, c_out, n_out, m_out,
                carry, q_scr, k_scr, c_scr, n_scr, m_scr, *, n_chunks):
    tm = qk_ref.shape[0]
    cl = tm // n_chunks

    @pl.when(pl.program_id(1) == 0)
    def _():
        carry[...] = hist_ref[...]
        c_scr[...] = c0_ref[...]
        n_scr[...] = n0_ref[...]
        m_scr[...] = m0_ref[...]

    x = qk_ref[...]
    ext = jnp.concatenate([carry[...], x], axis=0)
    carry[...] = x[tm - 8:tm, :]
    conv = cb_ref[...] + x * cw_ref[D_CONV - 1:D_CONV, :]
    for sh in range(1, D_CONV):
        conv = conv + pltpu.roll(ext, sh, axis=0)[8:8 + tm, :] * cw_ref[D_CONV - 1 - sh:D_CONV - sh, :]
    conv = conv * _sigmoid(conv)
    q_scr[...] = conv[:, :W]
    k_scr[...] = conv[:, W:] * (D_HEAD ** -0.5)

    row = lax.broadcasted_iota(jnp.int32, (cl, cl), 0)
    col = lax.broadcasted_iota(jnp.int32, (cl, cl), 1)
    causal = row >= col
    tri_incl = causal.astype(BF16)

    def chunk(c, carry_):
        off = pl.multiple_of(c * cl, cl)
        sl = pl.ds(off, cl)
        gpre = g_ref[sl, :] + gb_ref[...]
        lfa = _log_sigmoid(gpre)
        bcol = _dot_lhs01(tri_incl, lfa)
        g_t = gpre.T
        b_t = bcol.T
        for h in range(D_HEADS):
            hs = slice(h * D_HEAD, (h + 1) * D_HEAD)
            q = q_scr[sl, hs]
            k = k_scr[sl, hs]
            v = v_ref[sl, hs]
            ig_col = gpre[:, h:h + 1]
            b_col = bcol[:, D_HEADS + h:D_HEADS + h + 1]
            ig_row = g_t[h:h + 1, :]
            b_row = b_t[D_HEADS + h:D_HEADS + h + 1, :]
            m_prev = m_scr[h:h + 1, 0:1]
            log_d = jnp.where(causal, b_col - b_row + ig_row, -jnp.inf)
            m_inter = b_col + m_prev
            m_t = jnp.maximum(m_inter, jnp.max(log_d, axis=-1, keepdims=True))
            dmat = jnp.exp(log_d - m_t)
            inter = jnp.exp(m_inter - m_t)
            qb = q.astype(BF16)
            vb = v.astype(BF16)
            w_qk = _dot_nt(qb, k.astype(BF16)) * dmat
            num = _dot(w_qk.astype(BF16), vb) + inter * _dot(qb, c_scr[h].astype(BF16))
            qn = jnp.sum(q * n_scr[h:h + 1, :], axis=-1, keepdims=True)
            den = jnp.sum(w_qk, axis=-1, keepdims=True) + inter * qn
            den = jnp.maximum(jnp.abs(den), jnp.exp(-m_t))
            hv = num / den
            m_new = m_t[cl - 1:cl, :]
            b_last = b_col[cl - 1:cl, :]
            decay = jnp.exp(b_last + m_prev - m_new)
            w_s = jnp.exp(b_last - b_col + ig_col - m_new)
            kw = k * w_s
            c_scr[h] = decay * c_scr[h] + _dot(kw.T.astype(BF16), vb)
            n_scr[h:h + 1, :] = decay * n_scr[h:h + 1, :] + jnp.sum(kw, axis=0, keepdims=True)
            m_scr[h:h + 1, :] = jnp.broadcast_to(m_new, (1, LANE))
            mu = jnp.mean(hv, axis=-1, keepdims=True)
            hc = hv - mu
            var = jnp.mean(hc * hc, axis=-1, keepdims=True)
            hn = hc * lax.rsqrt(var + D_NORM_EPS) * nw_ref[:, hs]
            out_ref[sl, hs] = (_sigmoid(o_ref[sl, hs]) * hn).astype(BF16)
        return carry_

    lax.fori_loop(0, n_chunks, chunk, 0)
    conv_out[...] = carry[...]
    c_out[...] = c_scr[...]
    n_out[...] = n_scr[...]
    m_out[...] = m_scr[...]


def _mlstm(u3, conv_hist, c0, n0, m0, p, tm):
    bn, t, _ = u3.shape
    hist8 = jnp.pad(conv_hist, ((0, 0), (8 - (D_CONV - 1), 0), (0, 0)))
    n0p = jnp.pad(n0, ((0, 0), (0, 8 - D_HEADS), (0, 0)))
    m0p = jnp.pad(jnp.broadcast_to(m0[:, :, None], (bn, D_HEADS, LANE)), ((0, 0), (0, 8 - D_HEADS), (0, 0)))
    full = lambda a: pl.BlockSpec(a.shape, lambda b, i: (0,) * a.ndim)
    per_b = lambda *s: pl.BlockSpec((None,) + s, lambda b, i: (b,) + (0,) * len(s))
    consts = (p["conv_w"], p["conv_b"], p["gate_bias"], p["norm_w"])
    outs = pl.pallas_call(
        functools.partial(_mlstm_body, n_chunks=tm // min(MLSTM_CHUNK, tm)),
        grid=(bn, t // tm),
        in_specs=[pl.BlockSpec((None, tm, 2 * W), lambda b, i: (b, i, 0)),
                  pl.BlockSpec((None, tm, W), lambda b, i: (b, i, SLOT["dv"])),
                  pl.BlockSpec((None, tm, W), lambda b, i: (b, i, SLOT["do"])),
                  pl.BlockSpec((None, tm, LANE), lambda b, i: (b, i, COL_DG // LANE)),
                  per_b(8, 2 * W), per_b(D_HEADS, D_HEAD, D_HEAD), per_b(8, LANE), per_b(8, LANE)]
                 + [full(c) for c in consts],
        out_specs=[pl.BlockSpec((None, tm, W), lambda b, i: (b, i, 0)),
                   per_b(8, 2 * W), per_b(D_HEADS, D_HEAD, D_HEAD), per_b(8, LANE), per_b(8, LANE)],
        out_shape=[jax.ShapeDtypeStruct((bn, t, W), BF16),
                   jax.ShapeDtypeStruct((bn, 8, 2 * W), F32),
                   jax.ShapeDtypeStruct((bn, D_HEADS, D_HEAD, D_HEAD), F32),
                   jax.ShapeDtypeStruct((bn, 8, LANE), F32),
                   jax.ShapeDtypeStruct((bn, 8, LANE), F32)],
        scratch_shapes=[pltpu.VMEM((8, 2 * W), F32), pltpu.VMEM((tm, W), F32), pltpu.VMEM((tm, W), F32),
                        pltpu.VMEM((D_HEADS, D_HEAD, D_HEAD), F32), pltpu.VMEM((8, LANE), F32),
                        pltpu.VMEM((8, LANE), F32)],
        compiler_params=_params(("parallel", "arbitrary")),
        name="mlstm",
    )(u3, u3, u3, u3, hist8, c0, n0p, m0p, *consts)
    out_d, conv8, c, n8, m8 = outs
    return out_d, conv8[:, 8 - (D_CONV - 1):], c, n8[:, :D_HEADS], m8[:, :D_HEADS, 0]


def _merge_body(x_ref, g_ref, a_ref, b_ref, c_ref, d_ref, wg_ref, wb_ref, wo_ref, o_ref):
    g = g_ref[...].astype(BF16)
    merged = None
    for i, br in enumerate((a_ref, b_ref, c_ref, d_ref)):
        term = _sigmoid(_dot(g, wg_ref[i])) * _dot(br[...], wb_ref[i])
        merged = term if merged is None else merged + term
    o_ref[...] = x_ref[...] + _dot(merged.astype(BF16), wo_ref[...])


def _merge(x3, u3, out_a, out_b, out_c, out_d, wg, wb, wo, tm):
    bn, t, _ = x3.shape
    once = lambda a: pl.BlockSpec(a.shape, lambda b, i: (0,) * a.ndim, pipeline_mode=pl.Buffered(1))
    bblk = pl.BlockSpec((None, tm, W), lambda b, i: (b, i, 0))
    xblk = pl.BlockSpec((None, tm, D_MODEL), lambda b, i: (b, i, 0))
    return pl.pallas_call(
        _merge_body,
        grid=(bn, t // tm),
        in_specs=[xblk, pl.BlockSpec((None, tm, GATE_RANK), lambda b, i: (b, i, COL_G // GATE_RANK)),
                  bblk, bblk, bblk, bblk, once(wg), once(wb), once(wo)],
        out_specs=xblk,
        out_shape=jax.ShapeDtypeStruct((bn, t, D_MODEL), F32),
        compiler_params=_params(("parallel", "parallel")),
        name="merge",
    )(x3, u3, out_a, out_b, out_c, out_d, wg, wb, wo)


def _ffn_epilogue(acc, gf_ref, final_norm):
    return _rmsnorm(acc, gf_ref[...]) if final_norm else acc


def _ffn_dense_body(x_ref, g_ref, wg_ref, wu_ref, wd_ref, gf_ref, o_ref, h_scr, acc, *, final_norm):
    j = pl.program_id(1)

    @pl.when(j == 0)
    def _():
        x = x_ref[...]
        h_scr[...] = _rmsnorm(x, g_ref[...]).astype(BF16)
        acc[...] = x

    h = h_scr[...]
    a = _dot(h, wg_ref[...])
    act = (a * _sigmoid(a)) * _dot(h, wu_ref[...])
    acc[...] += _dot(act.astype(BF16), wd_ref[...])

    @pl.when(j == pl.num_programs(1) - 1)
    def _():
        o_ref[...] = _ffn_epilogue(acc[...], gf_ref, final_norm)


def _ffn_dense(x2d, g, wgate, wup, wdown, g_final, final_norm, tm, tf=512):
    n = x2d.shape[0]
    xblk = pl.BlockSpec((tm, D_MODEL), lambda i, j: (i, 0))
    vec = pl.BlockSpec((1, D_MODEL), lambda i, j: (0, 0))
    return pl.pallas_call(
        functools.partial(_ffn_dense_body, final_norm=final_norm),
        grid=(n // tm, D_FF // tf),
        in_specs=[xblk, vec,
                  pl.BlockSpec((D_MODEL, tf), lambda i, j: (0, j)),
                  pl.BlockSpec((D_MODEL, tf), lambda i, j: (0, j)),
                  pl.BlockSpec((tf, D_MODEL), lambda i, j: (j, 0)), vec],
        out_specs=xblk,
        out_shape=jax.ShapeDtypeStruct((n, D_MODEL), F32),
        scratch_shapes=[pltpu.VMEM((tm, D_MODEL), BF16), pltpu.VMEM((tm, D_MODEL), F32)],
        compiler_params=_params(("parallel", "arbitrary")),
        name="ffn_dense",
    )(x2d, g.reshape(1, D_MODEL), wgate, wup, wdown, g_final.reshape(1, D_MODEL))


def _ffn_moe_body(x_ref, g_ref, rt_ref, wg_ref, wu_ref, wd_ref, gf_ref, o_ref,
                  h_scr, acc, gates, *, final_norm):
    e = pl.program_id(1)

    @pl.when(e == 0)
    def _():
        x = x_ref[...]
        h = _rmsnorm(x, g_ref[...])
        h_scr[...] = h.astype(BF16)
        acc[...] = x
        h_hi, h_lo = _split2(h)
        r_hi, r_lo = _split2(rt_ref[...])
        logits = _dot(h_hi, r_hi) + _dot(h_lo, r_hi) + _dot(h_hi, r_lo)
        lane = lax.broadcasted_iota(jnp.int32, logits.shape, 1)
        logits = jnp.where(lane < N_EXPERTS, logits, -jnp.inf)
        v1 = jnp.max(logits, axis=-1, keepdims=True)
        i1 = jnp.min(jnp.where(logits == v1, lane, LANE), axis=-1, keepdims=True)
        rest = jnp.where(lane == i1, -jnp.inf, logits)
        v2 = jnp.max(rest, axis=-1, keepdims=True)
        i2 = jnp.min(jnp.where(rest == v2, lane, LANE), axis=-1, keepdims=True)
        e2 = jnp.exp(v2 - v1)
        den = 1.0 + e2
        gates[...] = jnp.where(lane == i1, 1.0 / den, 0.0) + jnp.where(lane == i2, e2 / den, 0.0)

    h = h_scr[...]
    lane = lax.broadcasted_iota(jnp.int32, gates.shape, 1)
    gate = jnp.sum(jnp.where(lane == e, gates[...], 0.0), axis=-1, keepdims=True)
    a = _dot(h, wg_ref[...])
    act = (a * _sigmoid(a)) * _dot(h, wu_ref[...])
    acc[...] += gate * _dot(act.astype(BF16), wd_ref[...])

    @pl.when(e == pl.num_programs(1) - 1)
    def _():
        o_ref[...] = _ffn_epilogue(acc[...], gf_ref, final_norm)


def _ffn_moe(x2d, g, router, wgate, wup, wdown, g_final, final_norm, tm):
    n = x2d.shape[0]
    xblk = pl.BlockSpec((tm, D_MODEL), lambda i, j: (i, 0))
    vec = pl.BlockSpec((1, D_MODEL), lambda i, j: (0, 0))
    return pl.pallas_call(
        functools.partial(_ffn_moe_body, final_norm=final_norm),
        grid=(n // tm, N_EXPERTS),
        in_specs=[xblk, vec,
                  pl.BlockSpec((D_MODEL, LANE), lambda i, j: (0, 0)),
                  pl.BlockSpec((None, D_MODEL, EXPERT_PAD), lambda i, j: (j, 0, 0)),
                  pl.BlockSpec((None, D_MODEL, EXPERT_PAD), lambda i, j: (j, 0, 0)),
                  pl.BlockSpec((None, EXPERT_PAD, D_MODEL), lambda i, j: (j, 0, 0)), vec],
        out_specs=xblk,
        out_shape=jax.ShapeDtypeStruct((n, D_MODEL), F32),
        scratch_shapes=[pltpu.VMEM((tm, D_MODEL), BF16), pltpu.VMEM((tm, D_MODEL), F32),
                        pltpu.VMEM((tm, LANE), F32)],
        compiler_params=_params(("parallel", "arbitrary")),
        name="ffn_moe",
    )(x2d, g.reshape(1, D_MODEL), router, wgate, wup, wdown, g_final.reshape(1, D_MODEL))


def _pack_w_in(w):
    a, b, c, d = w[:, :OFF_B], w[:, OFF_B:OFF_C], w[:, OFF_C:OFF_D], w[:, OFF_D:OFF_G]
    g = w[:, OFF_G:]
    zeros = lambda k: jnp.zeros((D_MODEL, k), w.dtype)
    lora = a[:, 3 * W:]
    gates = d[:, 4 * W:]
    cols = [d[:, :4 * W], a[:, :3 * W], b, c,
            lora, zeros(LORA_PAD - lora.shape[1]), g,
            gates, zeros(LANE - gates.shape[1]), zeros(WP - COL_DG - LANE)]
    return jnp.concatenate(cols, axis=1).astype(BF16)


def _layer_params(l, wts):
    (norm_mix, norm_ffn, norm_final, w_in, rwkv_mu, rwkv_w0, rwkv_w2, rwkv_a0, rwkv_a2,
     rwkv_g2, rwkv_k_k, rwkv_k_a, rwkv_r_k, rwkv_ln_w, rwkv_ln_b, pool_w, pool_scale,
     mlstm_conv_w, mlstm_conv_b, mlstm_i_bias, mlstm_f_bias, mlstm_norm_w,
     w_branch, w_merge_gate, w_out, ffn_w_gate, ffn_w_up, ffn_w_down,
     moe_router, moe_w_gate, moe_w_up, moe_w_down) = wts
    n_l = sum(A_LORA)
    row = lambda v: v.reshape(1, -1)

    def lora_pad(wm, start):
        return jnp.pad(wm, ((start, LORA_PAD - start - wm.shape[0]), (0, 0))).astype(BF16)

    hid = jnp.arange(W) // A_HEAD
    p = {
        "norm_mix": norm_mix[l], "norm_ffn": norm_ffn[l],
        "wp": _pack_w_in(w_in[l]),
        "rwkv": {
            "mu_rkv": rwkv_mu[l, :3 * W].reshape(3, W),
            "mu_l": jnp.pad(rwkv_mu[l, 3 * W:], (0, LORA_PAD - n_l)).reshape(1, LORA_PAD),
            "w0": row(rwkv_w0[l]), "a0": row(rwkv_a0[l]),
            "k_k": row(rwkv_k_k[l]), "k_a": row(rwkv_k_a[l]),
            "w2p": lora_pad(rwkv_w2[l], 0),
            "a2p": lora_pad(rwkv_a2[l], A_LORA[0]),
            "g2p": lora_pad(rwkv_g2[l], A_LORA[0] + A_LORA[1]),
            "head_ones": (hid[:, None] == hid[None, :]).astype(BF16),
            "r_k": row(rwkv_r_k[l]), "ln_w": row(rwkv_ln_w[l]), "ln_b": row(rwkv_ln_b[l]),
        },
        "pool_w": pool_w[l], "pool_scale": pool_scale[l],
        "mlstm": {
            "conv_w": mlstm_conv_w[l], "conv_b": row(mlstm_conv_b[l]),
            "gate_bias": jnp.pad(jnp.concatenate([mlstm_i_bias[l], mlstm_f_bias[l]]),
                                 (0, LANE - 2 * D_HEADS)).reshape(1, LANE),
            "norm_w": row(mlstm_norm_w[l]),
        },
        "wg": w_merge_gate[l].astype(BF16), "wb": w_branch[l].astype(BF16), "wo": w_out[l].astype(BF16),
    }
    if l % 2 == 0:
        p["ffn"] = (ffn_w_gate[l // 2].astype(BF16), ffn_w_up[l // 2].astype(BF16),
                    ffn_w_down[l // 2].astype(BF16))
    else:
        pe = EXPERT_PAD - D_FF_EXPERT
        p["moe"] = (jnp.pad(moe_router[l // 2], ((0, 0), (0, LANE - N_EXPERTS))),
                    jnp.pad(moe_w_gate[l // 2], ((0, 0), (0, 0), (0, pe))).astype(BF16),
                    jnp.pad(moe_w_up[l // 2], ((0, 0), (0, 0), (0, pe))).astype(BF16),
                    jnp.pad(moe_w_down[l // 2], ((0, 0), (0, pe), (0, 0))).astype(BF16))
    return p


def _slot(u3, name):
    s = SLOT[name] * W
    return u3[:, :, s:s + W]


def _run_trunk(x, start_pos, cache_k, cache_v, wkv0, shift0, pool0, conv0, c0, n0, m0,
               layers, norm_final):
    bn, t, _ = x.shape
    n = bn * t
    tm_tok = min(256, t)
    tm_row = min(512, n)
    x2 = x.reshape(n, D_MODEL)
    ks, vs, wkvs, shifts, pools, convs, cs, ns, ms = ([] for _ in range(9))
    for l, p in enumerate(layers):
        u2 = _inproj(x2, p["norm_mix"], p["wp"])
        u3 = u2.reshape(bn, t, WP)

        out_a, wkv = _rwkv(u3, shift0[l], wkv0[l], p["rwkv"], tm_tok)

        out_b = _pool(u3, pool0[l], p["pool_w"], p["pool_scale"], start_pos, tm_tok)

        k_c, v_c = _slot(u3, "ck"), _slot(u3, "cv")
        q_b, k_b, v_b = _to_bf16(u3, (SLOT["cq"], SLOT["ck"], SLOT["cv"]), min(512, t))
        if cache_k is None:
            out_c = _sb_attention(q_b, k_b, v_b, 0, min(SB_TQ, t))
        else:
            past = cache_k.shape[2]
            (kc_b,) = _to_bf16(cache_k[l].reshape(bn, past, W), (0,), min(512, past))
            (vc_b,) = _to_bf16(cache_v[l].reshape(bn, past, W), (0,), min(512, past))
            out_c = _sb_attention(q_b, jnp.concatenate([kc_b, k_b], axis=1),
                                  jnp.concatenate([vc_b, v_b], axis=1), past, t)

        out_d, conv_new, c_new, n_new, m_new = _mlstm(u3, conv0[l], c0[l], n0[l], m0[l], p["mlstm"], tm_tok)

        x2 = _merge(x2.reshape(bn, t, D_MODEL), u3, out_a, out_b, out_c, out_d,
                    p["wg"], p["wb"], p["wo"], tm_tok).reshape(n, D_MODEL)
        last = l == len(layers) - 1
        if "ffn" in p:
            x2 = _ffn_dense(x2, p["norm_ffn"], *p["ffn"], norm_final, last, tm_row)
        else:
            x2 = _ffn_moe(x2, p["norm_ffn"], *p["moe"], norm_final, last, tm_row)

        ks.append(k_c.reshape(bn, t, C_HEADS, C_HEAD))
        vs.append(v_c.reshape(bn, t, C_HEADS, C_HEAD))
        wkvs.append(wkv)
        u_last = u3[:, -1]
        shifts.append(jnp.concatenate(
            [u_last[:, SLOT["ar"] * W:SLOT["ar"] * W + 3 * W], u_last[:, COL_LORA:COL_LORA + sum(A_LORA)]], axis=1))
        pools.append(jnp.concatenate([pool0[l], _slot(u3, "b")[:, -B_HIST:]], axis=1)[:, -B_HIST:])
        convs.append(conv_new)
        cs.append(c_new)
        ns.append(n_new)
        ms.append(m_new)
    st = lambda lst: jnp.stack(lst, axis=0)
    return (x2.reshape(bn, t, D_MODEL),
            (st(ks), st(vs), st(wkvs), st(shifts), st(pools), st(convs), st(cs), st(ns), st(ms)))


def kernel(x_prompt, x_sample, cache_sb_k, cache_sb_v, state_rwkv_wkv, state_rwkv_shift, state_pool,
           state_mlstm_conv, state_mlstm_c, state_mlstm_n, state_mlstm_m, norm_mix, norm_ffn, norm_final,
           w_in, rwkv_mu, rwkv_w0, rwkv_w2, rwkv_a0, rwkv_a2, rwkv_g2, rwkv_k_k, rwkv_k_a, rwkv_r_k,
           rwkv_ln_w, rwkv_ln_b, pool_w, pool_scale, mlstm_conv_w, mlstm_conv_b, mlstm_i_bias,
           mlstm_f_bias, mlstm_norm_w, w_branch, w_merge_gate, w_out, ffn_w_gate, ffn_w_up, ffn_w_down,
           moe_router, moe_w_gate, moe_w_up, moe_w_down):
    wts = (norm_mix, norm_ffn, norm_final, w_in, rwkv_mu, rwkv_w0, rwkv_w2, rwkv_a0, rwkv_a2,
           rwkv_g2, rwkv_k_k, rwkv_k_a, rwkv_r_k, rwkv_ln_w, rwkv_ln_b, pool_w, pool_scale,
           mlstm_conv_w, mlstm_conv_b, mlstm_i_bias, mlstm_f_bias, mlstm_norm_w,
           w_branch, w_merge_gate, w_out, ffn_w_gate, ffn_w_up, ffn_w_down,
           moe_router, moe_w_gate, moe_w_up, moe_w_down)
    depth = w_in.shape[0]
    layers = [_layer_params(l, wts) for l in range(depth)]
    bp = x_prompt.shape[0]
    z = lambda *s: jnp.zeros((depth, bp) + s, F32)
    y_p, st_p = _run_trunk(
        x_prompt, 0, None, None, z(A_HEADS, A_HEAD, A_HEAD), z(A_IN), z(B_HIST, W),
        z(D_CONV - 1, 2 * W), z(D_HEADS, D_HEAD, D_HEAD), z(D_HEADS, D_HEAD), z(D_HEADS),
        layers, norm_final)
    y_s, st_s = _run_trunk(
        x_sample, cache_sb_k.shape[2], cache_sb_k, cache_sb_v, state_rwkv_wkv, state_rwkv_shift,
        state_pool, state_mlstm_conv, state_mlstm_c, state_mlstm_n, state_mlstm_m,
        layers, norm_final)
    return (y_p, y_s) + tuple(st_p) + tuple(st_s)
```

```python
import functools

import jax
import jax.numpy as jnp
from jax import lax
from jax.experimental import pallas as pl
from jax.experimental.pallas import tpu as pltpu

F32 = jnp.float32
BF16 = jnp.bfloat16

D_MODEL = 2048
DEPTH = 2
W = 512
A_HEAD, A_HEADS = 64, 8
A_LORA = (32, 32, 96)
A_IN = 3 * W + sum(A_LORA)
A_GN_EPS = 64e-5
B_WINDOWS = (2, 4, 8, 16)
B_HIST = 15
C_HEAD, C_HEADS = 64, 8
D_HEADS, D_HEAD = 4, 128
D_CONV = 4
D_NORM_EPS = 1e-6
GATE_RANK = 256
OFF_B = A_IN
OFF_C = OFF_B + W
OFF_D = OFF_C + 3 * W
OFF_G = OFF_D + 4 * W + 2 * D_HEADS
IN_WIDTH = OFF_G + GATE_RANK
D_FF = 5632
N_EXPERTS = 8
D_FF_EXPERT = D_FF // N_EXPERTS
NORM_EPS = 1e-6

LANE = 128
V7X_VMEM_BYTES = 64 * 1024 * 1024
VMEM_LIMIT = V7X_VMEM_BYTES - 8 * 1024 * 1024

SLOT = {name: i for i, name in enumerate(
    ("dq", "dk", "dv", "do", "ar", "ak", "av", "b", "cq", "ck", "cv"))}
COL_LORA = 11 * W
COL_G = COL_LORA + 256
COL_DG = COL_G + 256
IN_TN = 1280
IN_TM = 1024
WP = 6400
LORA_PAD = 256
EXPERT_PAD = 768

RWKV_CHUNK = 64
MLSTM_CHUNK = 128
SB_BLOCK = 128
SB_TQ = 256
SB_DEAD = -120.0


def _params(sem, vmem=VMEM_LIMIT):
    return pltpu.CompilerParams(dimension_semantics=sem, vmem_limit_bytes=vmem)


def _split2(x):
    hi = x.astype(BF16)
    lo = (x - hi.astype(F32)).astype(BF16)
    return hi, lo


def _split3(x):
    hi = x.astype(BF16)
    r1 = x - hi.astype(F32)
    mid = r1.astype(BF16)
    lo = (r1 - mid.astype(F32)).astype(BF16)
    return hi, mid, lo


def _dot(a, b):
    return jnp.dot(a, b, preferred_element_type=F32)


def _dot_nt(a, b):
    return lax.dot_general(a, b, (((1,), (1,)), ((), ())), preferred_element_type=F32)


def _bdot(a, b):
    return lax.dot_general(a, b, (((2,), (1,)), ((0,), (0,))), preferred_element_type=F32)


def _bdot_nt(a, b):
    return lax.dot_general(a, b, (((2,), (2,)), ((0,), (0,))), preferred_element_type=F32)


def _bdot_tn(a, b):
    return lax.dot_general(a, b, (((1,), (1,)), ((0,), (0,))), preferred_element_type=F32)


def _dot_lhs01(m01, x):
    hi, mid, lo = _split3(x)
    return _dot(m01, hi) + _dot(m01, mid) + _dot(m01, lo)


def _sigmoid(x):
    return 1.0 / (1.0 + jnp.exp(-x))


def _log_sigmoid(x):
    return jnp.minimum(x, 0.0) - jnp.log1p(jnp.exp(-jnp.abs(x)))


def _rmsnorm(x, g):
    ms = jnp.mean(x * x, axis=-1, keepdims=True)
    return x * lax.rsqrt(ms + NORM_EPS) * g


def _inproj_body(x_ref, g_ref, w_ref, o_ref, h_scr):
    @pl.when(pl.program_id(1) == 0)
    def _():
        h_scr[...] = _rmsnorm(x_ref[...], g_ref[...]).astype(BF16)

    o_ref[...] = _dot(h_scr[...], w_ref[...])


def _inproj(x2d, g, wp):
    n = x2d.shape[0]
    tm = min(IN_TM, n)
    return pl.pallas_call(
        _inproj_body,
        grid=(n // tm, WP // IN_TN),
        in_specs=[pl.BlockSpec((tm, D_MODEL), lambda i, j: (i, 0)),
                  pl.BlockSpec((1, D_MODEL), lambda i, j: (0, 0)),
                  pl.BlockSpec((D_MODEL, IN_TN), lambda i, j: (0, j))],
        out_specs=pl.BlockSpec((tm, IN_TN), lambda i, j: (i, j)),
        out_shape=jax.ShapeDtypeStruct((n, WP), F32),
        scratch_shapes=[pltpu.VMEM((tm, D_MODEL), BF16)],
        compiler_params=_params(("parallel", "arbitrary")),
        name="inproj",
    )(x2d, g.reshape(1, D_MODEL), wp)


def _shift_rows(u, carry_row):
    rolled = pltpu.roll(u, 1, axis=0)
    row = lax.broadcasted_iota(jnp.int32, u.shape, 0)
    return jnp.where(row == 0, carry_row, rolled)


def _rwkv_prep(ur_ref, uk_ref, uv_ref, ul_ref, sr_ref, sk_ref, sv_ref, sl_ref,
               mu_ref, mul_ref, w0_ref, a0_ref, kk_ref, ka_ref, w2_ref, a2_ref, g2_ref, ones_ref,
               cr, ck, cv, cl):
    tm = ur_ref.shape[0]

    @pl.when(pl.program_id(1) == 0)
    def _():
        cr[0:1, :] = sr_ref[...]
        ck[0:1, :] = sk_ref[...]
        cv[0:1, :] = sv_ref[...]
        cl[0:1, :] = sl_ref[...]

    def mix(u_ref, c_ref, mu):
        u = u_ref[...]
        prev = _shift_rows(u, c_ref[0:1, :])
        c_ref[0:1, :] = u[tm - 1:tm, :]
        return u + (prev - u) * mu

    r = mix(ur_ref, cr, mu_ref[0:1, :])
    k = mix(uk_ref, ck, mu_ref[1:2, :])
    v = mix(uv_ref, cv, mu_ref[2:3, :])
    xl = mix(ul_ref, cl, mul_ref[...])

    w_lora = _dot(jnp.tanh(xl).astype(BF16), w2_ref[...])
    a_lora = _dot(xl.astype(BF16), a2_ref[...])
    g = _dot(_sigmoid(xl).astype(BF16), g2_ref[...])

    y = -(w0_ref[...] + w_lora)
    w_log = -(jnp.maximum(y, 0.0) + jnp.log1p(jnp.exp(-jnp.abs(y)))) - 0.5
    log_decay = -jnp.exp(w_log)
    a = _sigmoid(a0_ref[...] + a_lora)

    kk = k * kk_ref[...]
    hi, lo = _split2(kk * kk)
    ss = _dot(hi, ones_ref[...]) + _dot(lo, ones_ref[...])
    kk = kk / jnp.maximum(jnp.sqrt(ss), 1e-12)
    k2 = k * (1.0 + (a - 1.0) * ka_ref[...])

    return r, k2, v, kk, kk * a, log_decay, g


def _rwkv_chunks(r_ref, k_ref, v_ref, kk_ref, b_ref, lw_ref, y_ref, s_scr, n_chunks):
    c_len = RWKV_CHUNK
    hh = A_HEADS
    nb = hh * n_chunks
    row = lax.broadcasted_iota(jnp.int32, (nb, c_len, c_len), 1)
    col = lax.broadcasted_iota(jnp.int32, (nb, c_len, c_len), 2)
    tri_incl = (row >= col).astype(BF16)
    lower = row >= col
    strict = row > col
    eye = (row == col).astype(F32)

    load = lambda ref: ref[...].reshape(nb, c_len, A_HEAD)
    r, k, v, kk, b, logw = (load(ref) for ref in (r_ref, k_ref, v_ref, kk_ref, b_ref, lw_ref))
    w_hi, w_mid, w_lo = _split3(logw)
    lw = _bdot(tri_incl, w_hi) + _bdot(tri_incl, w_mid) + _bdot(tri_incl, w_lo)
    lw_prev = lw - logw
    lw_last = lw[:, c_len - 1:c_len, :]
    e_neg = jnp.exp(-lw)
    e_end = jnp.exp(lw_last - lw)
    kkm = (kk * jnp.exp(lw_prev)).astype(BF16)
    rm = r * jnp.exp(lw)
    kp = (k * e_neg).astype(BF16)
    bp = (b * e_neg).astype(BF16)
    kpp = (k * e_end).astype(BF16)
    bpp = (b * e_end).astype(BF16)
    vb = v.astype(BF16)
    rmb = rm.astype(BF16)

    a_vk = jnp.where(strict, _bdot_nt(kkm, kp), 0.0)
    a_pb = jnp.where(strict, _bdot_nt(kkm, bp), 0.0)
    rk = jnp.where(lower, _bdot_nt(rmb, kp), 0.0)
    rb = jnp.where(lower, _bdot_nt(rmb, bp), 0.0)

    n_pow = -a_pb
    t_inv = eye + n_pow
    for _ in range(5):
        npb = n_pow.astype(BF16)
        n_pow = _bdot(npb, npb)
        t_inv = t_inv + _bdot(t_inv.astype(BF16), n_pow.astype(BF16))
    tb = t_inv.astype(BF16)

    kktb = _bdot(tb, kkm).astype(BF16)
    pvb = _bdot(tb, _bdot(a_vk.astype(BF16), vb).astype(BF16)).astype(BF16)
    rbb = rb.astype(BF16)
    per_chunk = lambda x: x.reshape((hh, n_chunks) + x.shape[1:])
    m_mat = per_chunk(eye * jnp.exp(lw_last) - _bdot_tn(kktb, bpp))
    n_mat = per_chunk(_bdot_tn(vb, kpp) - _bdot_tn(pvb, bpp))
    q_mat = per_chunk((rm - _bdot(rbb, kktb)).astype(BF16))
    y0 = per_chunk(_bdot(rk.astype(BF16), vb) - _bdot(rbb, pvb))

    s = s_scr[...]
    for g in range(n_chunks):
        s_hi, s_lo = _split2(s)
        m_hi, m_lo = _split2(m_mat[:, g])
        y_ref[:, g * c_len:(g + 1) * c_len, :] = _bdot_nt(q_mat[:, g], s_hi) + y0[:, g]
        s = _bdot(s_hi, m_hi) + _bdot(s_lo, m_hi) + _bdot(s_hi, m_lo) + n_mat[:, g]
    s_scr[...] = s


def _rwkv_post(y, r, k, v, g, r_k, ln_w, ln_b, ones):
    def head_sum(x):
        hi, lo = _split2(x)
        return _dot(hi, ones) + _dot(lo, ones)

    mu = head_sum(y) * (1.0 / A_HEAD)
    yc = y - mu
    var = head_sum(yc * yc) * (1.0 / A_HEAD)
    yn = yc * lax.rsqrt(var + A_GN_EPS) * ln_w + ln_b
    bonus = head_sum(r * k * r_k) * v
    return ((yn + bonus) * g).astype(BF16)


def _rwkv_body(ur_ref, uk_ref, uv_ref, ul_ref, sr_ref, sk_ref, sv_ref, sl_ref,
               mu_ref, mul_ref, w0_ref, a0_ref, kk_ref, ka_ref, w2_ref, a2_ref, g2_ref, ones_ref,
               rk_ref, lnw_ref, lnb_ref, s0_ref,
               o_ref, sout_ref,
               cr, ck, cv, cl, s_scr, r_hm, k_hm, v_hm, kk_hm, b_hm, lw_hm, y_hm, *, n_chunks):
    @pl.when(pl.program_id(1) == 0)
    def _():
        s_scr[...] = s0_ref[...]

    r, k, v, kk, b, lw, g = _rwkv_prep(
        ur_ref, uk_ref, uv_ref, ul_ref, sr_ref, sk_ref, sv_ref, sl_ref, mu_ref, mul_ref, w0_ref, a0_ref,
        kk_ref, ka_ref, w2_ref, a2_ref, g2_ref, ones_ref, cr, ck, cv, cl)
    for x, ref in ((r, r_hm), (k, k_hm), (v, v_hm), (kk, kk_hm), (b, b_hm), (lw, lw_hm)):
        for h in range(A_HEADS):
            ref[h] = x[:, h * A_HEAD:(h + 1) * A_HEAD]
    _rwkv_chunks(r_hm, k_hm, v_hm, kk_hm, b_hm, lw_hm, y_hm, s_scr, n_chunks)
    sout_ref[...] = s_scr[...]
    y = jnp.concatenate([y_hm[h] for h in range(A_HEADS)], axis=1)
    o_ref[...] = _rwkv_post(y, r, k, v, g, rk_ref[...], lnw_ref[...], lnb_ref[...], ones_ref[...])


def _rwkv(u3, shift_prev, s0, p, tm):
    bn, t, _ = u3.shape
    sr = shift_prev[:, None, 0:W]
    sk = shift_prev[:, None, W:2 * W]
    sv = shift_prev[:, None, 2 * W:3 * W]
    sl = jnp.pad(shift_prev[:, None, 3 * W:], ((0, 0), (0, 0), (0, LORA_PAD - sum(A_LORA))))
    row = lambda c: pl.BlockSpec((None, 1, c), lambda b, i: (b, 0, 0))
    full = lambda a: pl.BlockSpec(a.shape, lambda b, i: (0,) * a.ndim)
    ublk = lambda slot: pl.BlockSpec((None, tm, W), lambda b, i, s=slot: (b, i, s))
    consts = (p["mu_rkv"], p["mu_l"], p["w0"], p["a0"], p["k_k"], p["k_a"],
              p["w2p"], p["a2p"], p["g2p"], p["head_ones"], p["r_k"], p["ln_w"], p["ln_b"])
    sblk = pl.BlockSpec((None, A_HEADS, A_HEAD, A_HEAD), lambda b, i: (b, 0, 0, 0))
    head_major = pltpu.VMEM((A_HEADS, tm, A_HEAD), F32)
    return pl.pallas_call(
        functools.partial(_rwkv_body, n_chunks=tm // RWKV_CHUNK),
        grid=(bn, t // tm),
        in_specs=[ublk(SLOT["ar"]), ublk(SLOT["ak"]), ublk(SLOT["av"]),
                  pl.BlockSpec((None, tm, LORA_PAD), lambda b, i: (b, i, COL_LORA // LORA_PAD)),
                  row(W), row(W), row(W), row(LORA_PAD)] + [full(c) for c in consts] + [sblk],
        out_specs=[pl.BlockSpec((None, tm, W), lambda b, i: (b, i, 0)), sblk],
        out_shape=[jax.ShapeDtypeStruct((bn, t, W), BF16),
                   jax.ShapeDtypeStruct((bn, A_HEADS, A_HEAD, A_HEAD), F32)],
        scratch_shapes=[pltpu.VMEM((8, W), F32)] * 3
                       + [pltpu.VMEM((8, LORA_PAD), F32), pltpu.VMEM((A_HEADS, A_HEAD, A_HEAD), F32)]
                       + [head_major] * 7,
        compiler_params=_params(("parallel", "arbitrary")),
        name="rwkv",
    )(u3, u3, u3, u3, sr, sk, sv, sl, *consts, s0)


def _pool_body(u_ref, hist_ref, w_ref, scale_ref, o_ref, carry, *, start_pos):
    tm = u_ref.shape[0]
    i = pl.program_id(1)

    @pl.when(i == 0)
    def _():
        carry[...] = hist_ref[...]

    x = u_ref[...]
    ext = jnp.concatenate([carry[...], x], axis=0)
    carry[...] = x[tm - 16:tm, :]
    s = ext
    sums = []
    for sh in (1, 2, 4, 8):
        s = s + pltpu.roll(s, sh, axis=0)
        sums.append(s[16:16 + tm, :])
    pos = start_pos + i * tm + lax.broadcasted_iota(jnp.int32, (tm, 1), 0)
    for gi, wlen in enumerate(B_WINDOWS):
        cs = slice(gi * LANE, (gi + 1) * LANE)
        cnt = jnp.minimum(wlen, pos + 1).astype(F32)
        d = sums[gi][:, cs] / cnt - x[:, cs]
        o_ref[:, cs] = (_dot(d.astype(BF16), w_ref[gi]) * scale_ref[:, cs]).astype(BF16)


def _pool(u3, hist, pool_w, pool_scale, start_pos, tm):
    bn, t, _ = u3.shape
    hist16 = jnp.pad(hist, ((0, 0), (1, 0), (0, 0)))
    return pl.pallas_call(
        functools.partial(_pool_body, start_pos=start_pos),
        grid=(bn, t // tm),
        in_specs=[pl.BlockSpec((None, tm, W), lambda b, i: (b, i, SLOT["b"])),
                  pl.BlockSpec((None, 16, W), lambda b, i: (b, 0, 0)),
                  pl.BlockSpec((4, LANE, LANE), lambda b, i: (0, 0, 0)),
                  pl.BlockSpec((1, W), lambda b, i: (0, 0))],
        out_specs=pl.BlockSpec((None, tm, W), lambda b, i: (b, i, 0)),
        out_shape=jax.ShapeDtypeStruct((bn, t, W), BF16),
        scratch_shapes=[pltpu.VMEM((16, W), F32)],
        compiler_params=_params(("parallel", "arbitrary")),
        name="pool",
    )(u3, hist16, pool_w.astype(BF16), pool_scale.reshape(1, W))


def _sb_body(q_ref, k_ref, v_ref, o_ref, acc_scr, o_scr, *, tq, q_off, n_masked):
    qi = pl.program_id(1)
    n_pairs = W // LANE
    q = q_ref[...] * (C_HEAD ** -0.5)
    q_start = q_off + qi * tq
    first_masked = q_start // SB_BLOCK
    even = lax.broadcasted_iota(jnp.int32, (tq, LANE), 1) < C_HEAD
    qz = jnp.zeros((tq, LANE), BF16)
    qs = jnp.stack([jnp.stack([jnp.where(even, q[:, p * LANE:(p + 1) * LANE], qz),
                               jnp.where(even, qz, q[:, p * LANE:(p + 1) * LANE])])
                    for p in range(n_pairs)])

    jj = lax.broadcasted_iota(jnp.int32, (2 * SB_BLOCK, 2 * SB_BLOCK), 0) % SB_BLOCK
    ss = lax.broadcasted_iota(jnp.int32, (2 * SB_BLOCK, 2 * SB_BLOCK), 1)
    cs = jnp.where(jnp.logical_or(ss >= SB_BLOCK, jj > ss), -1.0, 0.0).astype(BF16)
    aligned = q_off % SB_BLOCK == 0 and tq % SB_BLOCK == 0

    def block(kb, r0, masked):
        n = tq - r0
        ks = pl.multiple_of(kb * SB_BLOCK, SB_BLOCK)
        kblk = k_ref[pl.ds(ks, SB_BLOCK), :]
        vblk = v_ref[pl.ds(ks, SB_BLOCK), :]
        kst = jnp.stack([kblk[:, p * LANE:(p + 1) * LANE] for p in range(n_pairs)])
        vst = jnp.stack([vblk[:, p * LANE:(p + 1) * LANE] for p in range(n_pairs)])
        z = _bdot_nt(qs[:, :, r0:, :].reshape(n_pairs, 2 * n, LANE), kst).reshape(2 * n_pairs, n, SB_BLOCK)
        zp = jnp.maximum(z, 0.0)
        zn = jnp.minimum(z, 0.0)
        l1p = jnp.log(1.0 + jnp.exp(zn - zp))
        nlf = zp + l1p
        s = zn - l1p
        if masked:
            q_pos = q_start + r0 + lax.broadcasted_iota(jnp.int32, (n, SB_BLOCK), 0)
            mask = ((ks + lax.broadcasted_iota(jnp.int32, (n, SB_BLOCK), 1)) < q_pos)[None]
            nlf = jnp.where(mask, nlf, 0.0)
        hi, lo = _split2(nlf)
        lt = _dot(jnp.concatenate([hi, lo], axis=2).reshape(2 * n_pairs * n, 2 * SB_BLOCK), cs)
        lt = lt.reshape(2 * n_pairs, n, 2 * SB_BLOCK)
        att = jnp.exp(s + lt[:, :, :SB_BLOCK] + acc_scr[:, r0:, :])
        if masked:
            att = jnp.where(mask, att, 0.0)
        pv = _bdot(att.astype(BF16).reshape(n_pairs, 2 * n, SB_BLOCK), vst)
        o_scr[:, r0:, :] += pv.reshape(2 * n_pairs, n, LANE)
        acc_scr[:, r0:, :] += lt[:, :, SB_BLOCK:]

    acc_scr[...] = jnp.zeros_like(acc_scr)
    o_scr[...] = jnp.zeros_like(o_scr)
    for m in range(n_masked - 1, -1, -1):
        block(first_masked + m, m * SB_BLOCK if aligned else 0, True)

    def live(i):
        return jnp.logical_and(i < first_masked, jnp.max(acc_scr[...]) > SB_DEAD)

    def step(i):
        block(first_masked - 1 - i, 0, False)
        return i + 1

    lax.while_loop(live, step, jnp.int32(0))
    o = o_scr[...]
    o_ref[...] = jnp.concatenate([jnp.where(even, o[2 * p], o[2 * p + 1]) for p in range(n_pairs)],
                                 axis=1).astype(BF16)


def _to_bf16_body(*refs, n, n_f32):
    for x_ref, o_ref in zip(refs[:n], refs[n:2 * n]):
        o_ref[...] = x_ref[...].astype(BF16)
    for x_ref, o_ref in zip(refs[n - n_f32:n], refs[2 * n:]):
        o_ref[...] = x_ref[...]


def _to_bf16(x3, slots, tm, n_f32=0):
    bn, t, _ = x3.shape
    n = len(slots)
    oblk = pl.BlockSpec((None, tm, W), lambda b, i: (b, i, 0))
    return pl.pallas_call(
        functools.partial(_to_bf16_body, n=n, n_f32=n_f32),
        grid=(bn, t // tm),
        in_specs=[pl.BlockSpec((None, tm, W), lambda b, i, s=s: (b, i, s)) for s in slots],
        out_specs=[oblk] * (n + n_f32),
        out_shape=[jax.ShapeDtypeStruct((bn, t, W), BF16)] * n + [jax.ShapeDtypeStruct((bn, t, W), F32)] * n_f32,
        compiler_params=_params(("parallel", "parallel")),
        name="to_bf16",
    )(*([x3] * n))


def _sb_attention(q, k, v, q_off, tq):
    bn, t, _ = q.shape
    sk = k.shape[1]
    sk_pad = -(-sk // SB_BLOCK) * SB_BLOCK
    if sk_pad != sk:
        pad = ((0, 0), (0, sk_pad - sk), (0, 0))
        k, v = jnp.pad(k, pad), jnp.pad(v, pad)
    assert tq % SB_BLOCK == 0 or t == tq
    n_masked = -(-((q_off % SB_BLOCK) + tq) // SB_BLOCK)
    kv = pl.BlockSpec((None, sk_pad, W), lambda b, i: (b, 0, 0), pipeline_mode=pl.Buffered(1))
    qo = pl.BlockSpec((None, tq, W), lambda b, i: (b, i, 0))
    return pl.pallas_call(
        functools.partial(_sb_body, tq=tq, q_off=q_off, n_masked=n_masked),
        grid=(bn, t // tq),
        in_specs=[qo, kv, kv],
        out_specs=qo,
        out_shape=jax.ShapeDtypeStruct((bn, t, W), BF16),
        scratch_shapes=[pltpu.VMEM((C_HEADS, tq, SB_BLOCK), F32), pltpu.VMEM((C_HEADS, tq, LANE), F32)],
        compiler_params=_params(("parallel", "arbitrary")),
        name="sb_attention",
    )(q, k, v)


def _mlstm_body(qk_ref, v_ref, o_ref, g_ref, hist_ref, c0_ref, n0_ref, m0_ref,
                cw_ref, cb_ref, gb_ref, nw_ref,
                out_ref, conv_out, c_out, n_out, m_out,
                carry, q_scr, k_scr, c_scr, n_scr, m_scr, *, n_chunks):
    tm = qk_ref.shape[0]
    cl = tm // n_chunks

    @pl.when(pl.program_id(1) == 0)
    def _():
        carry[...] = hist_ref[...]
        c_scr[...] = c0_ref[...]
        n_scr[...] = n0_ref[...]
        m_scr[...] = m0_ref[...]

    x = qk_ref[...]
    ext = jnp.concatenate([carry[...], x], axis=0)
    carry[...] = x[tm - 8:tm, :]
    conv = cb_ref[...] + x * cw_ref[D_CONV - 1:D_CONV, :]
    for sh in range(1, D_CONV):
        conv = conv + pltpu.roll(ext, sh, axis=0)[8:8 + tm, :] * cw_ref[D_CONV - 1 - sh:D_CONV - sh, :]
    conv = conv * _sigmoid(conv)
    q_scr[...] = conv[:, :W]
    k_scr[...] = conv[:, W:] * (D_HEAD ** -0.5)

    row = lax.broadcasted_iota(jnp.int32, (cl, cl), 0)
    col = lax.broadcasted_iota(jnp.int32, (cl, cl), 1)
    causal = row >= col
    tri_incl = causal.astype(BF16)

    def chunk(c, carry_):
        off = pl.multiple_of(c * cl, cl)
        sl = pl.ds(off, cl)
        gpre = g_ref[sl, :] + gb_ref[...]
        lfa = _log_sigmoid(gpre)
        bcol = _dot_lhs01(tri_incl, lfa)
        g_t = gpre.T
        b_t = bcol.T
        for h in range(D_HEADS):
            hs = slice(h * D_HEAD, (h + 1) * D_HEAD)
            q = q_scr[sl, hs]
            k = k_scr[sl, hs]
            v = v_ref[sl, hs]
            ig_col = gpre[:, h:h + 1]
            b_col = bcol[:, D_HEADS + h:D_HEADS + h + 1]
            ig_row = g_t[h:h + 1, :]
            b_row = b_t[D_HEADS + h:D_HEADS + h + 1, :]
            m_prev = m_scr[h:h + 1, 0:1]
            log_d = jnp.where(causal, b_col - b_row + ig_row, -jnp.inf)
            m_inter = b_col + m_prev
            m_t = jnp.maximum(m_inter, jnp.max(log_d, axis=-1, keepdims=True))
            dmat = jnp.exp(log_d - m_t)
            inter = jnp.exp(m_inter - m_t)
            qb = q.astype(BF16)
            vb = v.astype(BF16)
            w_qk = _dot_nt(qb, k.astype(BF16)) * dmat
            num = _dot(w_qk.astype(BF16), vb) + inter * _dot(qb, c_scr[h].astype(BF16))
            qn = jnp.sum(q * n_scr[h:h + 1, :], axis=-1, keepdims=True)
            den = jnp.sum(w_qk, axis=-1, keepdims=True) + inter * qn
            den = jnp.maximum(jnp.abs(den), jnp.exp(-m_t))
            hv = num / den
            m_new = m_t[cl - 1:cl, :]
            b_last = b_col[cl - 1:cl, :]
            decay = jnp.exp(b_last + m_prev - m_new)
            w_s = jnp.exp(b_last - b_col + ig_col - m_new)
            kw = k * w_s
            c_scr[h] = decay * c_scr[h] + _dot(kw.T.astype(BF16), vb)
            n_scr[h:h + 1, :] = decay * n_scr[h:h + 1, :] + jnp.sum(kw, axis=0, keepdims=True)
            m_scr[h:h + 1, :] = jnp.broadcast_to(m_new, (1, LANE))
            mu = jnp.mean(hv, axis=-1, keepdims=True)
            hc = hv - mu
            var = jnp.mean(hc * hc, axis=-1, keepdims=True)
            hn = hc * lax.rsqrt(var + D_NORM_EPS) * nw_ref[:, hs]
            out_ref[sl, hs] = (_sigmoid(o_ref[sl, hs]) * hn).astype(BF16)
        return carry_

    lax.fori_loop(0, n_chunks, chunk, 0)
    conv_out[...] = carry[...]
    c_out[...] = c_scr[...]
    n_out[...] = n_scr[...]
    m_out[...] = m_scr[...]


def _mlstm(u3, conv_hist, c0, n0, m0, p, tm):
    bn, t, _ = u3.shape
    hist8 = jnp.pad(conv_hist, ((0, 0), (8 - (D_CONV - 1), 0), (0, 0)))
    n0p = jnp.pad(n0, ((0, 0), (0, 8 - D_HEADS), (0, 0)))
    m0p = jnp.pad(jnp.broadcast_to(m0[:, :, None], (bn, D_HEADS, LANE)), ((0, 0), (0, 8 - D_HEADS), (0, 0)))
    full = lambda a: pl.BlockSpec(a.shape, lambda b, i: (0,) * a.ndim)
    per_b = lambda *s: pl.BlockSpec((None,) + s, lambda b, i: (b,) + (0,) * len(s))
    consts = (p["conv_w"], p["conv_b"], p["gate_bias"], p["norm_w"])
    outs = pl.pallas_call(
        functools.partial(_mlstm_body, n_chunks=tm // min(MLSTM_CHUNK, tm)),
        grid=(bn, t // tm),
        in_specs=[pl.BlockSpec((None, tm, 2 * W), lambda b, i: (b, i, 0)),
                  pl.BlockSpec((None, tm, W), lambda b, i: (b, i, SLOT["dv"])),
                  pl.BlockSpec((None, tm, W), lambda b, i: (b, i, SLOT["do"])),
                  pl.BlockSpec((None, tm, LANE), lambda b, i: (b, i, COL_DG // LANE)),
                  per_b(8, 2 * W), per_b(D_HEADS, D_HEAD, D_HEAD), per_b(8, LANE), per_b(8, LANE)]
                 + [full(c) for c in consts],
        out_specs=[pl.BlockSpec((None, tm, W), lambda b, i: (b, i, 0)),
                   per_b(8, 2 * W), per_b(D_HEADS, D_HEAD, D_HEAD), per_b(8, LANE), per_b(8, LANE)],
        out_shape=[jax.ShapeDtypeStruct((bn, t, W), BF16),
                   jax.ShapeDtypeStruct((bn, 8, 2 * W), F32),
                   jax.ShapeDtypeStruct((bn, D_HEADS, D_HEAD, D_HEAD), F32),
                   jax.ShapeDtypeStruct((bn, 8, LANE), F32),
                   jax.ShapeDtypeStruct((bn, 8, LANE), F32)],
        scratch_shapes=[pltpu.VMEM((8, 2 * W), F32), pltpu.VMEM((tm, W), F32), pltpu.VMEM((tm, W), F32),
                        pltpu.VMEM((D_HEADS, D_HEAD, D_HEAD), F32), pltpu.VMEM((8, LANE), F32),
                        pltpu.VMEM((8, LANE), F32)],
        compiler_params=_params(("parallel", "arbitrary")),
        name="mlstm",
    )(u3, u3, u3, u3, hist8, c0, n0p, m0p, *consts)
    out_d, conv8, c, n8, m8 = outs
    return out_d, conv8[:, 8 - (D_CONV - 1):], c, n8[:, :D_HEADS], m8[:, :D_HEADS, 0]


def _merge_body(x_ref, g_ref, a_ref, b_ref, c_ref, d_ref, wg_ref, wb_ref, wo_ref, o_ref):
    g = g_ref[...].astype(BF16)
    merged = None
    for i, br in enumerate((a_ref, b_ref, c_ref, d_ref)):
        term = _sigmoid(_dot(g, wg_ref[i])) * _dot(br[...], wb_ref[i])
        merged = term if merged is None else merged + term
    o_ref[...] = x_ref[...] + _dot(merged.astype(BF16), wo_ref[...])


def _merge(x3, u3, out_a, out_b, out_c, out_d, wg, wb, wo, tm):
    bn, t, _ = x3.shape
    once = lambda a: pl.BlockSpec(a.shape, lambda b, i: (0,) * a.ndim, pipeline_mode=pl.Buffered(1))
    bblk = pl.BlockSpec((None, tm, W), lambda b, i: (b, i, 0))
    xblk = pl.BlockSpec((None, tm, D_MODEL), lambda b, i: (b, i, 0))
    return pl.pallas_call(
        _merge_body,
        grid=(bn, t // tm),
        in_specs=[xblk, pl.BlockSpec((None, tm, GATE_RANK), lambda b, i: (b, i, COL_G // GATE_RANK)),
                  bblk, bblk, bblk, bblk, once(wg), once(wb), once(wo)],
        out_specs=xblk,
        out_shape=jax.ShapeDtypeStruct((bn, t, D_MODEL), F32),
        compiler_params=_params(("parallel", "parallel")),
        name="merge",
    )(x3, u3, out_a, out_b, out_c, out_d, wg, wb, wo)


def _ffn_epilogue(acc, gf_ref, final_norm):
    return _rmsnorm(acc, gf_ref[...]) if final_norm else acc


def _ffn_dense_body(x_ref, g_ref, wg_ref, wu_ref, wd_ref, gf_ref, o_ref, h_scr, acc, *, final_norm):
    j = pl.program_id(1)

    @pl.when(j == 0)
    def _():
        x = x_ref[...]
        h_scr[...] = _rmsnorm(x, g_ref[...]).astype(BF16)
        acc[...] = x

    h = h_scr[...]
    a = _dot(h, wg_ref[...])
    act = (a * _sigmoid(a)) * _dot(h, wu_ref[...])
    acc[...] += _dot(act.astype(BF16), wd_ref[...])

    @pl.when(j == pl.num_programs(1) - 1)
    def _():
        o_ref[...] = _ffn_epilogue(acc[...], gf_ref, final_norm)


def _ffn_dense(x2d, g, wgate, wup, wdown, g_final, final_norm, tm, tf=512):
    n = x2d.shape[0]
    xblk = pl.BlockSpec((tm, D_MODEL), lambda i, j: (i, 0))
    vec = pl.BlockSpec((1, D_MODEL), lambda i, j: (0, 0))
    return pl.pallas_call(
        functools.partial(_ffn_dense_body, final_norm=final_norm),
        grid=(n // tm, D_FF // tf),
        in_specs=[xblk, vec,
                  pl.BlockSpec((D_MODEL, tf), lambda i, j: (0, j)),
                  pl.BlockSpec((D_MODEL, tf), lambda i, j: (0, j)),
                  pl.BlockSpec((tf, D_MODEL), lambda i, j: (j, 0)), vec],
        out_specs=xblk,
        out_shape=jax.ShapeDtypeStruct((n, D_MODEL), F32),
        scratch_shapes=[pltpu.VMEM((tm, D_MODEL), BF16), pltpu.VMEM((tm, D_MODEL), F32)],
        compiler_params=_params(("parallel", "arbitrary")),
        name="ffn_dense",
    )(x2d, g.reshape(1, D_MODEL), wgate, wup, wdown, g_final.reshape(1, D_MODEL))


def _ffn_moe_body(x_ref, g_ref, rt_ref, wg_ref, wu_ref, wd_ref, gf_ref, o_ref,
                  h_scr, acc, gates, *, final_norm):
    e = pl.program_id(1)

    @pl.when(e == 0)
    def _():
        x = x_ref[...]
        h = _rmsnorm(x, g_ref[...])
        h_scr[...] = h.astype(BF16)
        acc[...] = x
        h_hi, h_lo = _split2(h)
        r_hi, r_lo = _split2(rt_ref[...])
        logits = _dot(h_hi, r_hi) + _dot(h_lo, r_hi) + _dot(h_hi, r_lo)
        lane = lax.broadcasted_iota(jnp.int32, logits.shape, 1)
        logits = jnp.where(lane < N_EXPERTS, logits, -jnp.inf)
        v1 = jnp.max(logits, axis=-1, keepdims=True)
        i1 = jnp.min(jnp.where(logits == v1, lane, LANE), axis=-1, keepdims=True)
        rest = jnp.where(lane == i1, -jnp.inf, logits)
        v2 = jnp.max(rest, axis=-1, keepdims=True)
        i2 = jnp.min(jnp.where(rest == v2, lane, LANE), axis=-1, keepdims=True)
        e2 = jnp.exp(v2 - v1)
        den = 1.0 + e2
        gates[...] = jnp.where(lane == i1, 1.0 / den, 0.0) + jnp.where(lane == i2, e2 / den, 0.0)

    h = h_scr[...]
    lane = lax.broadcasted_iota(jnp.int32, gates.shape, 1)
    gate = jnp.sum(jnp.where(lane == e, gates[...], 0.0), axis=-1, keepdims=True)
    a = _dot(h, wg_ref[...])
    act = (a * _sigmoid(a)) * _dot(h, wu_ref[...])
    acc[...] += gate * _dot(act.astype(BF16), wd_ref[...])

    @pl.when(e == pl.num_programs(1) - 1)
    def _():
        o_ref[...] = _ffn_epilogue(acc[...], gf_ref, final_norm)


def _ffn_moe(x2d, g, router, wgate, wup, wdown, g_final, final_norm, tm):
    n = x2d.shape[0]
    xblk = pl.BlockSpec((tm, D_MODEL), lambda i, j: (i, 0))
    vec = pl.BlockSpec((1, D_MODEL), lambda i, j: (0, 0))
    return pl.pallas_call(
        functools.partial(_ffn_moe_body, final_norm=final_norm),
        grid=(n // tm, N_EXPERTS),
        in_specs=[xblk, vec,
                  pl.BlockSpec((D_MODEL, LANE), lambda i, j: (0, 0)),
                  pl.BlockSpec((None, D_MODEL, EXPERT_PAD), lambda i, j: (j, 0, 0)),
                  pl.BlockSpec((None, D_MODEL, EXPERT_PAD), lambda i, j: (j, 0, 0)),
                  pl.BlockSpec((None, EXPERT_PAD, D_MODEL), lambda i, j: (j, 0, 0)), vec],
        out_specs=xblk,
        out_shape=jax.ShapeDtypeStruct((n, D_MODEL), F32),
        scratch_shapes=[pltpu.VMEM((tm, D_MODEL), BF16), pltpu.VMEM((tm, D_MODEL), F32),
                        pltpu.VMEM((tm, LANE), F32)],
        compiler_params=_params(("parallel", "arbitrary")),
        name="ffn_moe",
    )(x2d, g.reshape(1, D_MODEL), router, wgate, wup, wdown, g_final.reshape(1, D_MODEL))


def _pack_w_in(w):
    a, b, c, d = w[:, :OFF_B], w[:, OFF_B:OFF_C], w[:, OFF_C:OFF_D], w[:, OFF_D:OFF_G]
    g = w[:, OFF_G:]
    zeros = lambda k: jnp.zeros((D_MODEL, k), w.dtype)
    lora = a[:, 3 * W:]
    gates = d[:, 4 * W:]
    cols = [d[:, :4 * W], a[:, :3 * W], b, c,
            lora, zeros(LORA_PAD - lora.shape[1]), g,
            gates, zeros(LANE - gates.shape[1]), zeros(WP - COL_DG - LANE)]
    return jnp.concatenate(cols, axis=1).astype(BF16)


def _layer_params(l, wts):
    (norm_mix, norm_ffn, norm_final, w_in, rwkv_mu, rwkv_w0, rwkv_w2, rwkv_a0, rwkv_a2,
     rwkv_g2, rwkv_k_k, rwkv_k_a, rwkv_r_k, rwkv_ln_w, rwkv_ln_b, pool_w, pool_scale,
     mlstm_conv_w, mlstm_conv_b, mlstm_i_bias, mlstm_f_bias, mlstm_norm_w,
     w_branch, w_merge_gate, w_out, ffn_w_gate, ffn_w_up, ffn_w_down,
     moe_router, moe_w_gate, moe_w_up, moe_w_down) = wts
    n_l = sum(A_LORA)
    row = lambda v: v.reshape(1, -1)

    def lora_pad(wm, start):
        return jnp.pad(wm, ((start, LORA_PAD - start - wm.shape[0]), (0, 0))).astype(BF16)

    hid = jnp.arange(W) // A_HEAD
    p = {
        "norm_mix": norm_mix[l], "norm_ffn": norm_ffn[l],
        "wp": _pack_w_in(w_in[l]),
        "rwkv": {
            "mu_rkv": rwkv_mu[l, :3 * W].reshape(3, W),
            "mu_l": jnp.pad(rwkv_mu[l, 3 * W:], (0, LORA_PAD - n_l)).reshape(1, LORA_PAD),
            "w0": row(rwkv_w0[l]), "a0": row(rwkv_a0[l]),
            "k_k": row(rwkv_k_k[l]), "k_a": row(rwkv_k_a[l]),
            "w2p": lora_pad(rwkv_w2[l], 0),
            "a2p": lora_pad(rwkv_a2[l], A_LORA[0]),
            "g2p": lora_pad(rwkv_g2[l], A_LORA[0] + A_LORA[1]),
            "head_ones": (hid[:, None] == hid[None, :]).astype(BF16),
            "r_k": row(rwkv_r_k[l]), "ln_w": row(rwkv_ln_w[l]), "ln_b": row(rwkv_ln_b[l]),
        },
        "pool_w": pool_w[l], "pool_scale": pool_scale[l],
        "mlstm": {
            "conv_w": mlstm_conv_w[l], "conv_b": row(mlstm_conv_b[l]),
            "gate_bias": jnp.pad(jnp.concatenate([mlstm_i_bias[l], mlstm_f_bias[l]]),
                                 (0, LANE - 2 * D_HEADS)).reshape(1, LANE),
            "norm_w": row(mlstm_norm_w[l]),
        },
        "wg": w_merge_gate[l].astype(BF16), "wb": w_branch[l].astype(BF16), "wo": w_out[l].astype(BF16),
    }
    if l % 2 == 0:
        p["ffn"] = (ffn_w_gate[l // 2].astype(BF16), ffn_w_up[l // 2].astype(BF16),
                    ffn_w_down[l // 2].astype(BF16))
    else:
        pe = EXPERT_PAD - D_FF_EXPERT
        p["moe"] = (jnp.pad(moe_router[l // 2], ((0, 0), (0, LANE - N_EXPERTS))),
                    jnp.pad(moe_w_gate[l // 2], ((0, 0), (0, 0), (0, pe))).astype(BF16),
                    jnp.pad(moe_w_up[l // 2], ((0, 0), (0, 0), (0, pe))).astype(BF16),
                    jnp.pad(moe_w_down[l // 2], ((0, 0), (0, pe), (0, 0))).astype(BF16))
    return p


def _slot(u3, name):
    s = SLOT[name] * W
    return u3[:, :, s:s + W]


def _run_trunk(x, start_pos, cache_k, cache_v, wkv0, shift0, pool0, conv0, c0, n0, m0,
               layers, norm_final):
    bn, t, _ = x.shape
    n = bn * t
    tm_tok = min(256, t)
    tm_row = min(512, n)
    x2 = x.reshape(n, D_MODEL)
    ks, vs, wkvs, shifts, pools, convs, cs, ns, ms = ([] for _ in range(9))
    for l, p in enumerate(layers):
        u2 = _inproj(x2, p["norm_mix"], p["wp"])
        u3 = u2.reshape(bn, t, WP)

        out_a, wkv = _rwkv(u3, shift0[l], wkv0[l], p["rwkv"], tm_tok)

        out_b = _pool(u3, pool0[l], p["pool_w"], p["pool_scale"], start_pos, tm_tok)

        q_b, k_b, v_b, k_c, v_c = _to_bf16(u3, (SLOT["cq"], SLOT["ck"], SLOT["cv"]), min(512, t), n_f32=2)
        if cache_k is None:
            out_c = _sb_attention(q_b, k_b, v_b, 0, min(SB_TQ, t))
        else:
            past = cache_k.shape[2]
            (kc_b,) = _to_bf16(cache_k[l].reshape(bn, past, W), (0,), min(512, past))
            (vc_b,) = _to_bf16(cache_v[l].reshape(bn, past, W), (0,), min(512, past))
            out_c = _sb_attention(q_b, jnp.concatenate([kc_b, k_b], axis=1),
                                  jnp.concatenate([vc_b, v_b], axis=1), past, t)

        out_d, conv_new, c_new, n_new, m_new = _mlstm(u3, conv0[l], c0[l], n0[l], m0[l], p["mlstm"], tm_tok)

        x2 = _merge(x2.reshape(bn, t, D_MODEL), u3, out_a, out_b, out_c, out_d,
                    p["wg"], p["wb"], p["wo"], tm_tok).reshape(n, D_MODEL)
        last = l == len(layers) - 1
        if "ffn" in p:
            x2 = _ffn_dense(x2, p["norm_ffn"], *p["ffn"], norm_final, last, tm_row)
        else:
            x2 = _ffn_moe(x2, p["norm_ffn"], *p["moe"], norm_final, last, tm_row)

        ks.append(k_c.reshape(bn, t, C_HEADS, C_HEAD))
        vs.append(v_c.reshape(bn, t, C_HEADS, C_HEAD))
        wkvs.append(wkv)
        u_last = u3[:, -1]
        shifts.append(jnp.concatenate(
            [u_last[:, SLOT["ar"] * W:SLOT["ar"] * W + 3 * W], u_last[:, COL_LORA:COL_LORA + sum(A_LORA)]], axis=1))
        pools.append(jnp.concatenate([pool0[l], _slot(u3, "b")[:, -B_HIST:]], axis=1)[:, -B_HIST:])
        convs.append(conv_new)
        cs.append(c_new)
        ns.append(n_new)
        ms.append(m_new)
    st = lambda lst: jnp.stack(lst, axis=0)
    return (x2.reshape(bn, t, D_MODEL),
            (st(ks), st(vs), st(wkvs), st(shifts), st(pools), st(convs), st(cs), st(ns), st(ms)))


def kernel(x_prompt, x_sample, cache_sb_k, cache_sb_v, state_rwkv_wkv, state_rwkv_shift, state_pool,
           state_mlstm_conv, state_mlstm_c, state_mlstm_n, state_mlstm_m, norm_mix, norm_ffn, norm_final,
           w_in, rwkv_mu, rwkv_w0, rwkv_w2, rwkv_a0, rwkv_a2, rwkv_g2, rwkv_k_k, rwkv_k_a, rwkv_r_k,
           rwkv_ln_w, rwkv_ln_b, pool_w, pool_scale, mlstm_conv_w, mlstm_conv_b, mlstm_i_bias,
           mlstm_f_bias, mlstm_norm_w, w_branch, w_merge_gate, w_out, ffn_w_gate, ffn_w_up, ffn_w_down,
           moe_router, moe_w_gate, moe_w_up, moe_w_down):
    wts = (norm_mix, norm_ffn, norm_final, w_in, rwkv_mu, rwkv_w0, rwkv_w2, rwkv_a0, rwkv_a2,
           rwkv_g2, rwkv_k_k, rwkv_k_a, rwkv_r_k, rwkv_ln_w, rwkv_ln_b, pool_w, pool_scale,
           mlstm_conv_w, mlstm_conv_b, mlstm_i_bias, mlstm_f_bias, mlstm_norm_w,
           w_branch, w_merge_gate, w_out, ffn_w_gate, ffn_w_up, ffn_w_down,
           moe_router, moe_w_gate, moe_w_up, moe_w_down)
    depth = w_in.shape[0]
    layers = [_layer_params(l, wts) for l in range(depth)]
    bp = x_prompt.shape[0]
    z = lambda *s: jnp.zeros((depth, bp) + s, F32)
    y_p, st_p = _run_trunk(
        x_prompt, 0, None, None, z(A_HEADS, A_HEAD, A_HEAD), z(A_IN), z(B_HIST, W),
        z(D_CONV - 1, 2 * W), z(D_HEADS, D_HEAD, D_HEAD), z(D_HEADS, D_HEAD), z(D_HEADS),
        layers, norm_final)
    y_s, st_s = _run_trunk(
        x_sample, cache_sb_k.shape[2], cache_sb_k, cache_sb_v, state_rwkv_wkv, state_rwkv_shift,
        state_pool, state_mlstm_conv, state_mlstm_c, state_mlstm_n, state_mlstm_m,
        layers, norm_final)
    return (y_p, y_s) + tuple(st_p) + tuple(st_s)
```

```python
import functools

import jax
import jax.numpy as jnp
from jax import lax
from jax.experimental import pallas as pl
from jax.experimental.pallas import tpu as pltpu

F32 = jnp.float32
BF16 = jnp.bfloat16

D_MODEL = 2048
DEPTH = 2
W = 512
A_HEAD, A_HEADS = 64, 8
A_LORA = (32, 32, 96)
A_IN = 3 * W + sum(A_LORA)
A_GN_EPS = 64e-5
B_WINDOWS = (2, 4, 8, 16)
B_HIST = 15
C_HEAD, C_HEADS = 64, 8
D_HEADS, D_HEAD = 4, 128
D_CONV = 4
D_NORM_EPS = 1e-6
GATE_RANK = 256
OFF_B = A_IN
OFF_C = OFF_B + W
OFF_D = OFF_C + 3 * W
OFF_G = OFF_D + 4 * W + 2 * D_HEADS
IN_WIDTH = OFF_G + GATE_RANK
D_FF = 5632
N_EXPERTS = 8
D_FF_EXPERT = D_FF // N_EXPERTS
NORM_EPS = 1e-6

LANE = 128
V7X_VMEM_BYTES = 64 * 1024 * 1024
VMEM_LIMIT = V7X_VMEM_BYTES - 8 * 1024 * 1024

SLOT = {name: i for i, name in enumerate(
    ("dq", "dk", "dv", "do", "ar", "ak", "av", "b", "cq", "ck", "cv"))}
COL_LORA = 11 * W
COL_G = COL_LORA + 256
COL_DG = COL_G + 256
IN_TN = 1280
IN_TM = 1024
WP = 6400
LORA_PAD = 256
EXPERT_PAD = 768

RWKV_CHUNK = 64
MLSTM_CHUNK = 128
SB_BLOCK = 128
SB_TQ = 256
SB_DEAD = -120.0


def _params(sem, vmem=VMEM_LIMIT):
    return pltpu.CompilerParams(dimension_semantics=sem, vmem_limit_bytes=vmem)


def _split2(x):
    hi = x.astype(BF16)
    lo = (x - hi.astype(F32)).astype(BF16)
    return hi, lo


def _split3(x):
    hi = x.astype(BF16)
    r1 = x - hi.astype(F32)
    mid = r1.astype(BF16)
    lo = (r1 - mid.astype(F32)).astype(BF16)
    return hi, mid, lo


def _dot(a, b):
    return jnp.dot(a, b, preferred_element_type=F32)


def _dot_nt(a, b):
    return lax.dot_general(a, b, (((1,), (1,)), ((), ())), preferred_element_type=F32)


def _bdot(a, b):
    return lax.dot_general(a, b, (((2,), (1,)), ((0,), (0,))), preferred_element_type=F32)


def _bdot_nt(a, b):
    return lax.dot_general(a, b, (((2,), (2,)), ((0,), (0,))), preferred_element_type=F32)


def _bdot_tn(a, b):
    return lax.dot_general(a, b, (((1,), (1,)), ((0,), (0,))), preferred_element_type=F32)


def _dot_lhs01(m01, x):
    hi, mid, lo = _split3(x)
    return _dot(m01, hi) + _dot(m01, mid) + _dot(m01, lo)


def _sigmoid(x):
    return 1.0 / (1.0 + jnp.exp(-x))


def _log_sigmoid(x):
    return jnp.minimum(x, 0.0) - jnp.log1p(jnp.exp(-jnp.abs(x)))


def _rmsnorm(x, g):
    ms = jnp.mean(x * x, axis=-1, keepdims=True)
    return x * lax.rsqrt(ms + NORM_EPS) * g


def _inproj_body(x_ref, g_ref, w_ref, o_ref, h_scr):
    @pl.when(pl.program_id(1) == 0)
    def _():
        h_scr[...] = _rmsnorm(x_ref[...], g_ref[...]).astype(BF16)

    o_ref[...] = _dot(h_scr[...], w_ref[...])


def _inproj(x2d, g, wp):
    n = x2d.shape[0]
    tm = min(IN_TM, n)
    return pl.pallas_call(
        _inproj_body,
        grid=(n // tm, WP // IN_TN),
        in_specs=[pl.BlockSpec((tm, D_MODEL), lambda i, j: (i, 0)),
                  pl.BlockSpec((1, D_MODEL), lambda i, j: (0, 0)),
                  pl.BlockSpec((D_MODEL, IN_TN), lambda i, j: (0, j))],
        out_specs=pl.BlockSpec((tm, IN_TN), lambda i, j: (i, j)),
        out_shape=jax.ShapeDtypeStruct((n, WP), F32),
        scratch_shapes=[pltpu.VMEM((tm, D_MODEL), BF16)],
        compiler_params=_params(("parallel", "arbitrary")),
        name="inproj",
    )(x2d, g.reshape(1, D_MODEL), wp)


def _shift_rows(u, carry_row):
    rolled = pltpu.roll(u, 1, axis=0)
    row = lax.broadcasted_iota(jnp.int32, u.shape, 0)
    return jnp.where(row == 0, carry_row, rolled)


def _rwkv_prep(ur_ref, uk_ref, uv_ref, ul_ref, sr_ref, sk_ref, sv_ref, sl_ref,
               mu_ref, mul_ref, w0_ref, a0_ref, kk_ref, ka_ref, w2_ref, a2_ref, g2_ref, ones_ref,
               cr, ck, cv, cl):
    tm = ur_ref.shape[0]

    @pl.when(pl.program_id(1) == 0)
    def _():
        cr[0:1, :] = sr_ref[...]
        ck[0:1, :] = sk_ref[...]
        cv[0:1, :] = sv_ref[...]
        cl[0:1, :] = sl_ref[...]

    def mix(u_ref, c_ref, mu):
        u = u_ref[...]
        prev = _shift_rows(u, c_ref[0:1, :])
        c_ref[0:1, :] = u[tm - 1:tm, :]
        return u + (prev - u) * mu

    r = mix(ur_ref, cr, mu_ref[0:1, :])
    k = mix(uk_ref, ck, mu_ref[1:2, :])
    v = mix(uv_ref, cv, mu_ref[2:3, :])
    xl = mix(ul_ref, cl, mul_ref[...])

    w_lora = _dot(jnp.tanh(xl).astype(BF16), w2_ref[...])
    a_lora = _dot(xl.astype(BF16), a2_ref[...])
    g = _dot(_sigmoid(xl).astype(BF16), g2_ref[...])

    y = -(w0_ref[...] + w_lora)
    w_log = -(jnp.maximum(y, 0.0) + jnp.log1p(jnp.exp(-jnp.abs(y)))) - 0.5
    log_decay = -jnp.exp(w_log)
    a = _sigmoid(a0_ref[...] + a_lora)

    kk = k * kk_ref[...]
    hi, lo = _split2(kk * kk)
    ss = _dot(hi, ones_ref[...]) + _dot(lo, ones_ref[...])
    kk = kk / jnp.maximum(jnp.sqrt(ss), 1e-12)
    k2 = k * (1.0 + (a - 1.0) * ka_ref[...])

    return r, k2, v, kk, kk * a, log_decay, g


def _rwkv_chunks(r_ref, k_ref, v_ref, kk_ref, b_ref, lw_ref, y_ref, s_scr, n_chunks):
    c_len = RWKV_CHUNK
    hh = A_HEADS
    nb = hh * n_chunks
    row = lax.broadcasted_iota(jnp.int32, (nb, c_len, c_len), 1)
    col = lax.broadcasted_iota(jnp.int32, (nb, c_len, c_len), 2)
    tri_incl = (row >= col).astype(BF16)
    lower = row >= col
    strict = row > col
    eye = (row == col).astype(F32)

    load = lambda ref: ref[...].reshape(nb, c_len, A_HEAD)
    r, k, v, kk, b, logw = (load(ref) for ref in (r_ref, k_ref, v_ref, kk_ref, b_ref, lw_ref))
    w_hi, w_mid, w_lo = _split3(logw)
    lw = _bdot(tri_incl, w_hi) + _bdot(tri_incl, w_mid) + _bdot(tri_incl, w_lo)
    lw_prev = lw - logw
    lw_last = lw[:, c_len - 1:c_len, :]
    e_neg = jnp.exp(-lw)
    e_end = jnp.exp(lw_last - lw)
    kkm = (kk * jnp.exp(lw_prev)).astype(BF16)
    rm = r * jnp.exp(lw)
    kp = (k * e_neg).astype(BF16)
    bp = (b * e_neg).astype(BF16)
    kpp = (k * e_end).astype(BF16)
    bpp = (b * e_end).astype(BF16)
    vb = v.astype(BF16)
    rmb = rm.astype(BF16)

    a_vk = jnp.where(strict, _bdot_nt(kkm, kp), 0.0)
    a_pb = jnp.where(strict, _bdot_nt(kkm, bp), 0.0)
    rk = jnp.where(lower, _bdot_nt(rmb, kp), 0.0)
    rb = jnp.where(lower, _bdot_nt(rmb, bp), 0.0)

    n_pow = -a_pb
    t_inv = eye + n_pow
    for _ in range(5):
        npb = n_pow.astype(BF16)
        n_pow = _bdot(npb, npb)
        t_inv = t_inv + _bdot(t_inv.astype(BF16), n_pow.astype(BF16))
    tb = t_inv.astype(BF16)

    kktb = _bdot(tb, kkm).astype(BF16)
    pvb = _bdot(tb, _bdot(a_vk.astype(BF16), vb).astype(BF16)).astype(BF16)
    rbb = rb.astype(BF16)
    per_chunk = lambda x: x.reshape((hh, n_chunks) + x.shape[1:])
    m_mat = per_chunk(eye * jnp.exp(lw_last) - _bdot_tn(kktb, bpp))
    n_mat = per_chunk(_bdot_tn(vb, kpp) - _bdot_tn(pvb, bpp))
    q_mat = per_chunk((rm - _bdot(rbb, kktb)).astype(BF16))
    y0 = per_chunk(_bdot(rk.astype(BF16), vb) - _bdot(rbb, pvb))

    s = s_scr[...]
    for g in range(n_chunks):
        s_hi, s_lo = _split2(s)
        m_hi, m_lo = _split2(m_mat[:, g])
        y_ref[:, g * c_len:(g + 1) * c_len, :] = _bdot_nt(q_mat[:, g], s_hi) + y0[:, g]
        s = _bdot(s_hi, m_hi) + _bdot(s_lo, m_hi) + _bdot(s_hi, m_lo) + n_mat[:, g]
    s_scr[...] = s


def _rwkv_post(y, r, k, v, g, r_k, ln_w, ln_b, ones):
    def head_sum(x):
        hi, lo = _split2(x)
        return _dot(hi, ones) + _dot(lo, ones)

    mu = head_sum(y) * (1.0 / A_HEAD)
    yc = y - mu
    var = head_sum(yc * yc) * (1.0 / A_HEAD)
    yn = yc * lax.rsqrt(var + A_GN_EPS) * ln_w + ln_b
    bonus = head_sum(r * k * r_k) * v
    return ((yn + bonus) * g).astype(BF16)


def _rwkv_body(ur_ref, uk_ref, uv_ref, ul_ref, sr_ref, sk_ref, sv_ref, sl_ref,
               mu_ref, mul_ref, w0_ref, a0_ref, kk_ref, ka_ref, w2_ref, a2_ref, g2_ref, ones_ref,
               rk_ref, lnw_ref, lnb_ref, s0_ref,
               o_ref, sout_ref,
               cr, ck, cv, cl, s_scr, r_hm, k_hm, v_hm, kk_hm, b_hm, lw_hm, y_hm, *, n_chunks):
    @pl.when(pl.program_id(1) == 0)
    def _():
        s_scr[...] = s0_ref[...]

    r, k, v, kk, b, lw, g = _rwkv_prep(
        ur_ref, uk_ref, uv_ref, ul_ref, sr_ref, sk_ref, sv_ref, sl_ref, mu_ref, mul_ref, w0_ref, a0_ref,
        kk_ref, ka_ref, w2_ref, a2_ref, g2_ref, ones_ref, cr, ck, cv, cl)
    for x, ref in ((r, r_hm), (k, k_hm), (v, v_hm), (kk, kk_hm), (b, b_hm), (lw, lw_hm)):
        for h in range(A_HEADS):
            ref[h] = x[:, h * A_HEAD:(h + 1) * A_HEAD]
    _rwkv_chunks(r_hm, k_hm, v_hm, kk_hm, b_hm, lw_hm, y_hm, s_scr, n_chunks)
    sout_ref[...] = s_scr[...]
    y = jnp.concatenate([y_hm[h] for h in range(A_HEADS)], axis=1)
    o_ref[...] = _rwkv_post(y, r, k, v, g, rk_ref[...], lnw_ref[...], lnb_ref[...], ones_ref[...])


def _rwkv(u3, shift_prev, s0, p, tm):
    bn, t, _ = u3.shape
    sr = shift_prev[:, None, 0:W]
    sk = shift_prev[:, None, W:2 * W]
    sv = shift_prev[:, None, 2 * W:3 * W]
    sl = jnp.pad(shift_prev[:, None, 3 * W:], ((0, 0), (0, 0), (0, LORA_PAD - sum(A_LORA))))
    row = lambda c: pl.BlockSpec((None, 1, c), lambda b, i: (b, 0, 0))
    full = lambda a: pl.BlockSpec(a.shape, lambda b, i: (0,) * a.ndim)
    ublk = lambda slot: pl.BlockSpec((None, tm, W), lambda b, i, s=slot: (b, i, s))
    consts = (p["mu_rkv"], p["mu_l"], p["w0"], p["a0"], p["k_k"], p["k_a"],
              p["w2p"], p["a2p"], p["g2p"], p["head_ones"], p["r_k"], p["ln_w"], p["ln_b"])
    sblk = pl.BlockSpec((None, A_HEADS, A_HEAD, A_HEAD), lambda b, i: (b, 0, 0, 0))
    head_major = pltpu.VMEM((A_HEADS, tm, A_HEAD), F32)
    return pl.pallas_call(
        functools.partial(_rwkv_body, n_chunks=tm // RWKV_CHUNK),
        grid=(bn, t // tm),
        in_specs=[ublk(SLOT["ar"]), ublk(SLOT["ak"]), ublk(SLOT["av"]),
                  pl.BlockSpec((None, tm, LORA_PAD), lambda b, i: (b, i, COL_LORA // LORA_PAD)),
                  row(W), row(W), row(W), row(LORA_PAD)] + [full(c) for c in consts] + [sblk],
        out_specs=[pl.BlockSpec((None, tm, W), lambda b, i: (b, i, 0)), sblk],
        out_shape=[jax.ShapeDtypeStruct((bn, t, W), BF16),
                   jax.ShapeDtypeStruct((bn, A_HEADS, A_HEAD, A_HEAD), F32)],
        scratch_shapes=[pltpu.VMEM((8, W), F32)] * 3
                       + [pltpu.VMEM((8, LORA_PAD), F32), pltpu.VMEM((A_HEADS, A_HEAD, A_HEAD), F32)]
                       + [head_major] * 7,
        compiler_params=_params(("parallel", "arbitrary")),
        name="rwkv",
    )(u3, u3, u3, u3, sr, sk, sv, sl, *consts, s0)


def _pool_body(u_ref, hist_ref, w_ref, scale_ref, o_ref, carry, *, start_pos):
    tm = u_ref.shape[0]
    i = pl.program_id(1)

    @pl.when(i == 0)
    def _():
        carry[...] = hist_ref[...]

    x = u_ref[...]
    ext = jnp.concatenate([carry[...], x], axis=0)
    carry[...] = x[tm - 16:tm, :]
    s = ext
    sums = []
    for sh in (1, 2, 4, 8):
        s = s + pltpu.roll(s, sh, axis=0)
        sums.append(s[16:16 + tm, :])
    pos = start_pos + i * tm + lax.broadcasted_iota(jnp.int32, (tm, 1), 0)
    for gi, wlen in enumerate(B_WINDOWS):
        cs = slice(gi * LANE, (gi + 1) * LANE)
        cnt = jnp.minimum(wlen, pos + 1).astype(F32)
        d = sums[gi][:, cs] / cnt - x[:, cs]
        o_ref[:, cs] = (_dot(d.astype(BF16), w_ref[gi]) * scale_ref[:, cs]).astype(BF16)


def _pool(u3, hist, pool_w, pool_scale, start_pos, tm):
    bn, t, _ = u3.shape
    hist16 = jnp.pad(hist, ((0, 0), (1, 0), (0, 0)))
    return pl.pallas_call(
        functools.partial(_pool_body, start_pos=start_pos),
        grid=(bn, t // tm),
        in_specs=[pl.BlockSpec((None, tm, W), lambda b, i: (b, i, SLOT["b"])),
                  pl.BlockSpec((None, 16, W), lambda b, i: (b, 0, 0)),
                  pl.BlockSpec((4, LANE, LANE), lambda b, i: (0, 0, 0)),
                  pl.BlockSpec((1, W), lambda b, i: (0, 0))],
        out_specs=pl.BlockSpec((None, tm, W), lambda b, i: (b, i, 0)),
        out_shape=jax.ShapeDtypeStruct((bn, t, W), BF16),
        scratch_shapes=[pltpu.VMEM((16, W), F32)],
        compiler_params=_params(("parallel", "arbitrary")),
        name="pool",
    )(u3, hist16, pool_w.astype(BF16), pool_scale.reshape(1, W))


def _sb_body(q_ref, k_ref, v_ref, o_ref, acc_scr, o_scr, *, tq, q_off, n_masked):
    qi = pl.program_id(1)
    n_pairs = W // LANE
    q = q_ref[...] * (C_HEAD ** -0.5)
    q_start = q_off + qi * tq
    first_masked = q_start // SB_BLOCK
    even = lax.broadcasted_iota(jnp.int32, (tq, LANE), 1) < C_HEAD
    qz = jnp.zeros((tq, LANE), BF16)
    qs = jnp.stack([jnp.stack([jnp.where(even, q[:, p * LANE:(p + 1) * LANE], qz),
                               jnp.where(even, qz, q[:, p * LANE:(p + 1) * LANE])])
                    for p in range(n_pairs)])

    jj = lax.broadcasted_iota(jnp.int32, (2 * SB_BLOCK, 2 * SB_BLOCK), 0) % SB_BLOCK
    ss = lax.broadcasted_iota(jnp.int32, (2 * SB_BLOCK, 2 * SB_BLOCK), 1)
    cs = jnp.where(jnp.logical_or(ss >= SB_BLOCK, jj > ss), -1.0, 0.0).astype(BF16)
    aligned = q_off % SB_BLOCK == 0 and tq % SB_BLOCK == 0

    def block(kb, r0, masked):
        n = tq - r0
        ks = pl.multiple_of(kb * SB_BLOCK, SB_BLOCK)
        kblk = k_ref[pl.ds(ks, SB_BLOCK), :]
        vblk = v_ref[pl.ds(ks, SB_BLOCK), :]
        kst = jnp.stack([kblk[:, p * LANE:(p + 1) * LANE] for p in range(n_pairs)])
        vst = jnp.stack([vblk[:, p * LANE:(p + 1) * LANE] for p in range(n_pairs)])
        z = _bdot_nt(qs[:, :, r0:, :].reshape(n_pairs, 2 * n, LANE), kst).reshape(2 * n_pairs, n, SB_BLOCK)
        zp = jnp.maximum(z, 0.0)
        zn = jnp.minimum(z, 0.0)
        l1p = jnp.log(1.0 + jnp.exp(zn - zp))
        nlf = zp + l1p
        s = zn - l1p
        if masked:
            q_pos = q_start + r0 + lax.broadcasted_iota(jnp.int32, (n, SB_BLOCK), 0)
            mask = ((ks + lax.broadcasted_iota(jnp.int32, (n, SB_BLOCK), 1)) < q_pos)[None]
            nlf = jnp.where(mask, nlf, 0.0)
        hi, lo = _split2(nlf)
        lt = _dot(jnp.concatenate([hi, lo], axis=2).reshape(2 * n_pairs * n, 2 * SB_BLOCK), cs)
        lt = lt.reshape(2 * n_pairs, n, 2 * SB_BLOCK)
        att = jnp.exp(s + lt[:, :, :SB_BLOCK] + acc_scr[:, r0:, :])
        if masked:
            att = jnp.where(mask, att, 0.0)
        pv = _bdot(att.astype(BF16).reshape(n_pairs, 2 * n, SB_BLOCK), vst)
        o_scr[:, r0:, :] += pv.reshape(2 * n_pairs, n, LANE)
        acc_scr[:, r0:, :] += lt[:, :, SB_BLOCK:]

    acc_scr[...] = jnp.zeros_like(acc_scr)
    o_scr[...] = jnp.zeros_like(o_scr)
    for m in range(n_masked - 1, -1, -1):
        block(first_masked + m, m * SB_BLOCK if aligned else 0, True)

    def live(i):
        return jnp.logical_and(i < first_masked, jnp.max(acc_scr[...]) > SB_DEAD)

    def step(i):
        block(first_masked - 1 - i, 0, False)
        return i + 1

    lax.while_loop(live, step, jnp.int32(0))
    o = o_scr[...]
    o_ref[...] = jnp.concatenate([jnp.where(even, o[2 * p], o[2 * p + 1]) for p in range(n_pairs)],
                                 axis=1).astype(BF16)


def _to_bf16_body(*refs, n, n_f32):
    for x_ref, o_ref in zip(refs[:n], refs[n:2 * n]):
        o_ref[...] = x_ref[...].astype(BF16)
    for x_ref, o_ref in zip(refs[n - n_f32:n], refs[2 * n:]):
        o_ref[...] = x_ref[...]


def _to_bf16(x3, slots, tm, n_f32=0):
    bn, t, _ = x3.shape
    n = len(slots)
    oblk = pl.BlockSpec((None, tm, W), lambda b, i: (b, i, 0))
    return pl.pallas_call(
        functools.partial(_to_bf16_body, n=n, n_f32=n_f32),
        grid=(bn, t // tm),
        in_specs=[pl.BlockSpec((None, tm, W), lambda b, i, s=s: (b, i, s)) for s in slots],
        out_specs=[oblk] * (n + n_f32),
        out_shape=[jax.ShapeDtypeStruct((bn, t, W), BF16)] * n + [jax.ShapeDtypeStruct((bn, t, W), F32)] * n_f32,
        compiler_params=_params(("parallel", "parallel")),
        name="to_bf16",
    )(*([x3] * n))


def _sb_attention(q, k, v, q_off, tq):
    bn, t, _ = q.shape
    sk = k.shape[1]
    sk_pad = -(-sk // SB_BLOCK) * SB_BLOCK
    if sk_pad != sk:
        pad = ((0, 0), (0, sk_pad - sk), (0, 0))
        k, v = jnp.pad(k, pad), jnp.pad(v, pad)
    assert tq % SB_BLOCK == 0 or t == tq
    n_masked = -(-((q_off % SB_BLOCK) + tq) // SB_BLOCK)
    kv = pl.BlockSpec((None, sk_pad, W), lambda b, i: (b, 0, 0), pipeline_mode=pl.Buffered(1))
    qo = pl.BlockSpec((None, tq, W), lambda b, i: (b, i, 0))
    return pl.pallas_call(
        functools.partial(_sb_body, tq=tq, q_off=q_off, n_masked=n_masked),
        grid=(bn, t // tq),
        in_specs=[qo, kv, kv],
        out_specs=qo,
        out_shape=jax.ShapeDtypeStruct((bn, t, W), BF16),
        scratch_shapes=[pltpu.VMEM((C_HEADS, tq, SB_BLOCK), F32), pltpu.VMEM((C_HEADS, tq, LANE), F32)],
        compiler_params=_params(("parallel", "arbitrary")),
        name="sb_attention",
    )(q, k, v)


def _mlstm_body(qk_ref, v_ref, o_ref, g_ref, hist_ref, c0_ref, n0_ref, m0_ref,
                cw_ref, cb_ref, gb_ref, nw_ref,
                out_ref, conv_out, c_out, n_out, m_out,
                carry, q_scr, k_scr, c_scr, n_scr, m_scr, *, n_chunks):
    tm = qk_ref.shape[0]
    cl = tm // n_chunks

    @pl.when(pl.program_id(1) == 0)
    def _():
        carry[...] = hist_ref[...]
        c_scr[...] = c0_ref[...]
        n_scr[...] = n0_ref[...]
        m_scr[...] = m0_ref[...]

    x = qk_ref[...]
    ext = jnp.concatenate([carry[...], x], axis=0)
    carry[...] = x[tm - 8:tm, :]
    conv = cb_ref[...] + x * cw_ref[D_CONV - 1:D_CONV, :]
    for sh in range(1, D_CONV):
        conv = conv + pltpu.roll(ext, sh, axis=0)[8:8 + tm, :] * cw_ref[D_CONV - 1 - sh:D_CONV - sh, :]
    conv = conv * _sigmoid(conv)
    q_scr[...] = conv[:, :W]
    k_scr[...] = conv[:, W:] * (D_HEAD ** -0.5)

    row = lax.broadcasted_iota(jnp.int32, (cl, cl), 0)
    col = lax.broadcasted_iota(jnp.int32, (cl, cl), 1)
    causal = row >= col
    tri_incl = causal.astype(BF16)

    def chunk(c, carry_):
        off = pl.multiple_of(c * cl, cl)
        sl = pl.ds(off, cl)
        gpre = g_ref[sl, :] + gb_ref[...]
        lfa = _log_sigmoid(gpre)
        bcol = _dot_lhs01(tri_incl, lfa)
        g_t = gpre.T
        b_t = bcol.T
        heads = range(D_HEADS)
        hs = [slice(h * D_HEAD, (h + 1) * D_HEAD) for h in heads]
        q = jnp.stack([q_scr[sl, hs[h]] for h in heads])
        k = jnp.stack([k_scr[sl, hs[h]] for h in heads])
        v = jnp.stack([v_ref[sl, hs[h]] for h in heads])
        ig_col = jnp.stack([gpre[:, h:h + 1] for h in heads])
        b_col = jnp.stack([bcol[:, D_HEADS + h:D_HEADS + h + 1] for h in heads])
        ig_row = jnp.stack([g_t[h:h + 1, :] for h in heads])
        b_row = jnp.stack([b_t[D_HEADS + h:D_HEADS + h + 1, :] for h in heads])
        m_prev = jnp.stack([m_scr[h:h + 1, 0:1] for h in heads])
        n_prev = jnp.stack([n_scr[h:h + 1, :] for h in heads])
        c_prev = c_scr[...]
        log_d = jnp.where(causal[None], b_col - b_row + ig_row, -jnp.inf)
        m_inter = b_col + m_prev
        m_t = jnp.maximum(m_inter, jnp.max(log_d, axis=-1, keepdims=True))
        dmat = jnp.exp(log_d - m_t)
        inter = jnp.exp(m_inter - m_t)
        qb = q.astype(BF16)
        vb = v.astype(BF16)
        w_qk = _bdot_nt(qb, k.astype(BF16)) * dmat
        num = _bdot(w_qk.astype(BF16), vb) + inter * _bdot(qb, c_prev.astype(BF16))
        qn = jnp.sum(q * n_prev, axis=-1, keepdims=True)
        den = jnp.sum(w_qk, axis=-1, keepdims=True) + inter * qn
        den = jnp.maximum(jnp.abs(den), jnp.exp(-m_t))
        hv = num / den
        m_new = m_t[:, cl - 1:cl, :]
        b_last = b_col[:, cl - 1:cl, :]
        decay = jnp.exp(b_last + m_prev - m_new)
        w_s = jnp.exp(b_last - b_col + ig_col - m_new)
        kw = k * w_s
        c_scr[...] = decay * c_prev + _bdot_tn(kw.astype(BF16), vb)
        n_new = decay * n_prev + jnp.sum(kw, axis=1, keepdims=True)
        mu = jnp.mean(hv, axis=-1, keepdims=True)
        hc = hv - mu
        var = jnp.mean(hc * hc, axis=-1, keepdims=True)
        hn = hc * lax.rsqrt(var + D_NORM_EPS)
        for h in heads:
            n_scr[h:h + 1, :] = n_new[h]
            m_scr[h:h + 1, :] = jnp.broadcast_to(m_new[h], (1, LANE))
            out_ref[sl, hs[h]] = (_sigmoid(o_ref[sl, hs[h]]) * (hn[h] * nw_ref[:, hs[h]])).astype(BF16)
        return carry_

    lax.fori_loop(0, n_chunks, chunk, 0)
    conv_out[...] = carry[...]
    c_out[...] = c_scr[...]
    n_out[...] = n_scr[...]
    m_out[...] = m_scr[...]


def _mlstm(u3, conv_hist, c0, n0, m0, p, tm):
    bn, t, _ = u3.shape
    hist8 = jnp.pad(conv_hist, ((0, 0), (8 - (D_CONV - 1), 0), (0, 0)))
    n0p = jnp.pad(n0, ((0, 0), (0, 8 - D_HEADS), (0, 0)))
    m0p = jnp.pad(jnp.broadcast_to(m0[:, :, None], (bn, D_HEADS, LANE)), ((0, 0), (0, 8 - D_HEADS), (0, 0)))
    full = lambda a: pl.BlockSpec(a.shape, lambda b, i: (0,) * a.ndim)
    per_b = lambda *s: pl.BlockSpec((None,) + s, lambda b, i: (b,) + (0,) * len(s))
    consts = (p["conv_w"], p["conv_b"], p["gate_bias"], p["norm_w"])
    outs = pl.pallas_call(
        functools.partial(_mlstm_body, n_chunks=tm // min(MLSTM_CHUNK, tm)),
        grid=(bn, t // tm),
        in_specs=[pl.BlockSpec((None, tm, 2 * W), lambda b, i: (b, i, 0)),
                  pl.BlockSpec((None, tm, W), lambda b, i: (b, i, SLOT["dv"])),
                  pl.BlockSpec((None, tm, W), lambda b, i: (b, i, SLOT["do"])),
                  pl.BlockSpec((None, tm, LANE), lambda b, i: (b, i, COL_DG // LANE)),
                  per_b(8, 2 * W), per_b(D_HEADS, D_HEAD, D_HEAD), per_b(8, LANE), per_b(8, LANE)]
                 + [full(c) for c in consts],
        out_specs=[pl.BlockSpec((None, tm, W), lambda b, i: (b, i, 0)),
                   per_b(8, 2 * W), per_b(D_HEADS, D_HEAD, D_HEAD), per_b(8, LANE), per_b(8, LANE)],
        out_shape=[jax.ShapeDtypeStruct((bn, t, W), BF16),
                   jax.ShapeDtypeStruct((bn, 8, 2 * W), F32),
                   jax.ShapeDtypeStruct((bn, D_HEADS, D_HEAD, D_HEAD), F32),
                   jax.ShapeDtypeStruct((bn, 8, LANE), F32),
                   jax.ShapeDtypeStruct((bn, 8, LANE), F32)],
        scratch_shapes=[pltpu.VMEM((8, 2 * W), F32), pltpu.VMEM((tm, W), F32), pltpu.VMEM((tm, W), F32),
                        pltpu.VMEM((D_HEADS, D_HEAD, D_HEAD), F32), pltpu.VMEM((8, LANE), F32),
                        pltpu.VMEM((8, LANE), F32)],
        compiler_params=_params(("parallel", "arbitrary")),
        name="mlstm",
    )(u3, u3, u3, u3, hist8, c0, n0p, m0p, *consts)
    out_d, conv8, c, n8, m8 = outs
    return out_d, conv8[:, 8 - (D_CONV - 1):], c, n8[:, :D_HEADS], m8[:, :D_HEADS, 0]


def _merge_body(x_ref, g_ref, a_ref, b_ref, c_ref, d_ref, wg_ref, wb_ref, wo_ref, o_ref):
    g = g_ref[...].astype(BF16)
    merged = None
    for i, br in enumerate((a_ref, b_ref, c_ref, d_ref)):
        term = _sigmoid(_dot(g, wg_ref[i])) * _dot(br[...], wb_ref[i])
        merged = term if merged is None else merged + term
    o_ref[...] = x_ref[...] + _dot(merged.astype(BF16), wo_ref[...])


def _merge(x3, u3, out_a, out_b, out_c, out_d, wg, wb, wo, tm):
    bn, t, _ = x3.shape
    once = lambda a: pl.BlockSpec(a.shape, lambda b, i: (0,) * a.ndim, pipeline_mode=pl.Buffered(1))
    bblk = pl.BlockSpec((None, tm, W), lambda b, i: (b, i, 0))
    xblk = pl.BlockSpec((None, tm, D_MODEL), lambda b, i: (b, i, 0))
    return pl.pallas_call(
        _merge_body,
        grid=(bn, t // tm),
        in_specs=[xblk, pl.BlockSpec((None, tm, GATE_RANK), lambda b, i: (b, i, COL_G // GATE_RANK)),
                  bblk, bblk, bblk, bblk, once(wg), once(wb), once(wo)],
        out_specs=xblk,
        out_shape=jax.ShapeDtypeStruct((bn, t, D_MODEL), F32),
        compiler_params=_params(("parallel", "parallel")),
        name="merge",
    )(x3, u3, out_a, out_b, out_c, out_d, wg, wb, wo)


def _ffn_epilogue(acc, gf_ref, final_norm):
    return _rmsnorm(acc, gf_ref[...]) if final_norm else acc


def _ffn_dense_body(x_ref, g_ref, wg_ref, wu_ref, wd_ref, gf_ref, o_ref, h_scr, acc, *, final_norm):
    j = pl.program_id(1)

    @pl.when(j == 0)
    def _():
        x = x_ref[...]
        h_scr[...] = _rmsnorm(x, g_ref[...]).astype(BF16)
        acc[...] = x

    h = h_scr[...]
    a = _dot(h, wg_ref[...])
    act = (a * _sigmoid(a)) * _dot(h, wu_ref[...])
    acc[...] += _dot(act.astype(BF16), wd_ref[...])

    @pl.when(j == pl.num_programs(1) - 1)
    def _():
        o_ref[...] = _ffn_epilogue(acc[...], gf_ref, final_norm)


def _ffn_dense(x2d, g, wgate, wup, wdown, g_final, final_norm, tm, tf=512):
    n = x2d.shape[0]
    xblk = pl.BlockSpec((tm, D_MODEL), lambda i, j: (i, 0))
    vec = pl.BlockSpec((1, D_MODEL), lambda i, j: (0, 0))
    return pl.pallas_call(
        functools.partial(_ffn_dense_body, final_norm=final_norm),
        grid=(n // tm, D_FF // tf),
        in_specs=[xblk, vec,
                  pl.BlockSpec((D_MODEL, tf), lambda i, j: (0, j)),
                  pl.BlockSpec((D_MODEL, tf), lambda i, j: (0, j)),
                  pl.BlockSpec((tf, D_MODEL), lambda i, j: (j, 0)), vec],
        out_specs=xblk,
        out_shape=jax.ShapeDtypeStruct((n, D_MODEL), F32),
        scratch_shapes=[pltpu.VMEM((tm, D_MODEL), BF16), pltpu.VMEM((tm, D_MODEL), F32)],
        compiler_params=_params(("parallel", "arbitrary")),
        name="ffn_dense",
    )(x2d, g.reshape(1, D_MODEL), wgate, wup, wdown, g_final.reshape(1, D_MODEL))


def _ffn_moe_body(x_ref, g_ref, rt_ref, wg_ref, wu_ref, wd_ref, gf_ref, o_ref,
                  h_scr, acc, gates, *, final_norm):
    e = pl.program_id(1)

    @pl.when(e == 0)
    def _():
        x = x_ref[...]
        h = _rmsnorm(x, g_ref[...])
        h_scr[...] = h.astype(BF16)
        acc[...] = x
        h_hi, h_lo = _split2(h)
        r_hi, r_lo = _split2(rt_ref[...])
        logits = _dot(h_hi, r_hi) + _dot(h_lo, r_hi) + _dot(h_hi, r_lo)
        lane = lax.broadcasted_iota(jnp.int32, logits.shape, 1)
        logits = jnp.where(lane < N_EXPERTS, logits, -jnp.inf)
        v1 = jnp.max(logits, axis=-1, keepdims=True)
        i1 = jnp.min(jnp.where(logits == v1, lane, LANE), axis=-1, keepdims=True)
        rest = jnp.where(lane == i1, -jnp.inf, logits)
        v2 = jnp.max(rest, axis=-1, keepdims=True)
        i2 = jnp.min(jnp.where(rest == v2, lane, LANE), axis=-1, keepdims=True)
        e2 = jnp.exp(v2 - v1)
        den = 1.0 + e2
        gates[...] = jnp.where(lane == i1, 1.0 / den, 0.0) + jnp.where(lane == i2, e2 / den, 0.0)

    h = h_scr[...]
    lane = lax.broadcasted_iota(jnp.int32, gates.shape, 1)
    gate = jnp.sum(jnp.where(lane == e, gates[...], 0.0), axis=-1, keepdims=True)
    a = _dot(h, wg_ref[...])
    act = (a * _sigmoid(a)) * _dot(h, wu_ref[...])
    acc[...] += gate * _dot(act.astype(BF16), wd_ref[...])

    @pl.when(e == pl.num_programs(1) - 1)
    def _():
        o_ref[...] = _ffn_epilogue(acc[...], gf_ref, final_norm)


def _ffn_moe(x2d, g, router, wgate, wup, wdown, g_final, final_norm, tm):
    n = x2d.shape[0]
    xblk = pl.BlockSpec((tm, D_MODEL), lambda i, j: (i, 0))
    vec = pl.BlockSpec((1, D_MODEL), lambda i, j: (0, 0))
    return pl.pallas_call(
        functools.partial(_ffn_moe_body, final_norm=final_norm),
        grid=(n // tm, N_EXPERTS),
        in_specs=[xblk, vec,
                  pl.BlockSpec((D_MODEL, LANE), lambda i, j: (0, 0)),
                  pl.BlockSpec((None, D_MODEL, EXPERT_PAD), lambda i, j: (j, 0, 0)),
                  pl.BlockSpec((None, D_MODEL, EXPERT_PAD), lambda i, j: (j, 0, 0)),
                  pl.BlockSpec((None, EXPERT_PAD, D_MODEL), lambda i, j: (j, 0, 0)), vec],
        out_specs=xblk,
        out_shape=jax.ShapeDtypeStruct((n, D_MODEL), F32),
        scratch_shapes=[pltpu.VMEM((tm, D_MODEL), BF16), pltpu.VMEM((tm, D_MODEL), F32),
                        pltpu.VMEM((tm, LANE), F32)],
        compiler_params=_params(("parallel", "arbitrary")),
        name="ffn_moe",
    )(x2d, g.reshape(1, D_MODEL), router, wgate, wup, wdown, g_final.reshape(1, D_MODEL))


def _pack_w_in(w):
    a, b, c, d = w[:, :OFF_B], w[:, OFF_B:OFF_C], w[:, OFF_C:OFF_D], w[:, OFF_D:OFF_G]
    g = w[:, OFF_G:]
    zeros = lambda k: jnp.zeros((D_MODEL, k), w.dtype)
    lora = a[:, 3 * W:]
    gates = d[:, 4 * W:]
    cols = [d[:, :4 * W], a[:, :3 * W], b, c,
            lora, zeros(LORA_PAD - lora.shape[1]), g,
            gates, zeros(LANE - gates.shape[1]), zeros(WP - COL_DG - LANE)]
    return jnp.concatenate(cols, axis=1).astype(BF16)


def _layer_params(l, wts):
    (norm_mix, norm_ffn, norm_final, w_in, rwkv_mu, rwkv_w0, rwkv_w2, rwkv_a0, rwkv_a2,
     rwkv_g2, rwkv_k_k, rwkv_k_a, rwkv_r_k, rwkv_ln_w, rwkv_ln_b, pool_w, pool_scale,
     mlstm_conv_w, mlstm_conv_b, mlstm_i_bias, mlstm_f_bias, mlstm_norm_w,
     w_branch, w_merge_gate, w_out, ffn_w_gate, ffn_w_up, ffn_w_down,
     moe_router, moe_w_gate, moe_w_up, moe_w_down) = wts
    n_l = sum(A_LORA)
    row = lambda v: v.reshape(1, -1)

    def lora_pad(wm, start):
        return jnp.pad(wm, ((start, LORA_PAD - start - wm.shape[0]), (0, 0))).astype(BF16)

    hid = jnp.arange(W) // A_HEAD
    p = {
        "norm_mix": norm_mix[l], "norm_ffn": norm_ffn[l],
        "wp": _pack_w_in(w_in[l]),
        "rwkv": {
            "mu_rkv": rwkv_mu[l, :3 * W].reshape(3, W),
            "mu_l": jnp.pad(rwkv_mu[l, 3 * W:], (0, LORA_PAD - n_l)).reshape(1, LORA_PAD),
            "w0": row(rwkv_w0[l]), "a0": row(rwkv_a0[l]),
            "k_k": row(rwkv_k_k[l]), "k_a": row(rwkv_k_a[l]),
            "w2p": lora_pad(rwkv_w2[l], 0),
            "a2p": lora_pad(rwkv_a2[l], A_LORA[0]),
            "g2p": lora_pad(rwkv_g2[l], A_LORA[0] + A_LORA[1]),
            "head_ones": (hid[:, None] == hid[None, :]).astype(BF16),
            "r_k": row(rwkv_r_k[l]), "ln_w": row(rwkv_ln_w[l]), "ln_b": row(rwkv_ln_b[l]),
        },
        "pool_w": pool_w[l], "pool_scale": pool_scale[l],
        "mlstm": {
            "conv_w": mlstm_conv_w[l], "conv_b": row(mlstm_conv_b[l]),
            "gate_bias": jnp.pad(jnp.concatenate([mlstm_i_bias[l], mlstm_f_bias[l]]),
                                 (0, LANE - 2 * D_HEADS)).reshape(1, LANE),
            "norm_w": row(mlstm_norm_w[l]),
        },
        "wg": w_merge_gate[l].astype(BF16), "wb": w_branch[l].astype(BF16), "wo": w_out[l].astype(BF16),
    }
    if l % 2 == 0:
        p["ffn"] = (ffn_w_gate[l // 2].astype(BF16), ffn_w_up[l // 2].astype(BF16),
                    ffn_w_down[l // 2].astype(BF16))
    else:
        pe = EXPERT_PAD - D_FF_EXPERT
        p["moe"] = (jnp.pad(moe_router[l // 2], ((0, 0), (0, LANE - N_EXPERTS))),
                    jnp.pad(moe_w_gate[l // 2], ((0, 0), (0, 0), (0, pe))).astype(BF16),
                    jnp.pad(moe_w_up[l // 2], ((0, 0), (0, 0), (0, pe))).astype(BF16),
                    jnp.pad(moe_w_down[l // 2], ((0, 0), (0, pe), (0, 0))).astype(BF16))
    return p


def _slot(u3, name):
    s = SLOT[name] * W
    return u3[:, :, s:s + W]


def _run_trunk(x, start_pos, cache_k, cache_v, wkv0, shift0, pool0, conv0, c0, n0, m0,
               layers, norm_final):
    bn, t, _ = x.shape
    n = bn * t
    tm_tok = min(256, t)
    tm_row = min(512, n)
    x2 = x.reshape(n, D_MODEL)
    ks, vs, wkvs, shifts, pools, convs, cs, ns, ms = ([] for _ in range(9))
    for l, p in enumerate(layers):
        u2 = _inproj(x2, p["norm_mix"], p["wp"])
        u3 = u2.reshape(bn, t, WP)

        out_a, wkv = _rwkv(u3, shift0[l], wkv0[l], p["rwkv"], tm_tok)

        out_b = _pool(u3, pool0[l], p["pool_w"], p["pool_scale"], start_pos, tm_tok)

        q_b, k_b, v_b, k_c, v_c = _to_bf16(u3, (SLOT["cq"], SLOT["ck"], SLOT["cv"]), min(512, t), n_f32=2)
        if cache_k is None:
            out_c = _sb_attention(q_b, k_b, v_b, 0, min(SB_TQ, t))
        else:
            past = cache_k.shape[2]
            (kc_b,) = _to_bf16(cache_k[l].reshape(bn, past, W), (0,), min(512, past))
            (vc_b,) = _to_bf16(cache_v[l].reshape(bn, past, W), (0,), min(512, past))
            out_c = _sb_attention(q_b, jnp.concatenate([kc_b, k_b], axis=1),
                                  jnp.concatenate([vc_b, v_b], axis=1), past, t)

        out_d, conv_new, c_new, n_new, m_new = _mlstm(u3, conv0[l], c0[l], n0[l], m0[l], p["mlstm"], tm_tok)

        x2 = _merge(x2.reshape(bn, t, D_MODEL), u3, out_a, out_b, out_c, out_d,
                    p["wg"], p["wb"], p["wo"], tm_tok).reshape(n, D_MODEL)
        last = l == len(layers) - 1
        if "ffn" in p:
            x2 = _ffn_dense(x2, p["norm_ffn"], *p["ffn"], norm_final, last, tm_row)
        else:
            x2 = _ffn_moe(x2, p["norm_ffn"], *p["moe"], norm_final, last, tm_row)

        ks.append(k_c.reshape(bn, t, C_HEADS, C_HEAD))
        vs.append(v_c.reshape(bn, t, C_HEADS, C_HEAD))
        wkvs.append(wkv)
        u_last = u3[:, -1]
        shifts.append(jnp.concatenate(
            [u_last[:, SLOT["ar"] * W:SLOT["ar"] * W + 3 * W], u_last[:, COL_LORA:COL_LORA + sum(A_LORA)]], axis=1))
        pools.append(jnp.concatenate([pool0[l], _slot(u3, "b")[:, -B_HIST:]], axis=1)[:, -B_HIST:])
        convs.append(conv_new)
        cs.append(c_new)
        ns.append(n_new)
        ms.append(m_new)
    st = lambda lst: jnp.stack(lst, axis=0)
    return (x2.reshape(bn, t, D_MODEL),
            (st(ks), st(vs), st(wkvs), st(shifts), st(pools), st(convs), st(cs), st(ns), st(ms)))


def kernel(x_prompt, x_sample, cache_sb_k, cache_sb_v, state_rwkv_wkv, state_rwkv_shift, state_pool,
           state_mlstm_conv, state_mlstm_c, state_mlstm_n, state_mlstm_m, norm_mix, norm_ffn, norm_final,
           w_in, rwkv_mu, rwkv_w0, rwkv_w2, rwkv_a0, rwkv_a2, rwkv_g2, rwkv_k_k, rwkv_k_a, rwkv_r_k,
           rwkv_ln_w, rwkv_ln_b, pool_w, pool_scale, mlstm_conv_w, mlstm_conv_b, mlstm_i_bias,
           mlstm_f_bias, mlstm_norm_w, w_branch, w_merge_gate, w_out, ffn_w_gate, ffn_w_up, ffn_w_down,
           moe_router, moe_w_gate, moe_w_up, moe_w_down):
    wts = (norm_mix, norm_ffn, norm_final, w_in, rwkv_mu, rwkv_w0, rwkv_w2, rwkv_a0, rwkv_a2,
           rwkv_g2, rwkv_k_k, rwkv_k_a, rwkv_r_k, rwkv_ln_w, rwkv_ln_b, pool_w, pool_scale,
           mlstm_conv_w, mlstm_conv_b, mlstm_i_bias, mlstm_f_bias, mlstm_norm_w,
           w_branch, w_merge_gate, w_out, ffn_w_gate, ffn_w_up, ffn_w_down,
           moe_router, moe_w_gate, moe_w_up, moe_w_down)
    depth = w_in.shape[0]
    layers = [_layer_params(l, wts) for l in range(depth)]
    bp = x_prompt.shape[0]
    z = lambda *s: jnp.zeros((depth, bp) + s, F32)
    y_p, st_p = _run_trunk(
        x_prompt, 0, None, None, z(A_HEADS, A_HEAD, A_HEAD), z(A_IN), z(B_HIST, W),
        z(D_CONV - 1, 2 * W), z(D_HEADS, D_HEAD, D_HEAD), z(D_HEADS, D_HEAD), z(D_HEADS),
        layers, norm_final)
    y_s, st_s = _run_trunk(
        x_sample, cache_sb_k.shape[2], cache_sb_k, cache_sb_v, state_rwkv_wkv, state_rwkv_shift,
        state_pool, state_mlstm_conv, state_mlstm_c, state_mlstm_n, state_mlstm_m,
        layers, norm_final)
    return (y_p, y_s) + tuple(st_p) + tuple(st_s)
```

```python
import functools

import jax
import jax.numpy as jnp
from jax import lax
from jax.experimental import pallas as pl
from jax.experimental.pallas import tpu as pltpu

F32 = jnp.float32
BF16 = jnp.bfloat16

D_MODEL = 2048
DEPTH = 2
W = 512
A_HEAD, A_HEADS = 64, 8
A_LORA = (32, 32, 96)
A_IN = 3 * W + sum(A_LORA)
A_GN_EPS = 64e-5
B_WINDOWS = (2, 4, 8, 16)
B_HIST = 15
C_HEAD, C_HEADS = 64, 8
D_HEADS, D_HEAD = 4, 128
D_CONV = 4
D_NORM_EPS = 1e-6
GATE_RANK = 256
OFF_B = A_IN
OFF_C = OFF_B + W
OFF_D = OFF_C + 3 * W
OFF_G = OFF_D + 4 * W + 2 * D_HEADS
IN_WIDTH = OFF_G + GATE_RANK
D_FF = 5632
N_EXPERTS = 8
D_FF_EXPERT = D_FF // N_EXPERTS
NORM_EPS = 1e-6

LANE = 128
V7X_VMEM_BYTES = 64 * 1024 * 1024
VMEM_LIMIT = V7X_VMEM_BYTES - 8 * 1024 * 1024

SLOT = {name: i for i, name in enumerate(
    ("dq", "dk", "dv", "do", "ar", "ak", "av", "b", "cq", "ck", "cv"))}
COL_LORA = 11 * W
COL_G = COL_LORA + 256
COL_DG = COL_G + 256
IN_TN = 1280
IN_TM = 1024
WP = 6400
LORA_PAD = 256
EXPERT_PAD = 768

TOK_TILE = 256
ROW_TILE = 512
RWKV_CHUNK = 64
MLSTM_CHUNK = 128
SB_BLOCK = 128
SB_TQ = 256
SB_DEAD = -120.0


def _params(sem, vmem=VMEM_LIMIT):
    return pltpu.CompilerParams(dimension_semantics=sem, vmem_limit_bytes=vmem)


def _split2(x):
    hi = x.astype(BF16)
    lo = (x - hi.astype(F32)).astype(BF16)
    return hi, lo


def _split3(x):
    hi = x.astype(BF16)
    r1 = x - hi.astype(F32)
    mid = r1.astype(BF16)
    lo = (r1 - mid.astype(F32)).astype(BF16)
    return hi, mid, lo


def _dot(a, b):
    return jnp.dot(a, b, preferred_element_type=F32)


def _dot_nt(a, b):
    return lax.dot_general(a, b, (((1,), (1,)), ((), ())), preferred_element_type=F32)


def _bdot(a, b):
    return lax.dot_general(a, b, (((2,), (1,)), ((0,), (0,))), preferred_element_type=F32)


def _bdot_nt(a, b):
    return lax.dot_general(a, b, (((2,), (2,)), ((0,), (0,))), preferred_element_type=F32)


def _bdot_tn(a, b):
    return lax.dot_general(a, b, (((1,), (1,)), ((0,), (0,))), preferred_element_type=F32)


def _dot_lhs01(m01, x):
    hi, mid, lo = _split3(x)
    return _dot(m01, hi) + _dot(m01, mid) + _dot(m01, lo)


def _sigmoid(x):
    return 1.0 / (1.0 + jnp.exp(-x))


def _log_sigmoid(x):
    return jnp.minimum(x, 0.0) - jnp.log1p(jnp.exp(-jnp.abs(x)))


def _rmsnorm(x, g):
    ms = jnp.mean(x * x, axis=-1, keepdims=True)
    return x * lax.rsqrt(ms + NORM_EPS) * g


def _inproj_body(x_ref, g_ref, w_ref, o_ref, h_scr):
    @pl.when(pl.program_id(1) == 0)
    def _():
        h_scr[...] = _rmsnorm(x_ref[...], g_ref[...]).astype(BF16)

    o_ref[...] = _dot(h_scr[...], w_ref[...])


def _inproj(x2d, g, wp):
    n = x2d.shape[0]
    tm = min(IN_TM, n)
    return pl.pallas_call(
        _inproj_body,
        grid=(n // tm, WP // IN_TN),
        in_specs=[pl.BlockSpec((tm, D_MODEL), lambda i, j: (i, 0)),
                  pl.BlockSpec((1, D_MODEL), lambda i, j: (0, 0)),
                  pl.BlockSpec((D_MODEL, IN_TN), lambda i, j: (0, j))],
        out_specs=pl.BlockSpec((tm, IN_TN), lambda i, j: (i, j)),
        out_shape=jax.ShapeDtypeStruct((n, WP), F32),
        scratch_shapes=[pltpu.VMEM((tm, D_MODEL), BF16)],
        compiler_params=_params(("parallel", "arbitrary")),
        name="inproj",
    )(x2d, g.reshape(1, D_MODEL), wp)


def _shift_rows(u, carry_row):
    rolled = pltpu.roll(u, 1, axis=0)
    row = lax.broadcasted_iota(jnp.int32, u.shape, 0)
    return jnp.where(row == 0, carry_row, rolled)


def _rwkv_prep(ur_ref, uk_ref, uv_ref, ul_ref, sr_ref, sk_ref, sv_ref, sl_ref,
               mu_ref, mul_ref, w0_ref, a0_ref, kk_ref, ka_ref, w2_ref, a2_ref, g2_ref, ones_ref,
               cr, ck, cv, cl):
    tm = ur_ref.shape[0]

    @pl.when(pl.program_id(1) == 0)
    def _():
        cr[0:1, :] = sr_ref[...]
        ck[0:1, :] = sk_ref[...]
        cv[0:1, :] = sv_ref[...]
        cl[0:1, :] = sl_ref[...]

    def mix(u_ref, c_ref, mu):
        u = u_ref[...]
        prev = _shift_rows(u, c_ref[0:1, :])
        c_ref[0:1, :] = u[tm - 1:tm, :]
        return u + (prev - u) * mu

    r = mix(ur_ref, cr, mu_ref[0:1, :])
    k = mix(uk_ref, ck, mu_ref[1:2, :])
    v = mix(uv_ref, cv, mu_ref[2:3, :])
    xl = mix(ul_ref, cl, mul_ref[...])

    w_lora = _dot(jnp.tanh(xl).astype(BF16), w2_ref[...])
    a_lora = _dot(xl.astype(BF16), a2_ref[...])
    g = _dot(_sigmoid(xl).astype(BF16), g2_ref[...])

    y = -(w0_ref[...] + w_lora)
    w_log = -(jnp.maximum(y, 0.0) + jnp.log1p(jnp.exp(-jnp.abs(y)))) - 0.5
    log_decay = -jnp.exp(w_log)
    a = _sigmoid(a0_ref[...] + a_lora)

    kk = k * kk_ref[...]
    hi, lo = _split2(kk * kk)
    ss = _dot(hi, ones_ref[...]) + _dot(lo, ones_ref[...])
    kk = kk / jnp.maximum(jnp.sqrt(ss), 1e-12)
    k2 = k * (1.0 + (a - 1.0) * ka_ref[...])

    return r, k2, v, kk, kk * a, log_decay, g


def _rwkv_chunks(r_ref, k_ref, v_ref, kk_ref, b_ref, lw_ref, y_ref, s_scr, n_chunks):
    c_len = RWKV_CHUNK
    hh = A_HEADS
    nb = hh * n_chunks
    row = lax.broadcasted_iota(jnp.int32, (1, c_len, c_len), 1)
    col = lax.broadcasted_iota(jnp.int32, (1, c_len, c_len), 2)
    tri_incl = jnp.broadcast_to((row >= col).astype(BF16), (nb, c_len, c_len))
    lower = row >= col
    strict = row > col
    eye = (row == col).astype(F32)

    load = lambda ref: ref[...].reshape(nb, c_len, A_HEAD)
    r, k, v, kk, b, logw = (load(ref) for ref in (r_ref, k_ref, v_ref, kk_ref, b_ref, lw_ref))
    w_hi, w_mid, w_lo = _split3(logw)
    lw = _bdot(tri_incl, w_hi) + _bdot(tri_incl, w_mid) + _bdot(tri_incl, w_lo)
    lw_prev = lw - logw
    lw_last = lw[:, c_len - 1:c_len, :]
    e_neg = jnp.exp(-lw)
    e_end = jnp.exp(lw_last - lw)
    kkm = (kk * jnp.exp(lw_prev)).astype(BF16)
    rm = r * jnp.exp(lw)
    kp = (k * e_neg).astype(BF16)
    bp = (b * e_neg).astype(BF16)
    kpp = (k * e_end).astype(BF16)
    bpp = (b * e_end).astype(BF16)
    vb = v.astype(BF16)
    rmb = rm.astype(BF16)

    a_vk = jnp.where(strict, _bdot_nt(kkm, kp), 0.0)
    a_pb = jnp.where(strict, _bdot_nt(kkm, bp), 0.0)
    rk = jnp.where(lower, _bdot_nt(rmb, kp), 0.0)
    rb = jnp.where(lower, _bdot_nt(rmb, bp), 0.0)

    n_pow = -a_pb
    t_inv = eye + n_pow
    for _ in range(5):
        npb = n_pow.astype(BF16)
        n_pow = _bdot(npb, npb)
        t_inv = t_inv + _bdot(t_inv.astype(BF16), n_pow.astype(BF16))
    tb = t_inv.astype(BF16)

    kktb = _bdot(tb, kkm).astype(BF16)
    pvb = _bdot(tb, _bdot(a_vk.astype(BF16), vb).astype(BF16)).astype(BF16)
    rbb = rb.astype(BF16)
    per_chunk = lambda x: x.reshape((hh, n_chunks) + x.shape[1:])
    m_mat = per_chunk(eye * jnp.exp(lw_last) - _bdot_tn(kktb, bpp))
    n_mat = per_chunk(_bdot_tn(vb, kpp) - _bdot_tn(pvb, bpp))
    q_mat = per_chunk((rm - _bdot(rbb, kktb)).astype(BF16))
    y0 = per_chunk(_bdot(rk.astype(BF16), vb) - _bdot(rbb, pvb))

    s = s_scr[...]
    for g in range(n_chunks):
        s_hi, s_lo = _split2(s)
        m_hi, m_lo = _split2(m_mat[:, g])
        y_ref[:, g * c_len:(g + 1) * c_len, :] = _bdot_nt(q_mat[:, g], s_hi) + y0[:, g]
        s = _bdot(s_hi, m_hi) + _bdot(s_lo, m_hi) + _bdot(s_hi, m_lo) + n_mat[:, g]
    s_scr[...] = s


def _rwkv_post(y, r, k, v, g, r_k, ln_w, ln_b, ones):
    def head_sum(x):
        hi, lo = _split2(x)
        return _dot(hi, ones) + _dot(lo, ones)

    mu = head_sum(y) * (1.0 / A_HEAD)
    yc = y - mu
    var = head_sum(yc * yc) * (1.0 / A_HEAD)
    yn = yc * lax.rsqrt(var + A_GN_EPS) * ln_w + ln_b
    bonus = head_sum(r * k * r_k) * v
    return ((yn + bonus) * g).astype(BF16)


def _rwkv_body(ur_ref, uk_ref, uv_ref, ul_ref, sr_ref, sk_ref, sv_ref, sl_ref,
               mu_ref, mul_ref, w0_ref, a0_ref, kk_ref, ka_ref, w2_ref, a2_ref, g2_ref, ones_ref,
               rk_ref, lnw_ref, lnb_ref, s0_ref,
               o_ref, sout_ref,
               cr, ck, cv, cl, s_scr, r_hm, k_hm, v_hm, kk_hm, b_hm, lw_hm, y_hm, *, n_chunks):
    @pl.when(pl.program_id(1) == 0)
    def _():
        s_scr[...] = s0_ref[...]

    r, k, v, kk, b, lw, g = _rwkv_prep(
        ur_ref, uk_ref, uv_ref, ul_ref, sr_ref, sk_ref, sv_ref, sl_ref, mu_ref, mul_ref, w0_ref, a0_ref,
        kk_ref, ka_ref, w2_ref, a2_ref, g2_ref, ones_ref, cr, ck, cv, cl)
    for x, ref in ((r, r_hm), (k, k_hm), (v, v_hm), (kk, kk_hm), (b, b_hm), (lw, lw_hm)):
        for h in range(A_HEADS):
            ref[h] = x[:, h * A_HEAD:(h + 1) * A_HEAD]
    _rwkv_chunks(r_hm, k_hm, v_hm, kk_hm, b_hm, lw_hm, y_hm, s_scr, n_chunks)
    sout_ref[...] = s_scr[...]
    y = jnp.concatenate([y_hm[h] for h in range(A_HEADS)], axis=1)
    o_ref[...] = _rwkv_post(y, r, k, v, g, rk_ref[...], lnw_ref[...], lnb_ref[...], ones_ref[...])


def _rwkv(u3, shift_prev, s0, p, tm):
    bn, t, _ = u3.shape
    sr = shift_prev[:, None, 0:W]
    sk = shift_prev[:, None, W:2 * W]
    sv = shift_prev[:, None, 2 * W:3 * W]
    sl = jnp.pad(shift_prev[:, None, 3 * W:], ((0, 0), (0, 0), (0, LORA_PAD - sum(A_LORA))))
    row = lambda c: pl.BlockSpec((None, 1, c), lambda b, i: (b, 0, 0))
    full = lambda a: pl.BlockSpec(a.shape, lambda b, i: (0,) * a.ndim)
    ublk = lambda slot: pl.BlockSpec((None, tm, W), lambda b, i, s=slot: (b, i, s))
    consts = (p["mu_rkv"], p["mu_l"], p["w0"], p["a0"], p["k_k"], p["k_a"],
              p["w2p"], p["a2p"], p["g2p"], p["head_ones"], p["r_k"], p["ln_w"], p["ln_b"])
    sblk = pl.BlockSpec((None, A_HEADS, A_HEAD, A_HEAD), lambda b, i: (b, 0, 0, 0))
    head_major = pltpu.VMEM((A_HEADS, tm, A_HEAD), F32)
    return pl.pallas_call(
        functools.partial(_rwkv_body, n_chunks=tm // RWKV_CHUNK),
        grid=(bn, t // tm),
        in_specs=[ublk(SLOT["ar"]), ublk(SLOT["ak"]), ublk(SLOT["av"]),
                  pl.BlockSpec((None, tm, LORA_PAD), lambda b, i: (b, i, COL_LORA // LORA_PAD)),
                  row(W), row(W), row(W), row(LORA_PAD)] + [full(c) for c in consts] + [sblk],
        out_specs=[pl.BlockSpec((None, tm, W), lambda b, i: (b, i, 0)), sblk],
        out_shape=[jax.ShapeDtypeStruct((bn, t, W), BF16),
                   jax.ShapeDtypeStruct((bn, A_HEADS, A_HEAD, A_HEAD), F32)],
        scratch_shapes=[pltpu.VMEM((8, W), F32)] * 3
                       + [pltpu.VMEM((8, LORA_PAD), F32), pltpu.VMEM((A_HEADS, A_HEAD, A_HEAD), F32)]
                       + [head_major] * 7,
        compiler_params=_params(("parallel", "arbitrary")),
        name="rwkv",
    )(u3, u3, u3, u3, sr, sk, sv, sl, *consts, s0)


def _pool_body(u_ref, hist_ref, w_ref, scale_ref, o_ref, carry, *, start_pos):
    tm = u_ref.shape[0]
    i = pl.program_id(1)

    @pl.when(i == 0)
    def _():
        carry[...] = hist_ref[...]

    x = u_ref[...]
    ext = jnp.concatenate([carry[...], x], axis=0)
    carry[...] = x[tm - 16:tm, :]
    s = ext
    sums = []
    for sh in (1, 2, 4, 8):
        s = s + pltpu.roll(s, sh, axis=0)
        sums.append(s[16:16 + tm, :])
    pos = start_pos + i * tm + lax.broadcasted_iota(jnp.int32, (tm, 1), 0)
    for gi, wlen in enumerate(B_WINDOWS):
        cs = slice(gi * LANE, (gi + 1) * LANE)
        cnt = jnp.minimum(wlen, pos + 1).astype(F32)
        d = sums[gi][:, cs] / cnt - x[:, cs]
        o_ref[:, cs] = (_dot(d.astype(BF16), w_ref[gi]) * scale_ref[:, cs]).astype(BF16)


def _pool(u3, hist, pool_w, pool_scale, start_pos, tm):
    bn, t, _ = u3.shape
    hist16 = jnp.pad(hist, ((0, 0), (1, 0), (0, 0)))
    return pl.pallas_call(
        functools.partial(_pool_body, start_pos=start_pos),
        grid=(bn, t // tm),
        in_specs=[pl.BlockSpec((None, tm, W), lambda b, i: (b, i, SLOT["b"])),
                  pl.BlockSpec((None, 16, W), lambda b, i: (b, 0, 0)),
                  pl.BlockSpec((4, LANE, LANE), lambda b, i: (0, 0, 0)),
                  pl.BlockSpec((1, W), lambda b, i: (0, 0))],
        out_specs=pl.BlockSpec((None, tm, W), lambda b, i: (b, i, 0)),
        out_shape=jax.ShapeDtypeStruct((bn, t, W), BF16),
        scratch_shapes=[pltpu.VMEM((16, W), F32)],
        compiler_params=_params(("parallel", "arbitrary")),
        name="pool",
    )(u3, hist16, pool_w.astype(BF16), pool_scale.reshape(1, W))


def _sb_body(q_ref, k_ref, v_ref, o_ref, acc_scr, o_scr, *, tq, q_off, n_masked):
    qi = pl.program_id(1)
    n_pairs = W // LANE
    q = q_ref[...] * (C_HEAD ** -0.5)
    q_start = q_off + qi * tq
    first_masked = q_start // SB_BLOCK
    even = lax.broadcasted_iota(jnp.int32, (tq, LANE), 1) < C_HEAD
    qz = jnp.zeros((tq, LANE), BF16)
    qs = jnp.stack([jnp.stack([jnp.where(even, q[:, p * LANE:(p + 1) * LANE], qz),
                               jnp.where(even, qz, q[:, p * LANE:(p + 1) * LANE])])
                    for p in range(n_pairs)])

    jj = lax.broadcasted_iota(jnp.int32, (2 * SB_BLOCK, 2 * SB_BLOCK), 0) % SB_BLOCK
    ss = lax.broadcasted_iota(jnp.int32, (2 * SB_BLOCK, 2 * SB_BLOCK), 1)
    cs = jnp.where(jnp.logical_or(ss >= SB_BLOCK, jj > ss), -1.0, 0.0).astype(BF16)
    aligned = q_off % SB_BLOCK == 0 and tq % SB_BLOCK == 0

    def block(kb, r0, masked):
        n = tq - r0
        ks = pl.multiple_of(kb * SB_BLOCK, SB_BLOCK)
        kblk = k_ref[pl.ds(ks, SB_BLOCK), :]
        vblk = v_ref[pl.ds(ks, SB_BLOCK), :]
        kst = jnp.stack([kblk[:, p * LANE:(p + 1) * LANE] for p in range(n_pairs)])
        vst = jnp.stack([vblk[:, p * LANE:(p + 1) * LANE] for p in range(n_pairs)])
        z = _bdot_nt(qs[:, :, r0:, :].reshape(n_pairs, 2 * n, LANE), kst).reshape(2 * n_pairs, n, SB_BLOCK)
        zp = jnp.maximum(z, 0.0)
        zn = jnp.minimum(z, 0.0)
        l1p = jnp.log(1.0 + jnp.exp(zn - zp))
        nlf = zp + l1p
        s = zn - l1p
        if masked:
            q_pos = q_start + r0 + lax.broadcasted_iota(jnp.int32, (n, SB_BLOCK), 0)
            mask = ((ks + lax.broadcasted_iota(jnp.int32, (n, SB_BLOCK), 1)) < q_pos)[None]
            nlf = jnp.where(mask, nlf, 0.0)
        hi, lo = _split2(nlf)
        lt = _dot(jnp.concatenate([hi, lo], axis=2).reshape(2 * n_pairs * n, 2 * SB_BLOCK), cs)
        lt = lt.reshape(2 * n_pairs, n, 2 * SB_BLOCK)
        att = jnp.exp(s + lt[:, :, :SB_BLOCK] + acc_scr[:, r0:, :])
        if masked:
            att = jnp.where(mask, att, 0.0)
        pv = _bdot(att.astype(BF16).reshape(n_pairs, 2 * n, SB_BLOCK), vst)
        o_scr[:, r0:, :] += pv.reshape(2 * n_pairs, n, LANE)
        acc_scr[:, r0:, :] += lt[:, :, SB_BLOCK:]

    acc_scr[...] = jnp.zeros_like(acc_scr)
    o_scr[...] = jnp.zeros_like(o_scr)
    for m in range(n_masked - 1, -1, -1):
        block(first_masked + m, m * SB_BLOCK if aligned else 0, True)

    def live(i):
        return jnp.logical_and(i < first_masked, jnp.max(acc_scr[...]) > SB_DEAD)

    def step(i):
        block(first_masked - 1 - i, 0, False)
        return i + 1

    lax.while_loop(live, step, jnp.int32(0))
    o = o_scr[...]
    o_ref[...] = jnp.concatenate([jnp.where(even, o[2 * p], o[2 * p + 1]) for p in range(n_pairs)],
                                 axis=1).astype(BF16)


def _to_bf16_body(*refs, n, n_f32):
    for x_ref, o_ref in zip(refs[:n], refs[n:2 * n]):
        o_ref[...] = x_ref[...].astype(BF16)
    for x_ref, o_ref in zip(refs[n - n_f32:n], refs[2 * n:]):
        o_ref[...] = x_ref[...]


def _to_bf16(x3, slots, tm, n_f32=0):
    bn, t, _ = x3.shape
    n = len(slots)
    oblk = pl.BlockSpec((None, tm, W), lambda b, i: (b, i, 0))
    return pl.pallas_call(
        functools.partial(_to_bf16_body, n=n, n_f32=n_f32),
        grid=(bn, t // tm),
        in_specs=[pl.BlockSpec((None, tm, W), lambda b, i, s=s: (b, i, s)) for s in slots],
        out_specs=[oblk] * (n + n_f32),
        out_shape=[jax.ShapeDtypeStruct((bn, t, W), BF16)] * n + [jax.ShapeDtypeStruct((bn, t, W), F32)] * n_f32,
        compiler_params=_params(("parallel", "parallel")),
        name="to_bf16",
    )(*([x3] * n))


def _sb_attention(q, k, v, q_off, tq):
    bn, t, _ = q.shape
    sk = k.shape[1]
    sk_pad = -(-sk // SB_BLOCK) * SB_BLOCK
    if sk_pad != sk:
        pad = ((0, 0), (0, sk_pad - sk), (0, 0))
        k, v = jnp.pad(k, pad), jnp.pad(v, pad)
    assert tq % SB_BLOCK == 0 or t == tq
    n_masked = -(-((q_off % SB_BLOCK) + tq) // SB_BLOCK)
    kv = pl.BlockSpec((None, sk_pad, W), lambda b, i: (b, 0, 0), pipeline_mode=pl.Buffered(1))
    qo = pl.BlockSpec((None, tq, W), lambda b, i: (b, i, 0))
    return pl.pallas_call(
        functools.partial(_sb_body, tq=tq, q_off=q_off, n_masked=n_masked),
        grid=(bn, t // tq),
        in_specs=[qo, kv, kv],
        out_specs=qo,
        out_shape=jax.ShapeDtypeStruct((bn, t, W), BF16),
        scratch_shapes=[pltpu.VMEM((C_HEADS, tq, SB_BLOCK), F32), pltpu.VMEM((C_HEADS, tq, LANE), F32)],
        compiler_params=_params(("parallel", "arbitrary")),
        name="sb_attention",
    )(q, k, v)


def _mlstm_body(qk_ref, v_ref, o_ref, g_ref, hist_ref, c0_ref, n0_ref, m0_ref,
                cw_ref, cb_ref, gb_ref, nw_ref,
                out_ref, conv_out, c_out, n_out, m_out,
                carry, q_scr, k_scr, c_scr, n_scr, m_scr, *, n_chunks):
    tm = qk_ref.shape[0]
    cl = tm // n_chunks

    @pl.when(pl.program_id(1) == 0)
    def _():
        carry[...] = hist_ref[...]
        c_scr[...] = c0_ref[...]
        n_scr[...] = n0_ref[...]
        m_scr[...] = m0_ref[...]

    x = qk_ref[...]
    ext = jnp.concatenate([carry[...], x], axis=0)
    carry[...] = x[tm - 8:tm, :]
    conv = cb_ref[...] + x * cw_ref[D_CONV - 1:D_CONV, :]
    for sh in range(1, D_CONV):
        conv = conv + pltpu.roll(ext, sh, axis=0)[8:8 + tm, :] * cw_ref[D_CONV - 1 - sh:D_CONV - sh, :]
    conv = conv * _sigmoid(conv)
    q_scr[...] = conv[:, :W]
    k_scr[...] = conv[:, W:] * (D_HEAD ** -0.5)

    row = lax.broadcasted_iota(jnp.int32, (cl, cl), 0)
    col = lax.broadcasted_iota(jnp.int32, (cl, cl), 1)
    causal = row >= col
    tri_incl = causal.astype(BF16)

    def chunk(c, carry_):
        off = pl.multiple_of(c * cl, cl)
        sl = pl.ds(off, cl)
        gpre = g_ref[sl, :] + gb_ref[...]
        lfa = _log_sigmoid(gpre)
        bcol = _dot_lhs01(tri_incl, lfa)
        g_t = gpre.T
        b_t = bcol.T
        heads = range(D_HEADS)
        hs = [slice(h * D_HEAD, (h + 1) * D_HEAD) for h in heads]
        q = jnp.stack([q_scr[sl, hs[h]] for h in heads])
        k = jnp.stack([k_scr[sl, hs[h]] for h in heads])
        v = jnp.stack([v_ref[sl, hs[h]] for h in heads])
        ig_col = jnp.stack([gpre[:, h:h + 1] for h in heads])
        b_col = jnp.stack([bcol[:, D_HEADS + h:D_HEADS + h + 1] for h in heads])
        ig_row = jnp.stack([g_t[h:h + 1, :] for h in heads])
        b_row = jnp.stack([b_t[D_HEADS + h:D_HEADS + h + 1, :] for h in heads])
        m_prev = jnp.stack([m_scr[h:h + 1, 0:1] for h in heads])
        n_prev = jnp.stack([n_scr[h:h + 1, :] for h in heads])
        c_prev = c_scr[...]
        log_d = jnp.where(causal[None], b_col - b_row + ig_row, -jnp.inf)
        m_inter = b_col + m_prev
        m_t = jnp.maximum(m_inter, jnp.max(log_d, axis=-1, keepdims=True))
        dmat = jnp.exp(log_d - m_t)
        inter = jnp.exp(m_inter - m_t)
        qb = q.astype(BF16)
        vb = v.astype(BF16)
        w_qk = _bdot_nt(qb, k.astype(BF16)) * dmat
        num = _bdot(w_qk.astype(BF16), vb) + inter * _bdot(qb, c_prev.astype(BF16))
        qn = jnp.sum(q * n_prev, axis=-1, keepdims=True)
        den = jnp.sum(w_qk, axis=-1, keepdims=True) + inter * qn
        den = jnp.maximum(jnp.abs(den), jnp.exp(-m_t))
        hv = num / den
        m_new = m_t[:, cl - 1:cl, :]
        b_last = b_col[:, cl - 1:cl, :]
        decay = jnp.exp(b_last + m_prev - m_new)
        w_s = jnp.exp(b_last - b_col + ig_col - m_new)
        kw = k * w_s
        c_scr[...] = decay * c_prev + _bdot_tn(kw.astype(BF16), vb)
        n_new = decay * n_prev + jnp.sum(kw, axis=1, keepdims=True)
        mu = jnp.mean(hv, axis=-1, keepdims=True)
        hc = hv - mu
        var = jnp.mean(hc * hc, axis=-1, keepdims=True)
        hn = hc * lax.rsqrt(var + D_NORM_EPS)
        for h in heads:
            n_scr[h:h + 1, :] = n_new[h]
            m_scr[h:h + 1, :] = jnp.broadcast_to(m_new[h], (1, LANE))
            out_ref[sl, hs[h]] = (_sigmoid(o_ref[sl, hs[h]]) * (hn[h] * nw_ref[:, hs[h]])).astype(BF16)
        return carry_

    lax.fori_loop(0, n_chunks, chunk, 0)
    conv_out[...] = carry[...]
    c_out[...] = c_scr[...]
    n_out[...] = n_scr[...]
    m_out[...] = m_scr[...]


def _mlstm(u3, conv_hist, c0, n0, m0, p, tm):
    bn, t, _ = u3.shape
    hist8 = jnp.pad(conv_hist, ((0, 0), (8 - (D_CONV - 1), 0), (0, 0)))
    n0p = jnp.pad(n0, ((0, 0), (0, 8 - D_HEADS), (0, 0)))
    m0p = jnp.pad(jnp.broadcast_to(m0[:, :, None], (bn, D_HEADS, LANE)), ((0, 0), (0, 8 - D_HEADS), (0, 0)))
    full = lambda a: pl.BlockSpec(a.shape, lambda b, i: (0,) * a.ndim)
    per_b = lambda *s: pl.BlockSpec((None,) + s, lambda b, i: (b,) + (0,) * len(s))
    consts = (p["conv_w"], p["conv_b"], p["gate_bias"], p["norm_w"])
    outs = pl.pallas_call(
        functools.partial(_mlstm_body, n_chunks=tm // min(MLSTM_CHUNK, tm)),
        grid=(bn, t // tm),
        in_specs=[pl.BlockSpec((None, tm, 2 * W), lambda b, i: (b, i, 0)),
                  pl.BlockSpec((None, tm, W), lambda b, i: (b, i, SLOT["dv"])),
                  pl.BlockSpec((None, tm, W), lambda b, i: (b, i, SLOT["do"])),
                  pl.BlockSpec((None, tm, LANE), lambda b, i: (b, i, COL_DG // LANE)),
                  per_b(8, 2 * W), per_b(D_HEADS, D_HEAD, D_HEAD), per_b(8, LANE), per_b(8, LANE)]
                 + [full(c) for c in consts],
        out_specs=[pl.BlockSpec((None, tm, W), lambda b, i: (b, i, 0)),
                   per_b(8, 2 * W), per_b(D_HEADS, D_HEAD, D_HEAD), per_b(8, LANE), per_b(8, LANE)],
        out_shape=[jax.ShapeDtypeStruct((bn, t, W), BF16),
                   jax.ShapeDtypeStruct((bn, 8, 2 * W), F32),
                   jax.ShapeDtypeStruct((bn, D_HEADS, D_HEAD, D_HEAD), F32),
                   jax.ShapeDtypeStruct((bn, 8, LANE), F32),
                   jax.ShapeDtypeStruct((bn, 8, LANE), F32)],
        scratch_shapes=[pltpu.VMEM((8, 2 * W), F32), pltpu.VMEM((tm, W), F32), pltpu.VMEM((tm, W), F32),
                        pltpu.VMEM((D_HEADS, D_HEAD, D_HEAD), F32), pltpu.VMEM((8, LANE), F32),
                        pltpu.VMEM((8, LANE), F32)],
        compiler_params=_params(("parallel", "arbitrary")),
        name="mlstm",
    )(u3, u3, u3, u3, hist8, c0, n0p, m0p, *consts)
    out_d, conv8, c, n8, m8 = outs
    return out_d, conv8[:, 8 - (D_CONV - 1):], c, n8[:, :D_HEADS], m8[:, :D_HEADS, 0]


def _merge_body(x_ref, g_ref, a_ref, b_ref, c_ref, d_ref, wg_ref, wb_ref, wo_ref, o_ref):
    g = g_ref[...].astype(BF16)
    merged = None
    for i, br in enumerate((a_ref, b_ref, c_ref, d_ref)):
        term = _sigmoid(_dot(g, wg_ref[i])) * _dot(br[...], wb_ref[i])
        merged = term if merged is None else merged + term
    o_ref[...] = x_ref[...] + _dot(merged.astype(BF16), wo_ref[...])


def _merge(x3, u3, out_a, out_b, out_c, out_d, wg, wb, wo, tm):
    bn, t, _ = x3.shape
    once = lambda a: pl.BlockSpec(a.shape, lambda b, i: (0,) * a.ndim, pipeline_mode=pl.Buffered(1))
    bblk = pl.BlockSpec((None, tm, W), lambda b, i: (b, i, 0))
    xblk = pl.BlockSpec((None, tm, D_MODEL), lambda b, i: (b, i, 0))
    return pl.pallas_call(
        _merge_body,
        grid=(bn, t // tm),
        in_specs=[xblk, pl.BlockSpec((None, tm, GATE_RANK), lambda b, i: (b, i, COL_G // GATE_RANK)),
                  bblk, bblk, bblk, bblk, once(wg), once(wb), once(wo)],
        out_specs=xblk,
        out_shape=jax.ShapeDtypeStruct((bn, t, D_MODEL), F32),
        compiler_params=_params(("parallel", "parallel")),
        name="merge",
    )(x3, u3, out_a, out_b, out_c, out_d, wg, wb, wo)


def _ffn_epilogue(acc, gf_ref, final_norm):
    return _rmsnorm(acc, gf_ref[...]) if final_norm else acc


def _ffn_dense_body(x_ref, g_ref, wg_ref, wu_ref, wd_ref, gf_ref, o_ref, h_scr, acc, *, final_norm):
    j = pl.program_id(1)

    @pl.when(j == 0)
    def _():
        x = x_ref[...]
        h_scr[...] = _rmsnorm(x, g_ref[...]).astype(BF16)
        acc[...] = x

    h = h_scr[...]
    a = _dot(h, wg_ref[...])
    act = (a * _sigmoid(a)) * _dot(h, wu_ref[...])
    acc[...] += _dot(act.astype(BF16), wd_ref[...])

    @pl.when(j == pl.num_programs(1) - 1)
    def _():
        o_ref[...] = _ffn_epilogue(acc[...], gf_ref, final_norm)


def _ffn_dense(x2d, g, wgate, wup, wdown, g_final, final_norm, tm, tf=512):
    n = x2d.shape[0]
    xblk = pl.BlockSpec((tm, D_MODEL), lambda i, j: (i, 0))
    vec = pl.BlockSpec((1, D_MODEL), lambda i, j: (0, 0))
    return pl.pallas_call(
        functools.partial(_ffn_dense_body, final_norm=final_norm),
        grid=(n // tm, D_FF // tf),
        in_specs=[xblk, vec,
                  pl.BlockSpec((D_MODEL, tf), lambda i, j: (0, j)),
                  pl.BlockSpec((D_MODEL, tf), lambda i, j: (0, j)),
                  pl.BlockSpec((tf, D_MODEL), lambda i, j: (j, 0)), vec],
        out_specs=xblk,
        out_shape=jax.ShapeDtypeStruct((n, D_MODEL), F32),
        scratch_shapes=[pltpu.VMEM((tm, D_MODEL), BF16), pltpu.VMEM((tm, D_MODEL), F32)],
        compiler_params=_params(("parallel", "arbitrary")),
        name="ffn_dense",
    )(x2d, g.reshape(1, D_MODEL), wgate, wup, wdown, g_final.reshape(1, D_MODEL))


def _ffn_moe_body(x_ref, g_ref, rt_ref, wg_ref, wu_ref, wd_ref, gf_ref, o_ref,
                  h_scr, acc, gates, *, final_norm):
    e = pl.program_id(1)

    @pl.when(e == 0)
    def _():
        x = x_ref[...]
        h = _rmsnorm(x, g_ref[...])
        h_scr[...] = h.astype(BF16)
        acc[...] = x
        h_hi, h_lo = _split2(h)
        r_hi, r_lo = _split2(rt_ref[...])
        logits = _dot(h_hi, r_hi) + _dot(h_lo, r_hi) + _dot(h_hi, r_lo)
        lane = lax.broadcasted_iota(jnp.int32, logits.shape, 1)
        logits = jnp.where(lane < N_EXPERTS, logits, -jnp.inf)
        v1 = jnp.max(logits, axis=-1, keepdims=True)
        i1 = jnp.min(jnp.where(logits == v1, lane, LANE), axis=-1, keepdims=True)
        rest = jnp.where(lane == i1, -jnp.inf, logits)
        v2 = jnp.max(rest, axis=-1, keepdims=True)
        i2 = jnp.min(jnp.where(rest == v2, lane, LANE), axis=-1, keepdims=True)
        e2 = jnp.exp(v2 - v1)
        den = 1.0 + e2
        gates[...] = jnp.where(lane == i1, 1.0 / den, 0.0) + jnp.where(lane == i2, e2 / den, 0.0)

    h = h_scr[...]
    lane = lax.broadcasted_iota(jnp.int32, gates.shape, 1)
    gate = jnp.sum(jnp.where(lane == e, gates[...], 0.0), axis=-1, keepdims=True)
    a = _dot(h, wg_ref[...])
    act = (a * _sigmoid(a)) * _dot(h, wu_ref[...])
    acc[...] += gate * _dot(act.astype(BF16), wd_ref[...])

    @pl.when(e == pl.num_programs(1) - 1)
    def _():
        o_ref[...] = _ffn_epilogue(acc[...], gf_ref, final_norm)


def _ffn_moe(x2d, g, router, wgate, wup, wdown, g_final, final_norm, tm):
    n = x2d.shape[0]
    xblk = pl.BlockSpec((tm, D_MODEL), lambda i, j: (i, 0))
    vec = pl.BlockSpec((1, D_MODEL), lambda i, j: (0, 0))
    return pl.pallas_call(
        functools.partial(_ffn_moe_body, final_norm=final_norm),
        grid=(n // tm, N_EXPERTS),
        in_specs=[xblk, vec,
                  pl.BlockSpec((D_MODEL, LANE), lambda i, j: (0, 0)),
                  pl.BlockSpec((None, D_MODEL, EXPERT_PAD), lambda i, j: (j, 0, 0)),
                  pl.BlockSpec((None, D_MODEL, EXPERT_PAD), lambda i, j: (j, 0, 0)),
                  pl.BlockSpec((None, EXPERT_PAD, D_MODEL), lambda i, j: (j, 0, 0)), vec],
        out_specs=xblk,
        out_shape=jax.ShapeDtypeStruct((n, D_MODEL), F32),
        scratch_shapes=[pltpu.VMEM((tm, D_MODEL), BF16), pltpu.VMEM((tm, D_MODEL), F32),
                        pltpu.VMEM((tm, LANE), F32)],
        compiler_params=_params(("parallel", "arbitrary")),
        name="ffn_moe",
    )(x2d, g.reshape(1, D_MODEL), router, wgate, wup, wdown, g_final.reshape(1, D_MODEL))


def _pack_w_in(w):
    a, b, c, d = w[:, :OFF_B], w[:, OFF_B:OFF_C], w[:, OFF_C:OFF_D], w[:, OFF_D:OFF_G]
    g = w[:, OFF_G:]
    zeros = lambda k: jnp.zeros((D_MODEL, k), w.dtype)
    lora = a[:, 3 * W:]
    gates = d[:, 4 * W:]
    cols = [d[:, :4 * W], a[:, :3 * W], b, c,
            lora, zeros(LORA_PAD - lora.shape[1]), g,
            gates, zeros(LANE - gates.shape[1]), zeros(WP - COL_DG - LANE)]
    return jnp.concatenate(cols, axis=1).astype(BF16)


def _layer_params(l, wts):
    (norm_mix, norm_ffn, norm_final, w_in, rwkv_mu, rwkv_w0, rwkv_w2, rwkv_a0, rwkv_a2,
     rwkv_g2, rwkv_k_k, rwkv_k_a, rwkv_r_k, rwkv_ln_w, rwkv_ln_b, pool_w, pool_scale,
     mlstm_conv_w, mlstm_conv_b, mlstm_i_bias, mlstm_f_bias, mlstm_norm_w,
     w_branch, w_merge_gate, w_out, ffn_w_gate, ffn_w_up, ffn_w_down,
     moe_router, moe_w_gate, moe_w_up, moe_w_down) = wts
    n_l = sum(A_LORA)
    row = lambda v: v.reshape(1, -1)

    def lora_pad(wm, start):
        return jnp.pad(wm, ((start, LORA_PAD - start - wm.shape[0]), (0, 0))).astype(BF16)

    hid = jnp.arange(W) // A_HEAD
    p = {
        "norm_mix": norm_mix[l], "norm_ffn": norm_ffn[l],
        "wp": _pack_w_in(w_in[l]),
        "rwkv": {
            "mu_rkv": rwkv_mu[l, :3 * W].reshape(3, W),
            "mu_l": jnp.pad(rwkv_mu[l, 3 * W:], (0, LORA_PAD - n_l)).reshape(1, LORA_PAD),
            "w0": row(rwkv_w0[l]), "a0": row(rwkv_a0[l]),
            "k_k": row(rwkv_k_k[l]), "k_a": row(rwkv_k_a[l]),
            "w2p": lora_pad(rwkv_w2[l], 0),
            "a2p": lora_pad(rwkv_a2[l], A_LORA[0]),
            "g2p": lora_pad(rwkv_g2[l], A_LORA[0] + A_LORA[1]),
            "head_ones": (hid[:, None] == hid[None, :]).astype(BF16),
            "r_k": row(rwkv_r_k[l]), "ln_w": row(rwkv_ln_w[l]), "ln_b": row(rwkv_ln_b[l]),
        },
        "pool_w": pool_w[l], "pool_scale": pool_scale[l],
        "mlstm": {
            "conv_w": mlstm_conv_w[l], "conv_b": row(mlstm_conv_b[l]),
            "gate_bias": jnp.pad(jnp.concatenate([mlstm_i_bias[l], mlstm_f_bias[l]]),
                                 (0, LANE - 2 * D_HEADS)).reshape(1, LANE),
            "norm_w": row(mlstm_norm_w[l]),
        },
        "wg": w_merge_gate[l].astype(BF16), "wb": w_branch[l].astype(BF16), "wo": w_out[l].astype(BF16),
    }
    if l % 2 == 0:
        p["ffn"] = (ffn_w_gate[l // 2].astype(BF16), ffn_w_up[l // 2].astype(BF16),
                    ffn_w_down[l // 2].astype(BF16))
    else:
        pe = EXPERT_PAD - D_FF_EXPERT
        p["moe"] = (jnp.pad(moe_router[l // 2], ((0, 0), (0, LANE - N_EXPERTS))),
                    jnp.pad(moe_w_gate[l // 2], ((0, 0), (0, 0), (0, pe))).astype(BF16),
                    jnp.pad(moe_w_up[l // 2], ((0, 0), (0, 0), (0, pe))).astype(BF16),
                    jnp.pad(moe_w_down[l // 2], ((0, 0), (0, pe), (0, 0))).astype(BF16))
    return p


def _slot(u3, name):
    s = SLOT[name] * W
    return u3[:, :, s:s + W]


def _run_trunk(x, start_pos, cache_k, cache_v, wkv0, shift0, pool0, conv0, c0, n0, m0,
               layers, norm_final):
    bn, t, _ = x.shape
    n = bn * t
    tm_tok = min(TOK_TILE, t)
    tm_row = min(ROW_TILE, n)
    tm_cast = min(ROW_TILE, t)
    x2 = x.reshape(n, D_MODEL)
    ks, vs, wkvs, shifts, pools, convs, cs, ns, ms = ([] for _ in range(9))
    for l, p in enumerate(layers):
        u2 = _inproj(x2, p["norm_mix"], p["wp"])
        u3 = u2.reshape(bn, t, WP)

        out_a, wkv = _rwkv(u3, shift0[l], wkv0[l], p["rwkv"], tm_tok)

        out_b = _pool(u3, pool0[l], p["pool_w"], p["pool_scale"], start_pos, tm_tok)

        q_b, k_b, v_b, k_c, v_c = _to_bf16(u3, (SLOT["cq"], SLOT["ck"], SLOT["cv"]), tm_cast, n_f32=2)
        if cache_k is None:
            out_c = _sb_attention(q_b, k_b, v_b, 0, min(SB_TQ, t))
        else:
            past = cache_k.shape[2]
            kc_b = cache_k[l].astype(BF16).reshape(bn, past, W)
            vc_b = cache_v[l].astype(BF16).reshape(bn, past, W)
            out_c = _sb_attention(q_b, jnp.concatenate([kc_b, k_b], axis=1),
                                  jnp.concatenate([vc_b, v_b], axis=1), past, t)

        out_d, conv_new, c_new, n_new, m_new = _mlstm(u3, conv0[l], c0[l], n0[l], m0[l], p["mlstm"], tm_tok)

        rows = lambda a: a.reshape(1, n, a.shape[-1])
        x2 = _merge(rows(x2), rows(u3), rows(out_a), rows(out_b), rows(out_c), rows(out_d),
                    p["wg"], p["wb"], p["wo"], min(TOK_TILE, n)).reshape(n, D_MODEL)
        last = l == len(layers) - 1
        if "ffn" in p:
            x2 = _ffn_dense(x2, p["norm_ffn"], *p["ffn"], norm_final, last, tm_row)
        else:
            x2 = _ffn_moe(x2, p["norm_ffn"], *p["moe"], norm_final, last, tm_row)

        ks.append(k_c.reshape(bn, t, C_HEADS, C_HEAD))
        vs.append(v_c.reshape(bn, t, C_HEADS, C_HEAD))
        wkvs.append(wkv)
        u_last = u3[:, -1]
        shifts.append(jnp.concatenate(
            [u_last[:, SLOT["ar"] * W:SLOT["ar"] * W + 3 * W], u_last[:, COL_LORA:COL_LORA + sum(A_LORA)]], axis=1))
        pools.append(jnp.concatenate([pool0[l], _slot(u3, "b")[:, -B_HIST:]], axis=1)[:, -B_HIST:])
        convs.append(conv_new)
        cs.append(c_new)
        ns.append(n_new)
        ms.append(m_new)
    st = lambda lst: jnp.stack(lst, axis=0)
    return (x2.reshape(bn, t, D_MODEL),
            (st(ks), st(vs), st(wkvs), st(shifts), st(pools), st(convs), st(cs), st(ns), st(ms)))


def kernel(x_prompt, x_sample, cache_sb_k, cache_sb_v, state_rwkv_wkv, state_rwkv_shift, state_pool,
           state_mlstm_conv, state_mlstm_c, state_mlstm_n, state_mlstm_m, norm_mix, norm_ffn, norm_final,
           w_in, rwkv_mu, rwkv_w0, rwkv_w2, rwkv_a0, rwkv_a2, rwkv_g2, rwkv_k_k, rwkv_k_a, rwkv_r_k,
           rwkv_ln_w, rwkv_ln_b, pool_w, pool_scale, mlstm_conv_w, mlstm_conv_b, mlstm_i_bias,
           mlstm_f_bias, mlstm_norm_w, w_branch, w_merge_gate, w_out, ffn_w_gate, ffn_w_up, ffn_w_down,
           moe_router, moe_w_gate, moe_w_up, moe_w_down):
    wts = (norm_mix, norm_ffn, norm_final, w_in, rwkv_mu, rwkv_w0, rwkv_w2, rwkv_a0, rwkv_a2,
           rwkv_g2, rwkv_k_k, rwkv_k_a, rwkv_r_k, rwkv_ln_w, rwkv_ln_b, pool_w, pool_scale,
           mlstm_conv_w, mlstm_conv_b, mlstm_i_bias, mlstm_f_bias, mlstm_norm_w,
           w_branch, w_merge_gate, w_out, ffn_w_gate, ffn_w_up, ffn_w_down,
           moe_router, moe_w_gate, moe_w_up, moe_w_down)
    depth = w_in.shape[0]
    layers = [_layer_params(l, wts) for l in range(depth)]
    bp = x_prompt.shape[0]
    z = lambda *s: jnp.zeros((depth, bp) + s, F32)
    y_p, st_p = _run_trunk(
        x_prompt, 0, None, None, z(A_HEADS, A_HEAD, A_HEAD), z(A_IN), z(B_HIST, W),
        z(D_CONV - 1, 2 * W), z(D_HEADS, D_HEAD, D_HEAD), z(D_HEADS, D_HEAD), z(D_HEADS),
        layers, norm_final)
    y_s, st_s = _run_trunk(
        x_sample, cache_sb_k.shape[2], cache_sb_k, cache_sb_v, state_rwkv_wkv, state_rwkv_shift,
        state_pool, state_mlstm_conv, state_mlstm_c, state_mlstm_n, state_mlstm_m,
        layers, norm_final)
    return (y_p, y_s) + tuple(st_p) + tuple(st_s)
```

```python
import functools

import jax
import jax.numpy as jnp
from jax import lax
from jax.experimental import pallas as pl
from jax.experimental.pallas import tpu as pltpu

F32 = jnp.float32
BF16 = jnp.bfloat16

D_MODEL = 2048
DEPTH = 2
W = 512
A_HEAD, A_HEADS = 64, 8
A_LORA = (32, 32, 96)
A_IN = 3 * W + sum(A_LORA)
A_GN_EPS = 64e-5
B_WINDOWS = (2, 4, 8, 16)
B_HIST = 15
C_HEAD, C_HEADS = 64, 8
D_HEADS, D_HEAD = 4, 128
D_CONV = 4
D_NORM_EPS = 1e-6
GATE_RANK = 256
OFF_B = A_IN
OFF_C = OFF_B + W
OFF_D = OFF_C + 3 * W
OFF_G = OFF_D + 4 * W + 2 * D_HEADS
IN_WIDTH = OFF_G + GATE_RANK
D_FF = 5632
N_EXPERTS = 8
D_FF_EXPERT = D_FF // N_EXPERTS
NORM_EPS = 1e-6

LANE = 128
V7X_VMEM_BYTES = 64 * 1024 * 1024
VMEM_LIMIT = V7X_VMEM_BYTES - 8 * 1024 * 1024

SLOT = {name: i for i, name in enumerate(
    ("dq", "dk", "dv", "do", "ar", "ak", "av", "b", "cq", "ck", "cv"))}
COL_LORA = 11 * W
COL_G = COL_LORA + 256
COL_DG = COL_G + 256
IN_TN = 1280
IN_TM = 1024
WP = 6400
LORA_PAD = 256
EXPERT_PAD = 768

TOK_TILE = 256
ROW_TILE = 512
RWKV_CHUNK = 64
MLSTM_CHUNK = 128
SB_BLOCK = 128
SB_TQ = 256
SB_DEAD = -120.0


def _params(sem, vmem=VMEM_LIMIT):
    return pltpu.CompilerParams(dimension_semantics=sem, vmem_limit_bytes=vmem)


def _split2(x):
    hi = x.astype(BF16)
    lo = (x - hi.astype(F32)).astype(BF16)
    return hi, lo


def _split3(x):
    hi = x.astype(BF16)
    r1 = x - hi.astype(F32)
    mid = r1.astype(BF16)
    lo = (r1 - mid.astype(F32)).astype(BF16)
    return hi, mid, lo


def _dot(a, b):
    return jnp.dot(a, b, preferred_element_type=F32)


def _dot_nt(a, b):
    return lax.dot_general(a, b, (((1,), (1,)), ((), ())), preferred_element_type=F32)


def _bdot(a, b):
    return lax.dot_general(a, b, (((2,), (1,)), ((0,), (0,))), preferred_element_type=F32)


def _bdot_nt(a, b):
    return lax.dot_general(a, b, (((2,), (2,)), ((0,), (0,))), preferred_element_type=F32)


def _bdot_tn(a, b):
    return lax.dot_general(a, b, (((1,), (1,)), ((0,), (0,))), preferred_element_type=F32)


def _dot_lhs01(m01, x):
    hi, mid, lo = _split3(x)
    return _dot(m01, hi) + _dot(m01, mid) + _dot(m01, lo)


def _sigmoid(x):
    return 1.0 / (1.0 + jnp.exp(-x))


def _log_sigmoid(x):
    return jnp.minimum(x, 0.0) - jnp.log1p(jnp.exp(-jnp.abs(x)))


def _rmsnorm(x, g):
    ms = jnp.mean(x * x, axis=-1, keepdims=True)
    return x * lax.rsqrt(ms + NORM_EPS) * g


def _inproj_body(x_ref, g_ref, w_ref, o_ref, h_scr):
    @pl.when(pl.program_id(1) == 0)
    def _():
        h_scr[...] = _rmsnorm(x_ref[...], g_ref[...]).astype(BF16)

    o_ref[...] = _dot(h_scr[...], w_ref[...])


def _inproj(x2d, g, wp):
    n = x2d.shape[0]
    tm = min(IN_TM, n)
    return pl.pallas_call(
        _inproj_body,
        grid=(n // tm, WP // IN_TN),
        in_specs=[pl.BlockSpec((tm, D_MODEL), lambda i, j: (i, 0)),
                  pl.BlockSpec((1, D_MODEL), lambda i, j: (0, 0)),
                  pl.BlockSpec((D_MODEL, IN_TN), lambda i, j: (0, j))],
        out_specs=pl.BlockSpec((tm, IN_TN), lambda i, j: (i, j)),
        out_shape=jax.ShapeDtypeStruct((n, WP), F32),
        scratch_shapes=[pltpu.VMEM((tm, D_MODEL), BF16)],
        compiler_params=_params(("parallel", "arbitrary")),
        name="inproj",
    )(x2d, g.reshape(1, D_MODEL), wp)


def _shift_rows(u, carry_row):
    rolled = pltpu.roll(u, 1, axis=0)
    row = lax.broadcasted_iota(jnp.int32, u.shape, 0)
    return jnp.where(row == 0, carry_row, rolled)


def _rwkv_prep(ur_ref, uk_ref, uv_ref, ul_ref, sr_ref, sk_ref, sv_ref, sl_ref,
               mu_ref, mul_ref, w0_ref, a0_ref, kk_ref, ka_ref, w2_ref, a2_ref, g2_ref, ones_ref,
               cr, ck, cv, cl):
    tm = ur_ref.shape[0]

    @pl.when(pl.program_id(1) == 0)
    def _():
        cr[0:1, :] = sr_ref[...]
        ck[0:1, :] = sk_ref[...]
        cv[0:1, :] = sv_ref[...]
        cl[0:1, :] = sl_ref[...]

    def mix(u_ref, c_ref, mu):
        u = u_ref[...]
        prev = _shift_rows(u, c_ref[0:1, :])
        c_ref[0:1, :] = u[tm - 1:tm, :]
        return u + (prev - u) * mu

    r = mix(ur_ref, cr, mu_ref[0:1, :])
    k = mix(uk_ref, ck, mu_ref[1:2, :])
    v = mix(uv_ref, cv, mu_ref[2:3, :])
    xl = mix(ul_ref, cl, mul_ref[...])

    w_lora = _dot(jnp.tanh(xl).astype(BF16), w2_ref[...])
    a_lora = _dot(xl.astype(BF16), a2_ref[...])
    g = _dot(_sigmoid(xl).astype(BF16), g2_ref[...])

    y = -(w0_ref[...] + w_lora)
    w_log = -(jnp.maximum(y, 0.0) + jnp.log1p(jnp.exp(-jnp.abs(y)))) - 0.5
    log_decay = -jnp.exp(w_log)
    a = _sigmoid(a0_ref[...] + a_lora)

    kk = k * kk_ref[...]
    hi, lo = _split2(kk * kk)
    ss = _dot(hi, ones_ref[...]) + _dot(lo, ones_ref[...])
    kk = kk / jnp.maximum(jnp.sqrt(ss), 1e-12)
    k2 = k * (1.0 + (a - 1.0) * ka_ref[...])

    return r, k2, v, kk, kk * a, log_decay, g


def _rwkv_chunks(r_ref, k_ref, v_ref, kk_ref, b_ref, lw_ref, y_ref, s_scr, n_chunks):
    c_len = RWKV_CHUNK
    hh = A_HEADS
    nb = hh * n_chunks
    row = lax.broadcasted_iota(jnp.int32, (1, c_len, c_len), 1)
    col = lax.broadcasted_iota(jnp.int32, (1, c_len, c_len), 2)
    tri_incl = jnp.broadcast_to((row >= col).astype(BF16), (nb, c_len, c_len))
    lower = row >= col
    strict = row > col
    eye = (row == col).astype(F32)

    load = lambda ref: ref[...].reshape(nb, c_len, A_HEAD)
    r, k, v, kk, b, logw = (load(ref) for ref in (r_ref, k_ref, v_ref, kk_ref, b_ref, lw_ref))
    w_hi, w_mid, w_lo = _split3(logw)
    lw = _bdot(tri_incl, w_hi) + _bdot(tri_incl, w_mid) + _bdot(tri_incl, w_lo)
    lw_prev = lw - logw
    lw_last = lw[:, c_len - 1:c_len, :]
    e_neg = jnp.exp(-lw)
    e_end = jnp.exp(lw_last - lw)
    kkm = (kk * jnp.exp(lw_prev)).astype(BF16)
    rm = r * jnp.exp(lw)
    kp = (k * e_neg).astype(BF16)
    bp = (b * e_neg).astype(BF16)
    kpp = (k * e_end).astype(BF16)
    bpp = (b * e_end).astype(BF16)
    vb = v.astype(BF16)
    rmb = rm.astype(BF16)

    a_vk = jnp.where(strict, _bdot_nt(kkm, kp), 0.0)
    a_pb = jnp.where(strict, _bdot_nt(kkm, bp), 0.0)
    rk = jnp.where(lower, _bdot_nt(rmb, kp), 0.0)
    rb = jnp.where(lower, _bdot_nt(rmb, bp), 0.0)

    n_pow = -a_pb
    t_inv = eye + n_pow
    for _ in range(5):
        npb = n_pow.astype(BF16)
        n_pow = _bdot(npb, npb)
        t_inv = t_inv + _bdot(t_inv.astype(BF16), n_pow.astype(BF16))
    tb = t_inv.astype(BF16)

    kktb = _bdot(tb, kkm).astype(BF16)
    pvb = _bdot(tb, _bdot(a_vk.astype(BF16), vb).astype(BF16)).astype(BF16)
    rbb = rb.astype(BF16)
    per_chunk = lambda x: x.reshape((hh, n_chunks) + x.shape[1:])
    m_mat = per_chunk(eye * jnp.exp(lw_last) - _bdot_tn(kktb, bpp))
    n_mat = per_chunk(_bdot_tn(vb, kpp) - _bdot_tn(pvb, bpp))
    q_mat = per_chunk((rm - _bdot(rbb, kktb)).astype(BF16))
    y0 = per_chunk(_bdot(rk.astype(BF16), vb) - _bdot(rbb, pvb))

    s = s_scr[...]
    for g in range(n_chunks):
        s_hi, s_lo = _split2(s)
        m_hi, m_lo = _split2(m_mat[:, g])
        y_ref[:, g * c_len:(g + 1) * c_len, :] = _bdot_nt(q_mat[:, g], s_hi) + y0[:, g]
        s = _bdot(s_hi, m_hi) + _bdot(s_lo, m_hi) + _bdot(s_hi, m_lo) + n_mat[:, g]
    s_scr[...] = s


def _rwkv_post(y, r, k, v, g, r_k, ln_w, ln_b, ones):
    def head_sum(x):
        hi, lo = _split2(x)
        return _dot(hi, ones) + _dot(lo, ones)

    mu = head_sum(y) * (1.0 / A_HEAD)
    yc = y - mu
    var = head_sum(yc * yc) * (1.0 / A_HEAD)
    yn = yc * lax.rsqrt(var + A_GN_EPS) * ln_w + ln_b
    bonus = head_sum(r * k * r_k) * v
    return ((yn + bonus) * g).astype(BF16)


def _rwkv_body(ur_ref, uk_ref, uv_ref, ul_ref, sr_ref, sk_ref, sv_ref, sl_ref,
               mu_ref, mul_ref, w0_ref, a0_ref, kk_ref, ka_ref, w2_ref, a2_ref, g2_ref, ones_ref,
               rk_ref, lnw_ref, lnb_ref, s0_ref,
               o_ref, sout_ref,
               cr, ck, cv, cl, s_scr, r_hm, k_hm, v_hm, kk_hm, b_hm, lw_hm, y_hm, *, n_chunks):
    @pl.when(pl.program_id(1) == 0)
    def _():
        s_scr[...] = s0_ref[...]

    r, k, v, kk, b, lw, g = _rwkv_prep(
        ur_ref, uk_ref, uv_ref, ul_ref, sr_ref, sk_ref, sv_ref, sl_ref, mu_ref, mul_ref, w0_ref, a0_ref,
        kk_ref, ka_ref, w2_ref, a2_ref, g2_ref, ones_ref, cr, ck, cv, cl)
    for x, ref in ((r, r_hm), (k, k_hm), (v, v_hm), (kk, kk_hm), (b, b_hm), (lw, lw_hm)):
        for h in range(A_HEADS):
            ref[h] = x[:, h * A_HEAD:(h + 1) * A_HEAD]
    _rwkv_chunks(r_hm, k_hm, v_hm, kk_hm, b_hm, lw_hm, y_hm, s_scr, n_chunks)
    sout_ref[...] = s_scr[...]
    y = jnp.concatenate([y_hm[h] for h in range(A_HEADS)], axis=1)
    o_ref[...] = _rwkv_post(y, r, k, v, g, rk_ref[...], lnw_ref[...], lnb_ref[...], ones_ref[...])


def _rwkv(u3, shift_prev, s0, p, tm):
    bn, t, _ = u3.shape
    sr = shift_prev[:, None, 0:W]
    sk = shift_prev[:, None, W:2 * W]
    sv = shift_prev[:, None, 2 * W:3 * W]
    sl = jnp.pad(shift_prev[:, None, 3 * W:], ((0, 0), (0, 0), (0, LORA_PAD - sum(A_LORA))))
    row = lambda c: pl.BlockSpec((None, 1, c), lambda b, i: (b, 0, 0))
    full = lambda a: pl.BlockSpec(a.shape, lambda b, i: (0,) * a.ndim)
    ublk = lambda slot: pl.BlockSpec((None, tm, W), lambda b, i, s=slot: (b, i, s))
    consts = (p["mu_rkv"], p["mu_l"], p["w0"], p["a0"], p["k_k"], p["k_a"],
              p["w2p"], p["a2p"], p["g2p"], p["head_ones"], p["r_k"], p["ln_w"], p["ln_b"])
    sblk = pl.BlockSpec((None, A_HEADS, A_HEAD, A_HEAD), lambda b, i: (b, 0, 0, 0))
    head_major = pltpu.VMEM((A_HEADS, tm, A_HEAD), F32)
    return pl.pallas_call(
        functools.partial(_rwkv_body, n_chunks=tm // RWKV_CHUNK),
        grid=(bn, t // tm),
        in_specs=[ublk(SLOT["ar"]), ublk(SLOT["ak"]), ublk(SLOT["av"]),
                  pl.BlockSpec((None, tm, LORA_PAD), lambda b, i: (b, i, COL_LORA // LORA_PAD)),
                  row(W), row(W), row(W), row(LORA_PAD)] + [full(c) for c in consts] + [sblk],
        out_specs=[pl.BlockSpec((None, tm, W), lambda b, i: (b, i, 0)), sblk],
        out_shape=[jax.ShapeDtypeStruct((bn, t, W), BF16),
                   jax.ShapeDtypeStruct((bn, A_HEADS, A_HEAD, A_HEAD), F32)],
        scratch_shapes=[pltpu.VMEM((8, W), F32)] * 3
                       + [pltpu.VMEM((8, LORA_PAD), F32), pltpu.VMEM((A_HEADS, A_HEAD, A_HEAD), F32)]
                       + [head_major] * 7,
        compiler_params=_params(("parallel", "arbitrary")),
        name="rwkv",
    )(u3, u3, u3, u3, sr, sk, sv, sl, *consts, s0)


def _pool_body(u_ref, hist_ref, w_ref, scale_ref, o_ref, carry, *, start_pos):
    tm = u_ref.shape[0]
    i = pl.program_id(1)

    @pl.when(i == 0)
    def _():
        carry[...] = hist_ref[...]

    x = u_ref[...]
    ext = jnp.concatenate([carry[...], x], axis=0)
    carry[...] = x[tm - 16:tm, :]
    s = ext
    sums = []
    for sh in (1, 2, 4, 8):
        s = s + pltpu.roll(s, sh, axis=0)
        sums.append(s[16:16 + tm, :])
    pos = start_pos + i * tm + lax.broadcasted_iota(jnp.int32, (tm, 1), 0)
    for gi, wlen in enumerate(B_WINDOWS):
        cs = slice(gi * LANE, (gi + 1) * LANE)
        cnt = jnp.minimum(wlen, pos + 1).astype(F32)
        d = sums[gi][:, cs] / cnt - x[:, cs]
        o_ref[:, cs] = (_dot(d.astype(BF16), w_ref[gi]) * scale_ref[:, cs]).astype(BF16)


def _pool(u3, hist, pool_w, pool_scale, start_pos, tm):
    bn, t, _ = u3.shape
    hist16 = jnp.pad(hist, ((0, 0), (1, 0), (0, 0)))
    return pl.pallas_call(
        functools.partial(_pool_body, start_pos=start_pos),
        grid=(bn, t // tm),
        in_specs=[pl.BlockSpec((None, tm, W), lambda b, i: (b, i, SLOT["b"])),
                  pl.BlockSpec((None, 16, W), lambda b, i: (b, 0, 0)),
                  pl.BlockSpec((4, LANE, LANE), lambda b, i: (0, 0, 0)),
                  pl.BlockSpec((1, W), lambda b, i: (0, 0))],
        out_specs=pl.BlockSpec((None, tm, W), lambda b, i: (b, i, 0)),
        out_shape=jax.ShapeDtypeStruct((bn, t, W), BF16),
        scratch_shapes=[pltpu.VMEM((16, W), F32)],
        compiler_params=_params(("parallel", "arbitrary")),
        name="pool",
    )(u3, hist16, pool_w.astype(BF16), pool_scale.reshape(1, W))


def _sb_body(q_ref, k_ref, v_ref, o_ref, acc_scr, o_scr, *, tq, q_off, n_masked):
    qi = pl.program_id(1)
    n_pairs = W // LANE
    q = q_ref[...] * (C_HEAD ** -0.5)
    q_start = q_off + qi * tq
    first_masked = q_start // SB_BLOCK
    even = lax.broadcasted_iota(jnp.int32, (tq, LANE), 1) < C_HEAD
    qz = jnp.zeros((tq, LANE), BF16)
    qs = jnp.stack([jnp.stack([jnp.where(even, q[:, p * LANE:(p + 1) * LANE], qz),
                               jnp.where(even, qz, q[:, p * LANE:(p + 1) * LANE])])
                    for p in range(n_pairs)])

    jj = lax.broadcasted_iota(jnp.int32, (2 * SB_BLOCK, 2 * SB_BLOCK), 0) % SB_BLOCK
    ss = lax.broadcasted_iota(jnp.int32, (2 * SB_BLOCK, 2 * SB_BLOCK), 1)
    cs = jnp.where(jnp.logical_or(ss >= SB_BLOCK, jj > ss), -1.0, 0.0).astype(BF16)
    aligned = q_off % SB_BLOCK == 0 and tq % SB_BLOCK == 0

    def block(kb, r0, r1, masked):
        n = r1 - r0
        ks = pl.multiple_of(kb * SB_BLOCK, SB_BLOCK)
        kblk = k_ref[pl.ds(ks, SB_BLOCK), :]
        vblk = v_ref[pl.ds(ks, SB_BLOCK), :]
        kst = jnp.stack([kblk[:, p * LANE:(p + 1) * LANE] for p in range(n_pairs)])
        vst = jnp.stack([vblk[:, p * LANE:(p + 1) * LANE] for p in range(n_pairs)])
        z = _bdot_nt(qs[:, :, r0:r1, :].reshape(n_pairs, 2 * n, LANE), kst).reshape(2 * n_pairs, n, SB_BLOCK)
        zp = jnp.maximum(z, 0.0)
        zn = jnp.minimum(z, 0.0)
        l1p = jnp.log(1.0 + jnp.exp(zn - zp))
        nlf = zp + l1p
        s = zn - l1p
        if masked:
            q_pos = q_start + r0 + lax.broadcasted_iota(jnp.int32, (n, SB_BLOCK), 0)
            mask = ((ks + lax.broadcasted_iota(jnp.int32, (n, SB_BLOCK), 1)) < q_pos)[None]
            nlf = jnp.where(mask, nlf, 0.0)
        hi, lo = _split2(nlf)
        lt = _dot(jnp.concatenate([hi, lo], axis=2).reshape(2 * n_pairs * n, 2 * SB_BLOCK), cs)
        lt = lt.reshape(2 * n_pairs, n, 2 * SB_BLOCK)
        att = jnp.exp(s + lt[:, :, :SB_BLOCK] + acc_scr[:, r0:r1, :])
        if masked:
            att = jnp.where(mask, att, 0.0)
        pv = _bdot(att.astype(BF16).reshape(n_pairs, 2 * n, SB_BLOCK), vst)
        o_scr[:, r0:r1, :] += pv.reshape(2 * n_pairs, n, LANE)
        acc_scr[:, r0:r1, :] += lt[:, :, SB_BLOCK:]

    acc_scr[...] = jnp.zeros_like(acc_scr)
    o_scr[...] = jnp.zeros_like(o_scr)
    for m in range(n_masked - 1, -1, -1):
        block(first_masked + m, m * SB_BLOCK if aligned else 0, tq, True)

    def sweep(i0, r0, r1, alive_rows):
        def live(i):
            return jnp.logical_and(i < first_masked, jnp.max(acc_scr[:, alive_rows, :]) > SB_DEAD)

        def step(i):
            block(first_masked - 1 - i, r0, r1, False)
            return i + 1

        return lax.while_loop(live, step, i0)

    split = SB_BLOCK if aligned and tq > SB_BLOCK else 0
    i_next = sweep(jnp.int32(0), 0, tq, slice(split, tq))
    if split:
        sweep(i_next, 0, split, slice(0, split))
    o = o_scr[...]
    o_ref[...] = jnp.concatenate([jnp.where(even, o[2 * p], o[2 * p + 1]) for p in range(n_pairs)],
                                 axis=1).astype(BF16)


def _to_bf16_body(*refs, n, n_f32):
    for x_ref, o_ref in zip(refs[:n], refs[n:2 * n]):
        o_ref[...] = x_ref[...].astype(BF16)
    for x_ref, o_ref in zip(refs[n - n_f32:n], refs[2 * n:]):
        o_ref[...] = x_ref[...]


def _to_bf16(x3, slots, tm, n_f32=0):
    bn, t, _ = x3.shape
    n = len(slots)
    oblk = pl.BlockSpec((None, tm, W), lambda b, i: (b, i, 0))
    return pl.pallas_call(
        functools.partial(_to_bf16_body, n=n, n_f32=n_f32),
        grid=(bn, t // tm),
        in_specs=[pl.BlockSpec((None, tm, W), lambda b, i, s=s: (b, i, s)) for s in slots],
        out_specs=[oblk] * (n + n_f32),
        out_shape=[jax.ShapeDtypeStruct((bn, t, W), BF16)] * n + [jax.ShapeDtypeStruct((bn, t, W), F32)] * n_f32,
        compiler_params=_params(("parallel", "parallel")),
        name="to_bf16",
    )(*([x3] * n))


def _sb_attention(q, k, v, q_off, tq):
    bn, t, _ = q.shape
    sk = k.shape[1]
    sk_pad = -(-sk // SB_BLOCK) * SB_BLOCK
    if sk_pad != sk:
        pad = ((0, 0), (0, sk_pad - sk), (0, 0))
        k, v = jnp.pad(k, pad), jnp.pad(v, pad)
    assert tq % SB_BLOCK == 0 or t == tq
    n_masked = -(-((q_off % SB_BLOCK) + tq) // SB_BLOCK)
    kv = pl.BlockSpec((None, sk_pad, W), lambda b, i: (b, 0, 0), pipeline_mode=pl.Buffered(1))
    qo = pl.BlockSpec((None, tq, W), lambda b, i: (b, i, 0))
    return pl.pallas_call(
        functools.partial(_sb_body, tq=tq, q_off=q_off, n_masked=n_masked),
        grid=(bn, t // tq),
        in_specs=[qo, kv, kv],
        out_specs=qo,
        out_shape=jax.ShapeDtypeStruct((bn, t, W), BF16),
        scratch_shapes=[pltpu.VMEM((C_HEADS, tq, SB_BLOCK), F32), pltpu.VMEM((C_HEADS, tq, LANE), F32)],
        compiler_params=_params(("parallel", "arbitrary")),
        name="sb_attention",
    )(q, k, v)


def _mlstm_body(qk_ref, v_ref, o_ref, g_ref, hist_ref, c0_ref, n0_ref, m0_ref,
                cw_ref, cb_ref, gb_ref, nw_ref,
                out_ref, conv_out, c_out, n_out, m_out,
                carry, q_scr, k_scr, c_scr, n_scr, m_scr, *, n_chunks):
    tm = qk_ref.shape[0]
    cl = tm // n_chunks

    @pl.when(pl.program_id(1) == 0)
    def _():
        carry[...] = hist_ref[...]
        c_scr[...] = c0_ref[...]
        n_scr[...] = n0_ref[...]
        m_scr[...] = m0_ref[...]

    x = qk_ref[...]
    ext = jnp.concatenate([carry[...], x], axis=0)
    carry[...] = x[tm - 8:tm, :]
    conv = cb_ref[...] + x * cw_ref[D_CONV - 1:D_CONV, :]
    for sh in range(1, D_CONV):
        conv = conv + pltpu.roll(ext, sh, axis=0)[8:8 + tm, :] * cw_ref[D_CONV - 1 - sh:D_CONV - sh, :]
    conv = conv * _sigmoid(conv)
    q_scr[...] = conv[:, :W]
    k_scr[...] = conv[:, W:] * (D_HEAD ** -0.5)

    row = lax.broadcasted_iota(jnp.int32, (cl, cl), 0)
    col = lax.broadcasted_iota(jnp.int32, (cl, cl), 1)
    causal = row >= col
    tri_incl = causal.astype(BF16)

    def chunk(c, carry_):
        off = pl.multiple_of(c * cl, cl)
        sl = pl.ds(off, cl)
        gpre = g_ref[sl, :] + gb_ref[...]
        lfa = _log_sigmoid(gpre)
        bcol = _dot_lhs01(tri_incl, lfa)
        g_t = gpre.T
        b_t = bcol.T
        heads = range(D_HEADS)
        hs = [slice(h * D_HEAD, (h + 1) * D_HEAD) for h in heads]
        q = jnp.stack([q_scr[sl, hs[h]] for h in heads])
        k = jnp.stack([k_scr[sl, hs[h]] for h in heads])
        v = jnp.stack([v_ref[sl, hs[h]] for h in heads])
        ig_col = jnp.stack([gpre[:, h:h + 1] for h in heads])
        b_col = jnp.stack([bcol[:, D_HEADS + h:D_HEADS + h + 1] for h in heads])
        ig_row = jnp.stack([g_t[h:h + 1, :] for h in heads])
        b_row = jnp.stack([b_t[D_HEADS + h:D_HEADS + h + 1, :] for h in heads])
        m_prev = jnp.stack([m_scr[h:h + 1, 0:1] for h in heads])
        n_prev = jnp.stack([n_scr[h:h + 1, :] for h in heads])
        c_prev = c_scr[...]
        log_d = jnp.where(causal[None], b_col - b_row + ig_row, -jnp.inf)
        m_inter = b_col + m_prev
        m_t = jnp.maximum(m_inter, jnp.max(log_d, axis=-1, keepdims=True))
        dmat = jnp.exp(log_d - m_t)
        inter = jnp.exp(m_inter - m_t)
        qb = q.astype(BF16)
        vb = v.astype(BF16)
        w_qk = _bdot_nt(qb, k.astype(BF16)) * dmat
        num = _bdot(w_qk.astype(BF16), vb) + inter * _bdot(qb, c_prev.astype(BF16))
        qn = jnp.sum(q * n_prev, axis=-1, keepdims=True)
        den = jnp.sum(w_qk, axis=-1, keepdims=True) + inter * qn
        den = jnp.maximum(jnp.abs(den), jnp.exp(-m_t))
        hv = num / den
        m_new = m_t[:, cl - 1:cl, :]
        b_last = b_col[:, cl - 1:cl, :]
        decay = jnp.exp(b_last + m_prev - m_new)
        w_s = jnp.exp(b_last - b_col + ig_col - m_new)
        kw = k * w_s
        c_scr[...] = decay * c_prev + _bdot_tn(kw.astype(BF16), vb)
        n_new = decay * n_prev + jnp.sum(kw, axis=1, keepdims=True)
        mu = jnp.mean(hv, axis=-1, keepdims=True)
        hc = hv - mu
        var = jnp.mean(hc * hc, axis=-1, keepdims=True)
        hn = hc * lax.rsqrt(var + D_NORM_EPS)
        for h in heads:
            n_scr[h:h + 1, :] = n_new[h]
            m_scr[h:h + 1, :] = jnp.broadcast_to(m_new[h], (1, LANE))
            out_ref[sl, hs[h]] = (_sigmoid(o_ref[sl, hs[h]]) * (hn[h] * nw_ref[:, hs[h]])).astype(BF16)
        return carry_

    lax.fori_loop(0, n_chunks, chunk, 0)
    conv_out[...] = carry[...]
    c_out[...] = c_scr[...]
    n_out[...] = n_scr[...]
    m_out[...] = m_scr[...]


def _mlstm(u3, conv_hist, c0, n0, m0, p, tm):
    bn, t, _ = u3.shape
    hist8 = jnp.pad(conv_hist, ((0, 0), (8 - (D_CONV - 1), 0), (0, 0)))
    n0p = jnp.pad(n0, ((0, 0), (0, 8 - D_HEADS), (0, 0)))
    m0p = jnp.pad(jnp.broadcast_to(m0[:, :, None], (bn, D_HEADS, LANE)), ((0, 0), (0, 8 - D_HEADS), (0, 0)))
    full = lambda a: pl.BlockSpec(a.shape, lambda b, i: (0,) * a.ndim)
    per_b = lambda *s: pl.BlockSpec((None,) + s, lambda b, i: (b,) + (0,) * len(s))
    consts = (p["conv_w"], p["conv_b"], p["gate_bias"], p["norm_w"])
    outs = pl.pallas_call(
        functools.partial(_mlstm_body, n_chunks=tm // min(MLSTM_CHUNK, tm)),
        grid=(bn, t // tm),
        in_specs=[pl.BlockSpec((None, tm, 2 * W), lambda b, i: (b, i, 0)),
                  pl.BlockSpec((None, tm, W), lambda b, i: (b, i, SLOT["dv"])),
                  pl.BlockSpec((None, tm, W), lambda b, i: (b, i, SLOT["do"])),
                  pl.BlockSpec((None, tm, LANE), lambda b, i: (b, i, COL_DG // LANE)),
                  per_b(8, 2 * W), per_b(D_HEADS, D_HEAD, D_HEAD), per_b(8, LANE), per_b(8, LANE)]
                 + [full(c) for c in consts],
        out_specs=[pl.BlockSpec((None, tm, W), lambda b, i: (b, i, 0)),
                   per_b(8, 2 * W), per_b(D_HEADS, D_HEAD, D_HEAD), per_b(8, LANE), per_b(8, LANE)],
        out_shape=[jax.ShapeDtypeStruct((bn, t, W), BF16),
                   jax.ShapeDtypeStruct((bn, 8, 2 * W), F32),
                   jax.ShapeDtypeStruct((bn, D_HEADS, D_HEAD, D_HEAD), F32),
                   jax.ShapeDtypeStruct((bn, 8, LANE), F32),
                   jax.ShapeDtypeStruct((bn, 8, LANE), F32)],
        scratch_shapes=[pltpu.VMEM((8, 2 * W), F32), pltpu.VMEM((tm, W), F32), pltpu.VMEM((tm, W), F32),
                        pltpu.VMEM((D_HEADS, D_HEAD, D_HEAD), F32), pltpu.VMEM((8, LANE), F32),
                        pltpu.VMEM((8, LANE), F32)],
        compiler_params=_params(("parallel", "arbitrary")),
        name="mlstm",
    )(u3, u3, u3, u3, hist8, c0, n0p, m0p, *consts)
    out_d, conv8, c, n8, m8 = outs
    return out_d, conv8[:, 8 - (D_CONV - 1):], c, n8[:, :D_HEADS], m8[:, :D_HEADS, 0]


def _merge_body(x_ref, g_ref, a_ref, b_ref, c_ref, d_ref, wg_ref, wb_ref, wo_ref, o_ref):
    g = g_ref[...].astype(BF16)
    merged = None
    for i, br in enumerate((a_ref, b_ref, c_ref, d_ref)):
        term = _sigmoid(_dot(g, wg_ref[i])) * _dot(br[...], wb_ref[i])
        merged = term if merged is None else merged + term
    o_ref[...] = x_ref[...] + _dot(merged.astype(BF16), wo_ref[...])


def _merge(x3, u3, out_a, out_b, out_c, out_d, wg, wb, wo, tm):
    bn, t, _ = x3.shape
    once = lambda a: pl.BlockSpec(a.shape, lambda b, i: (0,) * a.ndim, pipeline_mode=pl.Buffered(1))
    bblk = pl.BlockSpec((None, tm, W), lambda b, i: (b, i, 0))
    xblk = pl.BlockSpec((None, tm, D_MODEL), lambda b, i: (b, i, 0))
    return pl.pallas_call(
        _merge_body,
        grid=(bn, t // tm),
        in_specs=[xblk, pl.BlockSpec((None, tm, GATE_RANK), lambda b, i: (b, i, COL_G // GATE_RANK)),
                  bblk, bblk, bblk, bblk, once(wg), once(wb), once(wo)],
        out_specs=xblk,
        out_shape=jax.ShapeDtypeStruct((bn, t, D_MODEL), F32),
        compiler_params=_params(("parallel", "parallel")),
        name="merge",
    )(x3, u3, out_a, out_b, out_c, out_d, wg, wb, wo)


def _ffn_epilogue(acc, gf_ref, final_norm):
    return _rmsnorm(acc, gf_ref[...]) if final_norm else acc


def _ffn_dense_body(x_ref, g_ref, wg_ref, wu_ref, wd_ref, gf_ref, o_ref, h_scr, acc, *, final_norm):
    j = pl.program_id(1)

    @pl.when(j == 0)
    def _():
        x = x_ref[...]
        h_scr[...] = _rmsnorm(x, g_ref[...]).astype(BF16)
        acc[...] = x

    h = h_scr[...]
    a = _dot(h, wg_ref[...])
    act = (a * _sigmoid(a)) * _dot(h, wu_ref[...])
    acc[...] += _dot(act.astype(BF16), wd_ref[...])

    @pl.when(j == pl.num_programs(1) - 1)
    def _():
        o_ref[...] = _ffn_epilogue(acc[...], gf_ref, final_norm)


def _ffn_dense(x2d, g, wgate, wup, wdown, g_final, final_norm, tm, tf=512):
    n = x2d.shape[0]
    xblk = pl.BlockSpec((tm, D_MODEL), lambda i, j: (i, 0))
    vec = pl.BlockSpec((1, D_MODEL), lambda i, j: (0, 0))
    return pl.pallas_call(
        functools.partial(_ffn_dense_body, final_norm=final_norm),
        grid=(n // tm, D_FF // tf),
        in_specs=[xblk, vec,
                  pl.BlockSpec((D_MODEL, tf), lambda i, j: (0, j)),
                  pl.BlockSpec((D_MODEL, tf), lambda i, j: (0, j)),
                  pl.BlockSpec((tf, D_MODEL), lambda i, j: (j, 0)), vec],
        out_specs=xblk,
        out_shape=jax.ShapeDtypeStruct((n, D_MODEL), F32),
        scratch_shapes=[pltpu.VMEM((tm, D_MODEL), BF16), pltpu.VMEM((tm, D_MODEL), F32)],
        compiler_params=_params(("parallel", "arbitrary")),
        name="ffn_dense",
    )(x2d, g.reshape(1, D_MODEL), wgate, wup, wdown, g_final.reshape(1, D_MODEL))


def _ffn_moe_body(x_ref, g_ref, rt_ref, wg_ref, wu_ref, wd_ref, gf_ref, o_ref,
                  h_scr, acc, gates, *, final_norm):
    e = pl.program_id(1)

    @pl.when(e == 0)
    def _():
        x = x_ref[...]
        h = _rmsnorm(x, g_ref[...])
        h_scr[...] = h.astype(BF16)
        acc[...] = x
        h_hi, h_lo = _split2(h)
        r_hi, r_lo = _split2(rt_ref[...])
        logits = _dot(h_hi, r_hi) + _dot(h_lo, r_hi) + _dot(h_hi, r_lo)
        lane = lax.broadcasted_iota(jnp.int32, logits.shape, 1)
        logits = jnp.where(lane < N_EXPERTS, logits, -jnp.inf)
        v1 = jnp.max(logits, axis=-1, keepdims=True)
        i1 = jnp.min(jnp.where(logits == v1, lane, LANE), axis=-1, keepdims=True)
        rest = jnp.where(lane == i1, -jnp.inf, logits)
        v2 = jnp.max(rest, axis=-1, keepdims=True)
        i2 = jnp.min(jnp.where(rest == v2, lane, LANE), axis=-1, keepdims=True)
        e2 = jnp.exp(v2 - v1)
        den = 1.0 + e2
        gates[...] = jnp.where(lane == i1, 1.0 / den, 0.0) + jnp.where(lane == i2, e2 / den, 0.0)

    h = h_scr[...]
    lane = lax.broadcasted_iota(jnp.int32, gates.shape, 1)
    gate = jnp.sum(jnp.where(lane == e, gates[...], 0.0), axis=-1, keepdims=True)
    a = _dot(h, wg_ref[...])
    act = (a * _sigmoid(a)) * _dot(h, wu_ref[...])
    acc[...] += gate * _dot(act.astype(BF16), wd_ref[...])

    @pl.when(e == pl.num_programs(1) - 1)
    def _():
        o_ref[...] = _ffn_epilogue(acc[...], gf_ref, final_norm)


def _ffn_moe(x2d, g, router, wgate, wup, wdown, g_final, final_norm, tm):
    n = x2d.shape[0]
    xblk = pl.BlockSpec((tm, D_MODEL), lambda i, j: (i, 0))
    vec = pl.BlockSpec((1, D_MODEL), lambda i, j: (0, 0))
    return pl.pallas_call(
        functools.partial(_ffn_moe_body, final_norm=final_norm),
        grid=(n // tm, N_EXPERTS),
        in_specs=[xblk, vec,
                  pl.BlockSpec((D_MODEL, LANE), lambda i, j: (0, 0)),
                  pl.BlockSpec((None, D_MODEL, EXPERT_PAD), lambda i, j: (j, 0, 0)),
                  pl.BlockSpec((None, D_MODEL, EXPERT_PAD), lambda i, j: (j, 0, 0)),
                  pl.BlockSpec((None, EXPERT_PAD, D_MODEL), lambda i, j: (j, 0, 0)), vec],
        out_specs=xblk,
        out_shape=jax.ShapeDtypeStruct((n, D_MODEL), F32),
        scratch_shapes=[pltpu.VMEM((tm, D_MODEL), BF16), pltpu.VMEM((tm, D_MODEL), F32),
                        pltpu.VMEM((tm, LANE), F32)],
        compiler_params=_params(("parallel", "arbitrary")),
        name="ffn_moe",
    )(x2d, g.reshape(1, D_MODEL), router, wgate, wup, wdown, g_final.reshape(1, D_MODEL))


def _pack_w_in(w):
    a, b, c, d = w[:, :OFF_B], w[:, OFF_B:OFF_C], w[:, OFF_C:OFF_D], w[:, OFF_D:OFF_G]
    g = w[:, OFF_G:]
    zeros = lambda k: jnp.zeros((D_MODEL, k), w.dtype)
    lora = a[:, 3 * W:]
    gates = d[:, 4 * W:]
    cols = [d[:, :4 * W], a[:, :3 * W], b, c,
            lora, zeros(LORA_PAD - lora.shape[1]), g,
            gates, zeros(LANE - gates.shape[1]), zeros(WP - COL_DG - LANE)]
    return jnp.concatenate(cols, axis=1).astype(BF16)


def _layer_params(l, wts):
    (norm_mix, norm_ffn, norm_final, w_in, rwkv_mu, rwkv_w0, rwkv_w2, rwkv_a0, rwkv_a2,
     rwkv_g2, rwkv_k_k, rwkv_k_a, rwkv_r_k, rwkv_ln_w, rwkv_ln_b, pool_w, pool_scale,
     mlstm_conv_w, mlstm_conv_b, mlstm_i_bias, mlstm_f_bias, mlstm_norm_w,
     w_branch, w_merge_gate, w_out, ffn_w_gate, ffn_w_up, ffn_w_down,
     moe_router, moe_w_gate, moe_w_up, moe_w_down) = wts
    n_l = sum(A_LORA)
    row = lambda v: v.reshape(1, -1)

    def lora_pad(wm, start):
        return jnp.pad(wm, ((start, LORA_PAD - start - wm.shape[0]), (0, 0))).astype(BF16)

    hid = jnp.arange(W) // A_HEAD
    p = {
        "norm_mix": norm_mix[l], "norm_ffn": norm_ffn[l],
        "wp": _pack_w_in(w_in[l]),
        "rwkv": {
            "mu_rkv": rwkv_mu[l, :3 * W].reshape(3, W),
            "mu_l": jnp.pad(rwkv_mu[l, 3 * W:], (0, LORA_PAD - n_l)).reshape(1, LORA_PAD),
            "w0": row(rwkv_w0[l]), "a0": row(rwkv_a0[l]),
            "k_k": row(rwkv_k_k[l]), "k_a": row(rwkv_k_a[l]),
            "w2p": lora_pad(rwkv_w2[l], 0),
            "a2p": lora_pad(rwkv_a2[l], A_LORA[0]),
            "g2p": lora_pad(rwkv_g2[l], A_LORA[0] + A_LORA[1]),
            "head_ones": (hid[:, None] == hid[None, :]).astype(BF16),
            "r_k": row(rwkv_r_k[l]), "ln_w": row(rwkv_ln_w[l]), "ln_b": row(rwkv_ln_b[l]),
        },
        "pool_w": pool_w[l], "pool_scale": pool_scale[l],
        "mlstm": {
            "conv_w": mlstm_conv_w[l], "conv_b": row(mlstm_conv_b[l]),
            "gate_bias": jnp.pad(jnp.concatenate([mlstm_i_bias[l], mlstm_f_bias[l]]),
                                 (0, LANE - 2 * D_HEADS)).reshape(1, LANE),
            "norm_w": row(mlstm_norm_w[l]),
        },
        "wg": w_merge_gate[l].astype(BF16), "wb": w_branch[l].astype(BF16), "wo": w_out[l].astype(BF16),
    }
    if l % 2 == 0:
        p["ffn"] = (ffn_w_gate[l // 2].astype(BF16), ffn_w_up[l // 2].astype(BF16),
                    ffn_w_down[l // 2].astype(BF16))
    else:
        pe = EXPERT_PAD - D_FF_EXPERT
        p["moe"] = (jnp.pad(moe_router[l // 2], ((0, 0), (0, LANE - N_EXPERTS))),
                    jnp.pad(moe_w_gate[l // 2], ((0, 0), (0, 0), (0, pe))).astype(BF16),
                    jnp.pad(moe_w_up[l // 2], ((0, 0), (0, 0), (0, pe))).astype(BF16),
                    jnp.pad(moe_w_down[l // 2], ((0, 0), (0, pe), (0, 0))).astype(BF16))
    return p


def _slot(u3, name):
    s = SLOT[name] * W
    return u3[:, :, s:s + W]


def _run_trunk(x, start_pos, cache_k, cache_v, wkv0, shift0, pool0, conv0, c0, n0, m0,
               layers, norm_final):
    bn, t, _ = x.shape
    n = bn * t
    tm_tok = min(TOK_TILE, t)
    tm_row = min(ROW_TILE, n)
    tm_cast = min(ROW_TILE, t)
    x2 = x.reshape(n, D_MODEL)
    ks, vs, wkvs, shifts, pools, convs, cs, ns, ms = ([] for _ in range(9))
    for l, p in enumerate(layers):
        u2 = _inproj(x2, p["norm_mix"], p["wp"])
        u3 = u2.reshape(bn, t, WP)

        out_a, wkv = _rwkv(u3, shift0[l], wkv0[l], p["rwkv"], tm_tok)

        out_b = _pool(u3, pool0[l], p["pool_w"], p["pool_scale"], start_pos, tm_tok)

        q_b, k_b, v_b, k_c, v_c = _to_bf16(u3, (SLOT["cq"], SLOT["ck"], SLOT["cv"]), tm_cast, n_f32=2)
        if cache_k is None:
            out_c = _sb_attention(q_b, k_b, v_b, 0, min(SB_TQ, t))
        else:
            past = cache_k.shape[2]
            kc_b = cache_k[l].astype(BF16).reshape(bn, past, W)
            vc_b = cache_v[l].astype(BF16).reshape(bn, past, W)
            out_c = _sb_attention(q_b, jnp.concatenate([kc_b, k_b], axis=1),
                                  jnp.concatenate([vc_b, v_b], axis=1), past, t)

        out_d, conv_new, c_new, n_new, m_new = _mlstm(u3, conv0[l], c0[l], n0[l], m0[l], p["mlstm"], tm_tok)

        rows = lambda a: a.reshape(1, n, a.shape[-1])
        x2 = _merge(rows(x2), rows(u3), rows(out_a), rows(out_b), rows(out_c), rows(out_d),
                    p["wg"], p["wb"], p["wo"], min(TOK_TILE, n)).reshape(n, D_MODEL)
        last = l == len(layers) - 1
        if "ffn" in p:
            x2 = _ffn_dense(x2, p["norm_ffn"], *p["ffn"], norm_final, last, tm_row)
        else:
            x2 = _ffn_moe(x2, p["norm_ffn"], *p["moe"], norm_final, last, tm_row)

        ks.append(k_c.reshape(bn, t, C_HEADS, C_HEAD))
        vs.append(v_c.reshape(bn, t, C_HEADS, C_HEAD))
        wkvs.append(wkv)
        u_last = u3[:, -1]
        shifts.append(jnp.concatenate(
            [u_last[:, SLOT["ar"] * W:SLOT["ar"] * W + 3 * W], u_last[:, COL_LORA:COL_LORA + sum(A_LORA)]], axis=1))
        pools.append(jnp.concatenate([pool0[l], _slot(u3, "b")[:, -B_HIST:]], axis=1)[:, -B_HIST:])
        convs.append(conv_new)
        cs.append(c_new)
        ns.append(n_new)
        ms.append(m_new)
    st = lambda lst: jnp.stack(lst, axis=0)
    return (x2.reshape(bn, t, D_MODEL),
            (st(ks), st(vs), st(wkvs), st(shifts), st(pools), st(convs), st(cs), st(ns), st(ms)))


def kernel(x_prompt, x_sample, cache_sb_k, cache_sb_v, state_rwkv_wkv, state_rwkv_shift, state_pool,
           state_mlstm_conv, state_mlstm_c, state_mlstm_n, state_mlstm_m, norm_mix, norm_ffn, norm_final,
           w_in, rwkv_mu, rwkv_w0, rwkv_w2, rwkv_a0, rwkv_a2, rwkv_g2, rwkv_k_k, rwkv_k_a, rwkv_r_k,
           rwkv_ln_w, rwkv_ln_b, pool_w, pool_scale, mlstm_conv_w, mlstm_conv_b, mlstm_i_bias,
           mlstm_f_bias, mlstm_norm_w, w_branch, w_merge_gate, w_out, ffn_w_gate, ffn_w_up, ffn_w_down,
           moe_router, moe_w_gate, moe_w_up, moe_w_down):
    wts = (norm_mix, norm_ffn, norm_final, w_in, rwkv_mu, rwkv_w0, rwkv_w2, rwkv_a0, rwkv_a2,
           rwkv_g2, rwkv_k_k, rwkv_k_a, rwkv_r_k, rwkv_ln_w, rwkv_ln_b, pool_w, pool_scale,
           mlstm_conv_w, mlstm_conv_b, mlstm_i_bias, mlstm_f_bias, mlstm_norm_w,
           w_branch, w_merge_gate, w_out, ffn_w_gate, ffn_w_up, ffn_w_down,
           moe_router, moe_w_gate, moe_w_up, moe_w_down)
    depth = w_in.shape[0]
    layers = [_layer_params(l, wts) for l in range(depth)]
    bp = x_prompt.shape[0]
    z = lambda *s: jnp.zeros((depth, bp) + s, F32)
    y_p, st_p = _run_trunk(
        x_prompt, 0, None, None, z(A_HEADS, A_HEAD, A_HEAD), z(A_IN), z(B_HIST, W),
        z(D_CONV - 1, 2 * W), z(D_HEADS, D_HEAD, D_HEAD), z(D_HEADS, D_HEAD), z(D_HEADS),
        layers, norm_final)
    y_s, st_s = _run_trunk(
        x_sample, cache_sb_k.shape[2], cache_sb_k, cache_sb_v, state_rwkv_wkv, state_rwkv_shift,
        state_pool, state_mlstm_conv, state_mlstm_c, state_mlstm_n, state_mlstm_m,
        layers, norm_final)
    return (y_p, y_s) + tuple(st_p) + tuple(st_s)
```

```python
import functools

import jax
import jax.numpy as jnp
from jax import lax
from jax.experimental import pallas as pl
from jax.experimental.pallas import tpu as pltpu

F32 = jnp.float32
BF16 = jnp.bfloat16

D_MODEL = 2048
DEPTH = 2
W = 512
A_HEAD, A_HEADS = 64, 8
A_LORA = (32, 32, 96)
A_IN = 3 * W + sum(A_LORA)
A_GN_EPS = 64e-5
B_WINDOWS = (2, 4, 8, 16)
B_HIST = 15
C_HEAD, C_HEADS = 64, 8
D_HEADS, D_HEAD = 4, 128
D_CONV = 4
D_NORM_EPS = 1e-6
GATE_RANK = 256
OFF_B = A_IN
OFF_C = OFF_B + W
OFF_D = OFF_C + 3 * W
OFF_G = OFF_D + 4 * W + 2 * D_HEADS
IN_WIDTH = OFF_G + GATE_RANK
D_FF = 5632
N_EXPERTS = 8
D_FF_EXPERT = D_FF // N_EXPERTS
NORM_EPS = 1e-6

LANE = 128
V7X_VMEM_BYTES = 64 * 1024 * 1024
VMEM_LIMIT = V7X_VMEM_BYTES - 8 * 1024 * 1024

SLOT = {name: i for i, name in enumerate(
    ("dq", "dk", "dv", "do", "ar", "ak", "av", "b", "cq", "ck", "cv"))}
COL_LORA = 11 * W
COL_G = COL_LORA + 256
COL_DG = COL_G + 256
IN_TN = 1280
IN_TM = 1024
WP = 6400
LORA_PAD = 256
EXPERT_PAD = 768

TOK_TILE = 256
ROW_TILE = 512
RWKV_CHUNK = 64
MLSTM_CHUNK = 256
SB_BLOCK = 128
SB_TQ = 256
SB_DEAD = -120.0


def _params(sem, vmem=VMEM_LIMIT):
    return pltpu.CompilerParams(dimension_semantics=sem, vmem_limit_bytes=vmem)


def _split2(x):
    hi = x.astype(BF16)
    lo = (x - hi.astype(F32)).astype(BF16)
    return hi, lo


def _split3(x):
    hi = x.astype(BF16)
    r1 = x - hi.astype(F32)
    mid = r1.astype(BF16)
    lo = (r1 - mid.astype(F32)).astype(BF16)
    return hi, mid, lo


def _dot(a, b):
    return jnp.dot(a, b, preferred_element_type=F32)


def _dot_nt(a, b):
    return lax.dot_general(a, b, (((1,), (1,)), ((), ())), preferred_element_type=F32)


def _bdot(a, b):
    return lax.dot_general(a, b, (((2,), (1,)), ((0,), (0,))), preferred_element_type=F32)


def _bdot_nt(a, b):
    return lax.dot_general(a, b, (((2,), (2,)), ((0,), (0,))), preferred_element_type=F32)


def _bdot_tn(a, b):
    return lax.dot_general(a, b, (((1,), (1,)), ((0,), (0,))), preferred_element_type=F32)


def _dot_lhs01(m01, x):
    hi, mid, lo = _split3(x)
    return _dot(m01, hi) + _dot(m01, mid) + _dot(m01, lo)


def _sigmoid(x):
    return 1.0 / (1.0 + jnp.exp(-x))


def _log_sigmoid(x):
    return jnp.minimum(x, 0.0) - jnp.log1p(jnp.exp(-jnp.abs(x)))


def _rmsnorm(x, g):
    ms = jnp.mean(x * x, axis=-1, keepdims=True)
    return x * lax.rsqrt(ms + NORM_EPS) * g


def _inproj_body(x_ref, g_ref, w_ref, o_ref, h_scr):
    @pl.when(pl.program_id(1) == 0)
    def _():
        h_scr[...] = _rmsnorm(x_ref[...], g_ref[...]).astype(BF16)

    o_ref[...] = _dot(h_scr[...], w_ref[...])


def _inproj(x2d, g, wp):
    n = x2d.shape[0]
    tm = min(IN_TM, n)
    return pl.pallas_call(
        _inproj_body,
        grid=(n // tm, WP // IN_TN),
        in_specs=[pl.BlockSpec((tm, D_MODEL), lambda i, j: (i, 0)),
                  pl.BlockSpec((1, D_MODEL), lambda i, j: (0, 0)),
                  pl.BlockSpec((D_MODEL, IN_TN), lambda i, j: (0, j))],
        out_specs=pl.BlockSpec((tm, IN_TN), lambda i, j: (i, j)),
        out_shape=jax.ShapeDtypeStruct((n, WP), F32),
        scratch_shapes=[pltpu.VMEM((tm, D_MODEL), BF16)],
        compiler_params=_params(("parallel", "arbitrary")),
        name="inproj",
    )(x2d, g.reshape(1, D_MODEL), wp)


def _shift_rows(u, carry_row):
    rolled = pltpu.roll(u, 1, axis=0)
    row = lax.broadcasted_iota(jnp.int32, u.shape, 0)
    return jnp.where(row == 0, carry_row, rolled)


def _rwkv_prep(ur_ref, uk_ref, uv_ref, ul_ref, sr_ref, sk_ref, sv_ref, sl_ref,
               mu_ref, mul_ref, w0_ref, a0_ref, kk_ref, ka_ref, w2_ref, a2_ref, g2_ref, ones_ref,
               cr, ck, cv, cl):
    tm = ur_ref.shape[0]

    @pl.when(pl.program_id(1) == 0)
    def _():
        cr[0:1, :] = sr_ref[...]
        ck[0:1, :] = sk_ref[...]
        cv[0:1, :] = sv_ref[...]
        cl[0:1, :] = sl_ref[...]

    def mix(u_ref, c_ref, mu):
        u = u_ref[...]
        prev = _shift_rows(u, c_ref[0:1, :])
        c_ref[0:1, :] = u[tm - 1:tm, :]
        return u + (prev - u) * mu

    r = mix(ur_ref, cr, mu_ref[0:1, :])
    k = mix(uk_ref, ck, mu_ref[1:2, :])
    v = mix(uv_ref, cv, mu_ref[2:3, :])
    xl = mix(ul_ref, cl, mul_ref[...])

    w_lora = _dot(jnp.tanh(xl).astype(BF16), w2_ref[...])
    a_lora = _dot(xl.astype(BF16), a2_ref[...])
    g = _dot(_sigmoid(xl).astype(BF16), g2_ref[...])

    y = -(w0_ref[...] + w_lora)
    w_log = -(jnp.maximum(y, 0.0) + jnp.log1p(jnp.exp(-jnp.abs(y)))) - 0.5
    log_decay = -jnp.exp(w_log)
    a = _sigmoid(a0_ref[...] + a_lora)

    kk = k * kk_ref[...]
    hi, lo = _split2(kk * kk)
    ss = _dot(hi, ones_ref[...]) + _dot(lo, ones_ref[...])
    kk = kk / jnp.maximum(jnp.sqrt(ss), 1e-12)
    k2 = k * (1.0 + (a - 1.0) * ka_ref[...])

    return r, k2, v, kk, kk * a, log_decay, g


def _rwkv_chunks(r_ref, k_ref, v_ref, kk_ref, b_ref, lw_ref, y_ref, s_scr, n_chunks):
    c_len = RWKV_CHUNK
    hh = A_HEADS
    nb = hh * n_chunks
    row = lax.broadcasted_iota(jnp.int32, (1, c_len, c_len), 1)
    col = lax.broadcasted_iota(jnp.int32, (1, c_len, c_len), 2)
    tri_incl = jnp.broadcast_to((row >= col).astype(BF16), (nb, c_len, c_len))
    lower = row >= col
    strict = row > col
    eye = (row == col).astype(F32)

    load = lambda ref: ref[...].reshape(nb, c_len, A_HEAD)
    r, k, v, kk, b, logw = (load(ref) for ref in (r_ref, k_ref, v_ref, kk_ref, b_ref, lw_ref))
    w_hi, w_mid, w_lo = _split3(logw)
    lw = _bdot(tri_incl, w_hi) + _bdot(tri_incl, w_mid) + _bdot(tri_incl, w_lo)
    lw_prev = lw - logw
    lw_last = lw[:, c_len - 1:c_len, :]
    e_neg = jnp.exp(-lw)
    e_end = jnp.exp(lw_last - lw)
    kkm = (kk * jnp.exp(lw_prev)).astype(BF16)
    rm = r * jnp.exp(lw)
    kp = (k * e_neg).astype(BF16)
    bp = (b * e_neg).astype(BF16)
    kpp = (k * e_end).astype(BF16)
    bpp = (b * e_end).astype(BF16)
    vb = v.astype(BF16)
    rmb = rm.astype(BF16)

    a_vk = jnp.where(strict, _bdot_nt(kkm, kp), 0.0)
    a_pb = jnp.where(strict, _bdot_nt(kkm, bp), 0.0)
    rk = jnp.where(lower, _bdot_nt(rmb, kp), 0.0)
    rb = jnp.where(lower, _bdot_nt(rmb, bp), 0.0)

    n_pow = -a_pb
    t_inv = eye + n_pow
    for _ in range(5):
        npb = n_pow.astype(BF16)
        n_pow = _bdot(npb, npb)
        t_inv = t_inv + _bdot(t_inv.astype(BF16), n_pow.astype(BF16))
    tb = t_inv.astype(BF16)

    kktb = _bdot(tb, kkm).astype(BF16)
    pvb = _bdot(tb, _bdot(a_vk.astype(BF16), vb).astype(BF16)).astype(BF16)
    rbb = rb.astype(BF16)
    per_chunk = lambda x: x.reshape((hh, n_chunks) + x.shape[1:])
    m_mat = per_chunk(eye * jnp.exp(lw_last) - _bdot_tn(kktb, bpp))
    n_mat = per_chunk(_bdot_tn(vb, kpp) - _bdot_tn(pvb, bpp))
    q_mat = per_chunk((rm - _bdot(rbb, kktb)).astype(BF16))
    y0 = per_chunk(_bdot(rk.astype(BF16), vb) - _bdot(rbb, pvb))

    s = s_scr[...]
    for g in range(n_chunks):
        s_hi, s_lo = _split2(s)
        m_hi, m_lo = _split2(m_mat[:, g])
        y_ref[:, g * c_len:(g + 1) * c_len, :] = _bdot_nt(q_mat[:, g], s_hi) + y0[:, g]
        s = _bdot(s_hi, m_hi) + _bdot(s_lo, m_hi) + _bdot(s_hi, m_lo) + n_mat[:, g]
    s_scr[...] = s


def _rwkv_post(y, r, k, v, g, r_k, ln_w, ln_b, ones):
    def head_sum(x):
        hi, lo = _split2(x)
        return _dot(hi, ones) + _dot(lo, ones)

    mu = head_sum(y) * (1.0 / A_HEAD)
    yc = y - mu
    var = head_sum(yc * yc) * (1.0 / A_HEAD)
    yn = yc * lax.rsqrt(var + A_GN_EPS) * ln_w + ln_b
    bonus = head_sum(r * k * r_k) * v
    return ((yn + bonus) * g).astype(BF16)


def _rwkv_body(ur_ref, uk_ref, uv_ref, ul_ref, sr_ref, sk_ref, sv_ref, sl_ref,
               mu_ref, mul_ref, w0_ref, a0_ref, kk_ref, ka_ref, w2_ref, a2_ref, g2_ref, ones_ref,
               rk_ref, lnw_ref, lnb_ref, s0_ref,
               o_ref, sout_ref,
               cr, ck, cv, cl, s_scr, r_hm, k_hm, v_hm, kk_hm, b_hm, lw_hm, y_hm, *, n_chunks):
    @pl.when(pl.program_id(1) == 0)
    def _():
        s_scr[...] = s0_ref[...]

    r, k, v, kk, b, lw, g = _rwkv_prep(
        ur_ref, uk_ref, uv_ref, ul_ref, sr_ref, sk_ref, sv_ref, sl_ref, mu_ref, mul_ref, w0_ref, a0_ref,
        kk_ref, ka_ref, w2_ref, a2_ref, g2_ref, ones_ref, cr, ck, cv, cl)
    for x, ref in ((r, r_hm), (k, k_hm), (v, v_hm), (kk, kk_hm), (b, b_hm), (lw, lw_hm)):
        for h in range(A_HEADS):
            ref[h] = x[:, h * A_HEAD:(h + 1) * A_HEAD]
    _rwkv_chunks(r_hm, k_hm, v_hm, kk_hm, b_hm, lw_hm, y_hm, s_scr, n_chunks)
    sout_ref[...] = s_scr[...]
    y = jnp.concatenate([y_hm[h] for h in range(A_HEADS)], axis=1)
    o_ref[...] = _rwkv_post(y, r, k, v, g, rk_ref[...], lnw_ref[...], lnb_ref[...], ones_ref[...])


def _rwkv(u3, shift_prev, s0, p, tm):
    bn, t, _ = u3.shape
    sr = shift_prev[:, None, 0:W]
    sk = shift_prev[:, None, W:2 * W]
    sv = shift_prev[:, None, 2 * W:3 * W]
    sl = jnp.pad(shift_prev[:, None, 3 * W:], ((0, 0), (0, 0), (0, LORA_PAD - sum(A_LORA))))
    row = lambda c: pl.BlockSpec((None, 1, c), lambda b, i: (b, 0, 0))
    full = lambda a: pl.BlockSpec(a.shape, lambda b, i: (0,) * a.ndim)
    ublk = lambda slot: pl.BlockSpec((None, tm, W), lambda b, i, s=slot: (b, i, s))
    consts = (p["mu_rkv"], p["mu_l"], p["w0"], p["a0"], p["k_k"], p["k_a"],
              p["w2p"], p["a2p"], p["g2p"], p["head_ones"], p["r_k"], p["ln_w"], p["ln_b"])
    sblk = pl.BlockSpec((None, A_HEADS, A_HEAD, A_HEAD), lambda b, i: (b, 0, 0, 0))
    head_major = pltpu.VMEM((A_HEADS, tm, A_HEAD), F32)
    return pl.pallas_call(
        functools.partial(_rwkv_body, n_chunks=tm // RWKV_CHUNK),
        grid=(bn, t // tm),
        in_specs=[ublk(SLOT["ar"]), ublk(SLOT["ak"]), ublk(SLOT["av"]),
                  pl.BlockSpec((None, tm, LORA_PAD), lambda b, i: (b, i, COL_LORA // LORA_PAD)),
                  row(W), row(W), row(W), row(LORA_PAD)] + [full(c) for c in consts] + [sblk],
        out_specs=[pl.BlockSpec((None, tm, W), lambda b, i: (b, i, 0)), sblk],
        out_shape=[jax.ShapeDtypeStruct((bn, t, W), BF16),
                   jax.ShapeDtypeStruct((bn, A_HEADS, A_HEAD, A_HEAD), F32)],
        scratch_shapes=[pltpu.VMEM((8, W), F32)] * 3
                       + [pltpu.VMEM((8, LORA_PAD), F32), pltpu.VMEM((A_HEADS, A_HEAD, A_HEAD), F32)]
                       + [head_major] * 7,
        compiler_params=_params(("parallel", "arbitrary")),
        name="rwkv",
    )(u3, u3, u3, u3, sr, sk, sv, sl, *consts, s0)


def _pool_body(u_ref, hist_ref, w_ref, scale_ref, o_ref, carry, *, start_pos):
    tm = u_ref.shape[0]
    i = pl.program_id(1)

    @pl.when(i == 0)
    def _():
        carry[...] = hist_ref[...]

    x = u_ref[...]
    ext = jnp.concatenate([carry[...], x], axis=0)
    carry[...] = x[tm - 16:tm, :]
    s = ext
    sums = []
    for sh in (1, 2, 4, 8):
        s = s + pltpu.roll(s, sh, axis=0)
        sums.append(s[16:16 + tm, :])
    pos = start_pos + i * tm + lax.broadcasted_iota(jnp.int32, (tm, 1), 0)
    for gi, wlen in enumerate(B_WINDOWS):
        cs = slice(gi * LANE, (gi + 1) * LANE)
        cnt = jnp.minimum(wlen, pos + 1).astype(F32)
        d = sums[gi][:, cs] / cnt - x[:, cs]
        o_ref[:, cs] = (_dot(d.astype(BF16), w_ref[gi]) * scale_ref[:, cs]).astype(BF16)


def _pool(u3, hist, pool_w, pool_scale, start_pos, tm):
    bn, t, _ = u3.shape
    hist16 = jnp.pad(hist, ((0, 0), (1, 0), (0, 0)))
    return pl.pallas_call(
        functools.partial(_pool_body, start_pos=start_pos),
        grid=(bn, t // tm),
        in_specs=[pl.BlockSpec((None, tm, W), lambda b, i: (b, i, SLOT["b"])),
                  pl.BlockSpec((None, 16, W), lambda b, i: (b, 0, 0)),
                  pl.BlockSpec((4, LANE, LANE), lambda b, i: (0, 0, 0)),
                  pl.BlockSpec((1, W), lambda b, i: (0, 0))],
        out_specs=pl.BlockSpec((None, tm, W), lambda b, i: (b, i, 0)),
        out_shape=jax.ShapeDtypeStruct((bn, t, W), BF16),
        scratch_shapes=[pltpu.VMEM((16, W), F32)],
        compiler_params=_params(("parallel", "arbitrary")),
        name="pool",
    )(u3, hist16, pool_w.astype(BF16), pool_scale.reshape(1, W))


def _sb_body(q_ref, k_ref, v_ref, o_ref, acc_scr, o_scr, *, tq, q_off, n_masked):
    qi = pl.program_id(1)
    n_pairs = W // LANE
    q = q_ref[...] * (C_HEAD ** -0.5)
    q_start = q_off + qi * tq
    first_masked = q_start // SB_BLOCK
    even = lax.broadcasted_iota(jnp.int32, (tq, LANE), 1) < C_HEAD
    qz = jnp.zeros((tq, LANE), BF16)
    qs = jnp.stack([jnp.stack([jnp.where(even, q[:, p * LANE:(p + 1) * LANE], qz),
                               jnp.where(even, qz, q[:, p * LANE:(p + 1) * LANE])])
                    for p in range(n_pairs)])

    jj = lax.broadcasted_iota(jnp.int32, (2 * SB_BLOCK, 2 * SB_BLOCK), 0) % SB_BLOCK
    ss = lax.broadcasted_iota(jnp.int32, (2 * SB_BLOCK, 2 * SB_BLOCK), 1)
    cs = jnp.where(jnp.logical_or(ss >= SB_BLOCK, jj > ss), -1.0, 0.0).astype(BF16)
    aligned = q_off % SB_BLOCK == 0 and tq % SB_BLOCK == 0

    def block(kb, r0, r1, masked):
        n = r1 - r0
        ks = pl.multiple_of(kb * SB_BLOCK, SB_BLOCK)
        kblk = k_ref[pl.ds(ks, SB_BLOCK), :]
        vblk = v_ref[pl.ds(ks, SB_BLOCK), :]
        kst = jnp.stack([kblk[:, p * LANE:(p + 1) * LANE] for p in range(n_pairs)])
        vst = jnp.stack([vblk[:, p * LANE:(p + 1) * LANE] for p in range(n_pairs)])
        z = _bdot_nt(qs[:, :, r0:r1, :].reshape(n_pairs, 2 * n, LANE), kst).reshape(2 * n_pairs, n, SB_BLOCK)
        zp = jnp.maximum(z, 0.0)
        zn = jnp.minimum(z, 0.0)
        l1p = jnp.log(1.0 + jnp.exp(zn - zp))
        nlf = zp + l1p
        s = zn - l1p
        if masked:
            q_pos = q_start + r0 + lax.broadcasted_iota(jnp.int32, (n, SB_BLOCK), 0)
            mask = ((ks + lax.broadcasted_iota(jnp.int32, (n, SB_BLOCK), 1)) < q_pos)[None]
            nlf = jnp.where(mask, nlf, 0.0)
        hi, lo = _split2(nlf)
        lt = _dot(jnp.concatenate([hi, lo], axis=2).reshape(2 * n_pairs * n, 2 * SB_BLOCK), cs)
        lt = lt.reshape(2 * n_pairs, n, 2 * SB_BLOCK)
        att = jnp.exp(s + lt[:, :, :SB_BLOCK] + acc_scr[:, r0:r1, :])
        if masked:
            att = jnp.where(mask, att, 0.0)
        pv = _bdot(att.astype(BF16).reshape(n_pairs, 2 * n, SB_BLOCK), vst)
        o_scr[:, r0:r1, :] += pv.reshape(2 * n_pairs, n, LANE)
        acc_scr[:, r0:r1, :] += lt[:, :, SB_BLOCK:]

    acc_scr[...] = jnp.zeros_like(acc_scr)
    o_scr[...] = jnp.zeros_like(o_scr)
    for m in range(n_masked - 1, -1, -1):
        block(first_masked + m, m * SB_BLOCK if aligned else 0, tq, True)

    def sweep(i0, r0, r1, alive_rows):
        def live(i):
            return jnp.logical_and(i < first_masked, jnp.max(acc_scr[:, alive_rows, :]) > SB_DEAD)

        def step(i):
            block(first_masked - 1 - i, r0, r1, False)
            return i + 1

        return lax.while_loop(live, step, i0)

    split = SB_BLOCK if aligned and tq > SB_BLOCK else 0
    i_next = sweep(jnp.int32(0), 0, tq, slice(split, tq))
    if split:
        sweep(i_next, 0, split, slice(0, split))
    o = o_scr[...]
    o_ref[...] = jnp.concatenate([jnp.where(even, o[2 * p], o[2 * p + 1]) for p in range(n_pairs)],
                                 axis=1).astype(BF16)


def _to_bf16_body(*refs, n, n_f32):
    for x_ref, o_ref in zip(refs[:n], refs[n:2 * n]):
        o_ref[...] = x_ref[...].astype(BF16)
    for x_ref, o_ref in zip(refs[n - n_f32:n], refs[2 * n:]):
        o_ref[...] = x_ref[...]


def _to_bf16(x3, slots, tm, n_f32=0):
    bn, t, _ = x3.shape
    n = len(slots)
    oblk = pl.BlockSpec((None, tm, W), lambda b, i: (b, i, 0))
    return pl.pallas_call(
        functools.partial(_to_bf16_body, n=n, n_f32=n_f32),
        grid=(bn, t // tm),
        in_specs=[pl.BlockSpec((None, tm, W), lambda b, i, s=s: (b, i, s)) for s in slots],
        out_specs=[oblk] * (n + n_f32),
        out_shape=[jax.ShapeDtypeStruct((bn, t, W), BF16)] * n + [jax.ShapeDtypeStruct((bn, t, W), F32)] * n_f32,
        compiler_params=_params(("parallel", "parallel")),
        name="to_bf16",
    )(*([x3] * n))


def _sb_attention(q, k, v, q_off, tq):
    bn, t, _ = q.shape
    sk = k.shape[1]
    sk_pad = -(-sk // SB_BLOCK) * SB_BLOCK
    if sk_pad != sk:
        pad = ((0, 0), (0, sk_pad - sk), (0, 0))
        k, v = jnp.pad(k, pad), jnp.pad(v, pad)
    assert tq % SB_BLOCK == 0 or t == tq
    n_masked = -(-((q_off % SB_BLOCK) + tq) // SB_BLOCK)
    kv = pl.BlockSpec((None, sk_pad, W), lambda b, i: (b, 0, 0), pipeline_mode=pl.Buffered(1))
    qo = pl.BlockSpec((None, tq, W), lambda b, i: (b, i, 0))
    return pl.pallas_call(
        functools.partial(_sb_body, tq=tq, q_off=q_off, n_masked=n_masked),
        grid=(bn, t // tq),
        in_specs=[qo, kv, kv],
        out_specs=qo,
        out_shape=jax.ShapeDtypeStruct((bn, t, W), BF16),
        scratch_shapes=[pltpu.VMEM((C_HEADS, tq, SB_BLOCK), F32), pltpu.VMEM((C_HEADS, tq, LANE), F32)],
        compiler_params=_params(("parallel", "arbitrary")),
        name="sb_attention",
    )(q, k, v)


def _mlstm_body(qk_ref, v_ref, o_ref, g_ref, hist_ref, c0_ref, n0_ref, m0_ref,
                cw_ref, cb_ref, gb_ref, nw_ref,
                out_ref, conv_out, c_out, n_out, m_out,
                carry, q_scr, k_scr, c_scr, n_scr, m_scr, *, n_chunks):
    tm = qk_ref.shape[0]
    cl = tm // n_chunks

    @pl.when(pl.program_id(1) == 0)
    def _():
        carry[...] = hist_ref[...]
        c_scr[...] = c0_ref[...]
        n_scr[...] = n0_ref[...]
        m_scr[...] = m0_ref[...]

    x = qk_ref[...]
    ext = jnp.concatenate([carry[...], x], axis=0)
    carry[...] = x[tm - 8:tm, :]
    conv = cb_ref[...] + x * cw_ref[D_CONV - 1:D_CONV, :]
    for sh in range(1, D_CONV):
        conv = conv + pltpu.roll(ext, sh, axis=0)[8:8 + tm, :] * cw_ref[D_CONV - 1 - sh:D_CONV - sh, :]
    conv = conv * _sigmoid(conv)
    q_scr[...] = conv[:, :W]
    k_scr[...] = conv[:, W:] * (D_HEAD ** -0.5)

    row = lax.broadcasted_iota(jnp.int32, (cl, cl), 0)
    col = lax.broadcasted_iota(jnp.int32, (cl, cl), 1)
    causal = row >= col
    tri_incl = causal.astype(BF16)

    def chunk(c, carry_):
        off = pl.multiple_of(c * cl, cl)
        sl = pl.ds(off, cl)
        gpre = g_ref[sl, :] + gb_ref[...]
        lfa = _log_sigmoid(gpre)
        bcol = _dot_lhs01(tri_incl, lfa)
        g_t = gpre.T
        b_t = bcol.T
        heads = range(D_HEADS)
        hs = [slice(h * D_HEAD, (h + 1) * D_HEAD) for h in heads]
        q = jnp.stack([q_scr[sl, hs[h]] for h in heads])
        k = jnp.stack([k_scr[sl, hs[h]] for h in heads])
        v = jnp.stack([v_ref[sl, hs[h]] for h in heads])
        ig_col = jnp.stack([gpre[:, h:h + 1] for h in heads])
        b_col = jnp.stack([bcol[:, D_HEADS + h:D_HEADS + h + 1] for h in heads])
        ig_row = jnp.stack([g_t[h:h + 1, :] for h in heads])
        b_row = jnp.stack([b_t[D_HEADS + h:D_HEADS + h + 1, :] for h in heads])
        m_prev = jnp.stack([m_scr[h:h + 1, 0:1] for h in heads])
        n_prev = jnp.stack([n_scr[h:h + 1, :] for h in heads])
        c_prev = c_scr[...]
        log_d = jnp.where(causal[None], b_col - b_row + ig_row, -jnp.inf)
        m_inter = b_col + m_prev
        m_t = jnp.maximum(m_inter, jnp.max(log_d, axis=-1, keepdims=True))
        dmat = jnp.exp(log_d - m_t)
        inter = jnp.exp(m_inter - m_t)
        qb = q.astype(BF16)
        vb = v.astype(BF16)
        w_qk = _bdot_nt(qb, k.astype(BF16)) * dmat
        num = _bdot(w_qk.astype(BF16), vb) + inter * _bdot(qb, c_prev.astype(BF16))
        qn = jnp.sum(q * n_prev, axis=-1, keepdims=True)
        den = jnp.sum(w_qk, axis=-1, keepdims=True) + inter * qn
        den = jnp.maximum(jnp.abs(den), jnp.exp(-m_t))
        hv = num / den
        m_new = m_t[:, cl - 1:cl, :]
        b_last = b_col[:, cl - 1:cl, :]
        decay = jnp.exp(b_last + m_prev - m_new)
        w_s = jnp.exp(b_last - b_col + ig_col - m_new)
        kw = k * w_s
        c_scr[...] = decay * c_prev + _bdot_tn(kw.astype(BF16), vb)
        n_new = decay * n_prev + jnp.sum(kw, axis=1, keepdims=True)
        mu = jnp.mean(hv, axis=-1, keepdims=True)
        hc = hv - mu
        var = jnp.mean(hc * hc, axis=-1, keepdims=True)
        hn = hc * lax.rsqrt(var + D_NORM_EPS)
        for h in heads:
            n_scr[h:h + 1, :] = n_new[h]
            m_scr[h:h + 1, :] = jnp.broadcast_to(m_new[h], (1, LANE))
            out_ref[sl, hs[h]] = (_sigmoid(o_ref[sl, hs[h]]) * (hn[h] * nw_ref[:, hs[h]])).astype(BF16)
        return carry_

    lax.fori_loop(0, n_chunks, chunk, 0)
    conv_out[...] = carry[...]
    c_out[...] = c_scr[...]
    n_out[...] = n_scr[...]
    m_out[...] = m_scr[...]


def _mlstm(u3, conv_hist, c0, n0, m0, p, tm):
    bn, t, _ = u3.shape
    hist8 = jnp.pad(conv_hist, ((0, 0), (8 - (D_CONV - 1), 0), (0, 0)))
    n0p = jnp.pad(n0, ((0, 0), (0, 8 - D_HEADS), (0, 0)))
    m0p = jnp.pad(jnp.broadcast_to(m0[:, :, None], (bn, D_HEADS, LANE)), ((0, 0), (0, 8 - D_HEADS), (0, 0)))
    full = lambda a: pl.BlockSpec(a.shape, lambda b, i: (0,) * a.ndim)
    per_b = lambda *s: pl.BlockSpec((None,) + s, lambda b, i: (b,) + (0,) * len(s))
    consts = (p["conv_w"], p["conv_b"], p["gate_bias"], p["norm_w"])
    outs = pl.pallas_call(
        functools.partial(_mlstm_body, n_chunks=tm // min(MLSTM_CHUNK, tm)),
        grid=(bn, t // tm),
        in_specs=[pl.BlockSpec((None, tm, 2 * W), lambda b, i: (b, i, 0)),
                  pl.BlockSpec((None, tm, W), lambda b, i: (b, i, SLOT["dv"])),
                  pl.BlockSpec((None, tm, W), lambda b, i: (b, i, SLOT["do"])),
                  pl.BlockSpec((None, tm, LANE), lambda b, i: (b, i, COL_DG // LANE)),
                  per_b(8, 2 * W), per_b(D_HEADS, D_HEAD, D_HEAD), per_b(8, LANE), per_b(8, LANE)]
                 + [full(c) for c in consts],
        out_specs=[pl.BlockSpec((None, tm, W), lambda b, i: (b, i, 0)),
                   per_b(8, 2 * W), per_b(D_HEADS, D_HEAD, D_HEAD), per_b(8, LANE), per_b(8, LANE)],
        out_shape=[jax.ShapeDtypeStruct((bn, t, W), BF16),
                   jax.ShapeDtypeStruct((bn, 8, 2 * W), F32),
                   jax.ShapeDtypeStruct((bn, D_HEADS, D_HEAD, D_HEAD), F32),
                   jax.ShapeDtypeStruct((bn, 8, LANE), F32),
                   jax.ShapeDtypeStruct((bn, 8, LANE), F32)],
        scratch_shapes=[pltpu.VMEM((8, 2 * W), F32), pltpu.VMEM((tm, W), F32), pltpu.VMEM((tm, W), F32),
                        pltpu.VMEM((D_HEADS, D_HEAD, D_HEAD), F32), pltpu.VMEM((8, LANE), F32),
                        pltpu.VMEM((8, LANE), F32)],
        compiler_params=_params(("parallel", "arbitrary")),
        name="mlstm",
    )(u3, u3, u3, u3, hist8, c0, n0p, m0p, *consts)
    out_d, conv8, c, n8, m8 = outs
    return out_d, conv8[:, 8 - (D_CONV - 1):], c, n8[:, :D_HEADS], m8[:, :D_HEADS, 0]


def _merge_body(x_ref, g_ref, a_ref, b_ref, c_ref, d_ref, wg_ref, wb_ref, wo_ref, o_ref):
    g = g_ref[...].astype(BF16)
    merged = None
    for i, br in enumerate((a_ref, b_ref, c_ref, d_ref)):
        term = _sigmoid(_dot(g, wg_ref[i])) * _dot(br[...], wb_ref[i])
        merged = term if merged is None else merged + term
    o_ref[...] = x_ref[...] + _dot(merged.astype(BF16), wo_ref[...])


def _merge(x3, u3, out_a, out_b, out_c, out_d, wg, wb, wo, tm):
    bn, t, _ = x3.shape
    once = lambda a: pl.BlockSpec(a.shape, lambda b, i: (0,) * a.ndim, pipeline_mode=pl.Buffered(1))
    bblk = pl.BlockSpec((None, tm, W), lambda b, i: (b, i, 0))
    xblk = pl.BlockSpec((None, tm, D_MODEL), lambda b, i: (b, i, 0))
    return pl.pallas_call(
        _merge_body,
        grid=(bn, t // tm),
        in_specs=[xblk, pl.BlockSpec((None, tm, GATE_RANK), lambda b, i: (b, i, COL_G // GATE_RANK)),
                  bblk, bblk, bblk, bblk, once(wg), once(wb), once(wo)],
        out_specs=xblk,
        out_shape=jax.ShapeDtypeStruct((bn, t, D_MODEL), F32),
        compiler_params=_params(("parallel", "parallel")),
        name="merge",
    )(x3, u3, out_a, out_b, out_c, out_d, wg, wb, wo)


def _ffn_epilogue(acc, gf_ref, final_norm):
    return _rmsnorm(acc, gf_ref[...]) if final_norm else acc


def _ffn_dense_body(x_ref, g_ref, wg_ref, wu_ref, wd_ref, gf_ref, o_ref, h_scr, acc, *, final_norm):
    j = pl.program_id(1)

    @pl.when(j == 0)
    def _():
        x = x_ref[...]
        h_scr[...] = _rmsnorm(x, g_ref[...]).astype(BF16)
        acc[...] = x

    h = h_scr[...]
    a = _dot(h, wg_ref[...])
    act = (a * _sigmoid(a)) * _dot(h, wu_ref[...])
    acc[...] += _dot(act.astype(BF16), wd_ref[...])

    @pl.when(j == pl.num_programs(1) - 1)
    def _():
        o_ref[...] = _ffn_epilogue(acc[...], gf_ref, final_norm)


def _ffn_dense(x2d, g, wgate, wup, wdown, g_final, final_norm, tm, tf=512):
    n = x2d.shape[0]
    xblk = pl.BlockSpec((tm, D_MODEL), lambda i, j: (i, 0))
    vec = pl.BlockSpec((1, D_MODEL), lambda i, j: (0, 0))
    return pl.pallas_call(
        functools.partial(_ffn_dense_body, final_norm=final_norm),
        grid=(n // tm, D_FF // tf),
        in_specs=[xblk, vec,
                  pl.BlockSpec((D_MODEL, tf), lambda i, j: (0, j)),
                  pl.BlockSpec((D_MODEL, tf), lambda i, j: (0, j)),
                  pl.BlockSpec((tf, D_MODEL), lambda i, j: (j, 0)), vec],
        out_specs=xblk,
        out_shape=jax.ShapeDtypeStruct((n, D_MODEL), F32),
        scratch_shapes=[pltpu.VMEM((tm, D_MODEL), BF16), pltpu.VMEM((tm, D_MODEL), F32)],
        compiler_params=_params(("parallel", "arbitrary")),
        name="ffn_dense",
    )(x2d, g.reshape(1, D_MODEL), wgate, wup, wdown, g_final.reshape(1, D_MODEL))


def _ffn_moe_body(x_ref, g_ref, rt_ref, wg_ref, wu_ref, wd_ref, gf_ref, o_ref,
                  h_scr, acc, gates, *, final_norm):
    e = pl.program_id(1)

    @pl.when(e == 0)
    def _():
        x = x_ref[...]
        h = _rmsnorm(x, g_ref[...])
        h_scr[...] = h.astype(BF16)
        acc[...] = x
        h_hi, h_lo = _split2(h)
        r_hi, r_lo = _split2(rt_ref[...])
        logits = _dot(h_hi, r_hi) + _dot(h_lo, r_hi) + _dot(h_hi, r_lo)
        lane = lax.broadcasted_iota(jnp.int32, logits.shape, 1)
        logits = jnp.where(lane < N_EXPERTS, logits, -jnp.inf)
        v1 = jnp.max(logits, axis=-1, keepdims=True)
        i1 = jnp.min(jnp.where(logits == v1, lane, LANE), axis=-1, keepdims=True)
        rest = jnp.where(lane == i1, -jnp.inf, logits)
        v2 = jnp.max(rest, axis=-1, keepdims=True)
        i2 = jnp.min(jnp.where(rest == v2, lane, LANE), axis=-1, keepdims=True)
        e2 = jnp.exp(v2 - v1)
        den = 1.0 + e2
        gates[...] = jnp.where(lane == i1, 1.0 / den, 0.0) + jnp.where(lane == i2, e2 / den, 0.0)

    h = h_scr[...]
    lane = lax.broadcasted_iota(jnp.int32, gates.shape, 1)
    gate = jnp.sum(jnp.where(lane == e, gates[...], 0.0), axis=-1, keepdims=True)
    a = _dot(h, wg_ref[...])
    act = (a * _sigmoid(a)) * _dot(h, wu_ref[...])
    acc[...] += gate * _dot(act.astype(BF16), wd_ref[...])

    @pl.when(e == pl.num_programs(1) - 1)
    def _():
        o_ref[...] = _ffn_epilogue(acc[...], gf_ref, final_norm)


def _ffn_moe(x2d, g, router, wgate, wup, wdown, g_final, final_norm, tm):
    n = x2d.shape[0]
    xblk = pl.BlockSpec((tm, D_MODEL), lambda i, j: (i, 0))
    vec = pl.BlockSpec((1, D_MODEL), lambda i, j: (0, 0))
    return pl.pallas_call(
        functools.partial(_ffn_moe_body, final_norm=final_norm),
        grid=(n // tm, N_EXPERTS),
        in_specs=[xblk, vec,
                  pl.BlockSpec((D_MODEL, LANE), lambda i, j: (0, 0)),
                  pl.BlockSpec((None, D_MODEL, EXPERT_PAD), lambda i, j: (j, 0, 0)),
                  pl.BlockSpec((None, D_MODEL, EXPERT_PAD), lambda i, j: (j, 0, 0)),
                  pl.BlockSpec((None, EXPERT_PAD, D_MODEL), lambda i, j: (j, 0, 0)), vec],
        out_specs=xblk,
        out_shape=jax.ShapeDtypeStruct((n, D_MODEL), F32),
        scratch_shapes=[pltpu.VMEM((tm, D_MODEL), BF16), pltpu.VMEM((tm, D_MODEL), F32),
                        pltpu.VMEM((tm, LANE), F32)],
        compiler_params=_params(("parallel", "arbitrary")),
        name="ffn_moe",
    )(x2d, g.reshape(1, D_MODEL), router, wgate, wup, wdown, g_final.reshape(1, D_MODEL))


def _pack_w_in(w):
    a, b, c, d = w[:, :OFF_B], w[:, OFF_B:OFF_C], w[:, OFF_C:OFF_D], w[:, OFF_D:OFF_G]
    g = w[:, OFF_G:]
    zeros = lambda k: jnp.zeros((D_MODEL, k), w.dtype)
    lora = a[:, 3 * W:]
    gates = d[:, 4 * W:]
    cols = [d[:, :4 * W], a[:, :3 * W], b, c,
            lora, zeros(LORA_PAD - lora.shape[1]), g,
            gates, zeros(LANE - gates.shape[1]), zeros(WP - COL_DG - LANE)]
    return jnp.concatenate(cols, axis=1).astype(BF16)


def _layer_params(l, wts):
    (norm_mix, norm_ffn, norm_final, w_in, rwkv_mu, rwkv_w0, rwkv_w2, rwkv_a0, rwkv_a2,
     rwkv_g2, rwkv_k_k, rwkv_k_a, rwkv_r_k, rwkv_ln_w, rwkv_ln_b, pool_w, pool_scale,
     mlstm_conv_w, mlstm_conv_b, mlstm_i_bias, mlstm_f_bias, mlstm_norm_w,
     w_branch, w_merge_gate, w_out, ffn_w_gate, ffn_w_up, ffn_w_down,
     moe_router, moe_w_gate, moe_w_up, moe_w_down) = wts
    n_l = sum(A_LORA)
    row = lambda v: v.reshape(1, -1)

    def lora_pad(wm, start):
        return jnp.pad(wm, ((start, LORA_PAD - start - wm.shape[0]), (0, 0))).astype(BF16)

    hid = jnp.arange(W) // A_HEAD
    p = {
        "norm_mix": norm_mix[l], "norm_ffn": norm_ffn[l],
        "wp": _pack_w_in(w_in[l]),
        "rwkv": {
            "mu_rkv": rwkv_mu[l, :3 * W].reshape(3, W),
            "mu_l": jnp.pad(rwkv_mu[l, 3 * W:], (0, LORA_PAD - n_l)).reshape(1, LORA_PAD),
            "w0": row(rwkv_w0[l]), "a0": row(rwkv_a0[l]),
            "k_k": row(rwkv_k_k[l]), "k_a": row(rwkv_k_a[l]),
            "w2p": lora_pad(rwkv_w2[l], 0),
            "a2p": lora_pad(rwkv_a2[l], A_LORA[0]),
            "g2p": lora_pad(rwkv_g2[l], A_LORA[0] + A_LORA[1]),
            "head_ones": (hid[:, None] == hid[None, :]).astype(BF16),
            "r_k": row(rwkv_r_k[l]), "ln_w": row(rwkv_ln_w[l]), "ln_b": row(rwkv_ln_b[l]),
        },
        "pool_w": pool_w[l], "pool_scale": pool_scale[l],
        "mlstm": {
            "conv_w": mlstm_conv_w[l], "conv_b": row(mlstm_conv_b[l]),
            "gate_bias": jnp.pad(jnp.concatenate([mlstm_i_bias[l], mlstm_f_bias[l]]),
                                 (0, LANE - 2 * D_HEADS)).reshape(1, LANE),
            "norm_w": row(mlstm_norm_w[l]),
        },
        "wg": w_merge_gate[l].astype(BF16), "wb": w_branch[l].astype(BF16), "wo": w_out[l].astype(BF16),
    }
    if l % 2 == 0:
        p["ffn"] = (ffn_w_gate[l // 2].astype(BF16), ffn_w_up[l // 2].astype(BF16),
                    ffn_w_down[l // 2].astype(BF16))
    else:
        pe = EXPERT_PAD - D_FF_EXPERT
        p["moe"] = (jnp.pad(moe_router[l // 2], ((0, 0), (0, LANE - N_EXPERTS))),
                    jnp.pad(moe_w_gate[l // 2], ((0, 0), (0, 0), (0, pe))).astype(BF16),
                    jnp.pad(moe_w_up[l // 2], ((0, 0), (0, 0), (0, pe))).astype(BF16),
                    jnp.pad(moe_w_down[l // 2], ((0, 0), (0, pe), (0, 0))).astype(BF16))
    return p


def _slot(u3, name):
    s = SLOT[name] * W
    return u3[:, :, s:s + W]


def _run_trunk(x, start_pos, cache_k, cache_v, wkv0, shift0, pool0, conv0, c0, n0, m0,
               layers, norm_final):
    bn, t, _ = x.shape
    n = bn * t
    tm_tok = min(TOK_TILE, t)
    tm_row = min(ROW_TILE, n)
    tm_cast = min(ROW_TILE, t)
    x2 = x.reshape(n, D_MODEL)
    ks, vs, wkvs, shifts, pools, convs, cs, ns, ms = ([] for _ in range(9))
    for l, p in enumerate(layers):
        u2 = _inproj(x2, p["norm_mix"], p["wp"])
        u3 = u2.reshape(bn, t, WP)

        out_a, wkv = _rwkv(u3, shift0[l], wkv0[l], p["rwkv"], tm_tok)

        out_b = _pool(u3, pool0[l], p["pool_w"], p["pool_scale"], start_pos, tm_tok)

        q_b, k_b, v_b, k_c, v_c = _to_bf16(u3, (SLOT["cq"], SLOT["ck"], SLOT["cv"]), tm_cast, n_f32=2)
        if cache_k is None:
            out_c = _sb_attention(q_b, k_b, v_b, 0, min(SB_TQ, t))
        else:
            past = cache_k.shape[2]
            kc_b = cache_k[l].astype(BF16).reshape(bn, past, W)
            vc_b = cache_v[l].astype(BF16).reshape(bn, past, W)
            out_c = _sb_attention(q_b, jnp.concatenate([kc_b, k_b], axis=1),
                                  jnp.concatenate([vc_b, v_b], axis=1), past, t)

        out_d, conv_new, c_new, n_new, m_new = _mlstm(u3, conv0[l], c0[l], n0[l], m0[l], p["mlstm"], tm_tok)

        rows = lambda a: a.reshape(1, n, a.shape[-1])
        x2 = _merge(rows(x2), rows(u3), rows(out_a), rows(out_b), rows(out_c), rows(out_d),
                    p["wg"], p["wb"], p["wo"], min(TOK_TILE, n)).reshape(n, D_MODEL)
        last = l == len(layers) - 1
        if "ffn" in p:
            x2 = _ffn_dense(x2, p["norm_ffn"], *p["ffn"], norm_final, last, tm_row)
        else:
            x2 = _ffn_moe(x2, p["norm_ffn"], *p["moe"], norm_final, last, tm_row)

        ks.append(k_c.reshape(bn, t, C_HEADS, C_HEAD))
        vs.append(v_c.reshape(bn, t, C_HEADS, C_HEAD))
        wkvs.append(wkv)
        u_last = u3[:, -1]
        shifts.append(jnp.concatenate(
            [u_last[:, SLOT["ar"] * W:SLOT["ar"] * W + 3 * W], u_last[:, COL_LORA:COL_LORA + sum(A_LORA)]], axis=1))
        pools.append(jnp.concatenate([pool0[l], _slot(u3, "b")[:, -B_HIST:]], axis=1)[:, -B_HIST:])
        convs.append(conv_new)
        cs.append(c_new)
        ns.append(n_new)
        ms.append(m_new)
    st = lambda lst: jnp.stack(lst, axis=0)
    return (x2.reshape(bn, t, D_MODEL),
            (st(ks), st(vs), st(wkvs), st(shifts), st(pools), st(convs), st(cs), st(ns), st(ms)))


def kernel(x_prompt, x_sample, cache_sb_k, cache_sb_v, state_rwkv_wkv, state_rwkv_shift, state_pool,
           state_mlstm_conv, state_mlstm_c, state_mlstm_n, state_mlstm_m, norm_mix, norm_ffn, norm_final,
           w_in, rwkv_mu, rwkv_w0, rwkv_w2, rwkv_a0, rwkv_a2, rwkv_g2, rwkv_k_k, rwkv_k_a, rwkv_r_k,
           rwkv_ln_w, rwkv_ln_b, pool_w, pool_scale, mlstm_conv_w, mlstm_conv_b, mlstm_i_bias,
           mlstm_f_bias, mlstm_norm_w, w_branch, w_merge_gate, w_out, ffn_w_gate, ffn_w_up, ffn_w_down,
           moe_router, moe_w_gate, moe_w_up, moe_w_down):
    wts = (norm_mix, norm_ffn, norm_final, w_in, rwkv_mu, rwkv_w0, rwkv_w2, rwkv_a0, rwkv_a2,
           rwkv_g2, rwkv_k_k, rwkv_k_a, rwkv_r_k, rwkv_ln_w, rwkv_ln_b, pool_w, pool_scale,
           mlstm_conv_w, mlstm_conv_b, mlstm_i_bias, mlstm_f_bias, mlstm_norm_w,
           w_branch, w_merge_gate, w_out, ffn_w_gate, ffn_w_up, ffn_w_down,
           moe_router, moe_w_gate, moe_w_up, moe_w_down)
    depth = w_in.shape[0]
    layers = [_layer_params(l, wts) for l in range(depth)]
    bp = x_prompt.shape[0]
    z = lambda *s: jnp.zeros((depth, bp) + s, F32)
    y_p, st_p = _run_trunk(
        x_prompt, 0, None, None, z(A_HEADS, A_HEAD, A_HEAD), z(A_IN), z(B_HIST, W),
        z(D_CONV - 1, 2 * W), z(D_HEADS, D_HEAD, D_HEAD), z(D_HEADS, D_HEAD), z(D_HEADS),
        layers, norm_final)
    y_s, st_s = _run_trunk(
        x_sample, cache_sb_k.shape[2], cache_sb_k, cache_sb_v, state_rwkv_wkv, state_rwkv_shift,
        state_pool, state_mlstm_conv, state_mlstm_c, state_mlstm_n, state_mlstm_m,
        layers, norm_final)
    return (y_p, y_s) + tuple(st_p) + tuple(st_s)
```
